```python
import jax, jax.numpy as jnp
from jax import lax
import numpy as np

D_MODEL = 1024
BATCH = 8
SEQ = 4096
DEPTH = 1

CHUNK = 64
Q_BLOCK = 128
LN_EPS = 1e-5
RMS_EPS = 1e-6

MLA_HEADS = 8
MLA_NOPE = 64
MLA_ROPE = 32
MLA_V = 64
MLA_QK = MLA_NOPE + MLA_ROPE
MLA_Q_RANK = 256
MLA_KV_RANK = 128
MLA_WIDTH = MLA_HEADS * MLA_V
ROPE_THETA = 10000.0

RWKV_HEADS = 8
RWKV_HEAD = 64
RWKV_WIDTH = RWKV_HEADS * RWKV_HEAD
DECAY_LORA = 64
ICLR_LORA = 64
RWKV_SHIFT_WIDTH = 3 * RWKV_WIDTH + DECAY_LORA + ICLR_LORA
GN_EPS = 64e-5

ALPHA = (2.0 * DEPTH) ** 0.25
BETA = (8.0 * DEPTH) ** -0.25

IN_SPLITS = (MLA_Q_RANK, MLA_KV_RANK, MLA_ROPE, MLA_WIDTH,
             RWKV_SHIFT_WIDTH, RWKV_WIDTH,
             D_MODEL, D_MODEL)
IN_WIDTH = sum(IN_SPLITS)

kernel_name = 'hybrid_mla_rwkv7_gated_deepnorm_block'


def _split(t, sizes):
    return jnp.split(t, [int(i) for i in np.cumsum(sizes)[:-1]], axis=-1)


def layer_norm(x):
    x = x.astype(jnp.float32)
    xc = x - jnp.mean(x, -1, keepdims=True)
    return xc * lax.rsqrt(jnp.mean(xc * xc, -1, keepdims=True) + LN_EPS)


def rms_norm(x, g):
    x32 = x.astype(jnp.float32)
    y = x32 * lax.rsqrt(jnp.mean(x32 * x32, -1, keepdims=True) + RMS_EPS)
    return (y * g).astype(x.dtype)


def rope_tables(positions):
    inv = ROPE_THETA ** (-jnp.arange(0, MLA_ROPE, 2, dtype=jnp.float32) / MLA_ROPE)
    ang = positions.astype(jnp.float32)[..., None] * inv
    return jnp.cos(ang)[:, :, None, :], jnp.sin(ang)[:, :, None, :]


def apply_rope(t, cos, sin):
    t1, t2 = jnp.split(t.astype(jnp.float32), 2, axis=-1)
    return jnp.concatenate([t1 * cos - t2 * sin, t1 * sin + t2 * cos], -1).astype(t.dtype)


def token_shift(u, mu):
    u_prev = jnp.pad(u, ((0, 0), (1, 0), (0, 0)))[:, :-1]
    return u + (u_prev - u) * mu


def chunk_causal_attention(q, k, v):
    B, S, H, Dk = q.shape
    nb = S // Q_BLOCK
    qb = q.reshape(B, nb, Q_BLOCK, H, Dk).transpose(1, 0, 2, 3, 4)
    key_chunk = jnp.arange(S) // CHUNK
    scale = Dk ** -0.5

    def block(args):
        qi, bi = args
        s = jnp.einsum('bqhd,bkhd->bhqk', qi, k).astype(jnp.float32) * scale
        q_chunk = (bi * Q_BLOCK + jnp.arange(Q_BLOCK)) // CHUNK
        mask = key_chunk[None, :] <= q_chunk[:, None]
        p = jax.nn.softmax(jnp.where(mask, s, -jnp.inf), axis=-1).astype(v.dtype)
        return jnp.einsum('bhqk,bkhd->bqhd', p, v)

    out = lax.map(block, (qb, jnp.arange(nb)))
    return out.transpose(1, 0, 2, 3, 4).reshape(B, S, H, v.shape[-1])


def wkv7(r, w, k, v, a, b):
    B, S, H, N = r.shape

    def step(state, inp):
        r_t, w_t, k_t, v_t, a_t, b_t = inp
        sa = jnp.einsum('bhij,bhj->bhi', state, a_t)
        state = (state * w_t[:, :, None, :] + sa[..., None] * b_t[:, :, None, :]
                 + v_t[..., None] * k_t[:, :, None, :])
        return state, jnp.einsum('bhij,bhj->bhi', state, r_t)

    xs = tuple(jnp.moveaxis(t.astype(jnp.float32), 1, 0) for t in (r, w, k, v, a, b))
    _, y = lax.scan(step, jnp.zeros((B, H, N, N), jnp.float32), xs)
    return jnp.moveaxis(y, 0, 1)


def mla_branch(q_c, kv_c, k_rope, cos, sin, q_norm_g, w_uq, kv_norm_g, w_ukv):
    B, S, _ = q_c.shape
    q = (rms_norm(q_c, q_norm_g) @ w_uq).reshape(B, S, MLA_HEADS, MLA_QK)
    kv = (rms_norm(kv_c, kv_norm_g) @ w_ukv).reshape(B, S, MLA_HEADS, MLA_NOPE + MLA_V)
    q_nope, q_pe = jnp.split(q, [MLA_NOPE], axis=-1)
    k_nope, v = jnp.split(kv, [MLA_NOPE], axis=-1)
    q_pe = apply_rope(q_pe, cos, sin)
    k_pe = apply_rope(k_rope[:, :, None, :], cos, sin)
    q = jnp.concatenate([q_nope, q_pe], -1)
    k = jnp.concatenate([k_nope, jnp.broadcast_to(k_pe, (B, S, MLA_HEADS, MLA_ROPE))], -1)
    return chunk_causal_attention(q, k, v).reshape(B, S, MLA_WIDTH)


def rwkv7_branch(u, w0, w_decay_up, a0, w_iclr_up, k_k, k_a, r_k, gn_g, gn_b):
    B, S, _ = u.shape
    r, k, v, wd, ad = _split(u, (RWKV_WIDTH, RWKV_WIDTH, RWKV_WIDTH, DECAY_LORA, ICLR_LORA))
    w_log = -jax.nn.softplus(-(w0 + jnp.tanh(wd) @ w_decay_up)) - 0.5
    decay = jnp.exp(-jnp.exp(w_log.astype(jnp.float32)))
    a = jax.nn.sigmoid(a0 + ad @ w_iclr_up)
    hs = lambda t: t.reshape(B, S, RWKV_HEADS, RWKV_HEAD)
    kk = hs(k * k_k).astype(jnp.float32)
    kk = kk / jnp.maximum(jnp.sqrt(jnp.sum(kk * kk, -1, keepdims=True)), 1e-12)
    k = k * (1 + (a - 1) * k_a)
    r_h, k_h, v_h, a_h = hs(r), hs(k), hs(v), hs(a)
    y = wkv7(r_h, hs(decay), k_h, v_h, -kk, kk * a_h)
    y = y - jnp.mean(y, -1, keepdims=True)
    y = y * lax.rsqrt(jnp.mean(y * y, -1, keepdims=True) + GN_EPS)
    y = y.reshape(B, S, RWKV_WIDTH) * gn_g + gn_b
    bonus = jnp.sum((r_h * k_h * r_k).astype(jnp.float32), -1, keepdims=True) * v_h
    return (y + bonus.reshape(B, S, RWKV_WIDTH)).astype(u.dtype)


def hybrid_layer(x, c, cos, sin, w_ada, b_ada, w_in, q_norm_g, w_uq, kv_norm_g, w_ukv,
                 mu_rwkv, w0, w_decay_up, a0, w_iclr_up, k_k, k_a, r_k, gn_g, gn_b,
                 w_proj_a, w_proj_b, w_out, post_g, post_b):
    dt = x.dtype
    shift, scale, gate = jnp.split(jax.nn.silu(c) @ w_ada + b_ada, 3, axis=-1)
    h = (layer_norm(x) * (1 + scale[:, None]) + shift[:, None]).astype(dt)
    proj = h @ w_in
    q_c, kv_c, k_rope, gpath_a, rwkv_in, gpath_b, merge_a, merge_b = _split(proj, IN_SPLITS)
    y_a = mla_branch(q_c, kv_c, k_rope, cos, sin, q_norm_g, w_uq, kv_norm_g, w_ukv)
    y_b = rwkv7_branch(token_shift(rwkv_in, mu_rwkv), w0, w_decay_up, a0, w_iclr_up,
                       k_k, k_a, r_k, gn_g, gn_b)
    y_a = (y_a * jax.nn.silu(gpath_a)) @ w_proj_a
    y_b = (y_b * jax.nn.silu(gpath_b)) @ w_proj_b
    merged = jax.nn.sigmoid(merge_a) * y_a + jax.nn.sigmoid(merge_b) * y_b
    sub = merged @ w_out
    out = layer_norm(ALPHA * x + (1 + gate[:, None]) * sub) * post_g + post_b
    return out.astype(dt)


def setup_inputs(seed: int = 0) -> dict:
    key = jax.random.key(seed)
    k = jax.random.split(key, 32)

    def nrm(i, shape, fan_in, gain=1.0):
        return jax.random.normal(k[i], (DEPTH,) + shape, jnp.float32) * (gain * fan_in ** -0.5)

    def near(i, shape, center, spread=0.02):
        return center + spread * jax.random.normal(k[i], (DEPTH,) + shape, jnp.float32)

    x = jax.random.normal(k[0], (BATCH, SEQ, D_MODEL), jnp.float32)
    c = jax.random.normal(k[1], (BATCH, D_MODEL), jnp.float32)
    positions = (jax.random.randint(k[2], (BATCH, 1), 0, 8192, dtype=jnp.int32)
                 + jnp.arange(SEQ, dtype=jnp.int32)[None, :])
    decay_base = -6.0 + 5.0 * jnp.linspace(0.0, 1.0, RWKV_WIDTH, dtype=jnp.float32) ** 0.9
    return {
        'x': x,
        'c': c,
        'positions': positions,
        'w_ada': nrm(3, (D_MODEL, 3 * D_MODEL), D_MODEL, 0.2),
        'b_ada': near(4, (3 * D_MODEL,), 0.0),
        'w_in': nrm(5, (D_MODEL, IN_WIDTH), D_MODEL),
        'q_norm_g': near(6, (MLA_Q_RANK,), 1.0),
        'w_uq': nrm(7, (MLA_Q_RANK, MLA_HEADS * MLA_QK), MLA_Q_RANK),
        'kv_norm_g': near(8, (MLA_KV_RANK,), 1.0),
        'w_ukv': nrm(9, (MLA_KV_RANK, MLA_HEADS * (MLA_NOPE + MLA_V)), MLA_KV_RANK),
        'mu_rwkv': jax.random.uniform(k[10], (DEPTH, RWKV_SHIFT_WIDTH), jnp.float32),
        'w0': decay_base + near(11, (RWKV_WIDTH,), 0.0, 0.1),
        'w_decay_up': nrm(12, (DECAY_LORA, RWKV_WIDTH), DECAY_LORA),
        'a0': near(13, (RWKV_WIDTH,), 0.0, 0.1),
        'w_iclr_up': nrm(14, (ICLR_LORA, RWKV_WIDTH), ICLR_LORA),
        'k_k': near(15, (RWKV_WIDTH,), 0.85),
        'k_a': near(16, (RWKV_WIDTH,), 1.0),
        'r_k': near(17, (RWKV_HEADS, RWKV_HEAD), 0.0, 0.1),
        'gn_g': near(18, (RWKV_WIDTH,), 1.0),
        'gn_b': near(19, (RWKV_WIDTH,), 0.0),
        'w_proj_a': nrm(20, (MLA_WIDTH, D_MODEL), MLA_WIDTH, BETA),
        'w_proj_b': nrm(21, (RWKV_WIDTH, D_MODEL), RWKV_WIDTH, BETA),
        'w_out': nrm(22, (D_MODEL, D_MODEL), D_MODEL, BETA),
        'post_g': near(23, (D_MODEL,), 1.0),
        'post_b': near(24, (D_MODEL,), 0.0),
    }


def reference(x, c, positions, w_ada, b_ada, w_in, q_norm_g, w_uq, kv_norm_g, w_ukv,
              mu_rwkv, w0, w_decay_up, a0, w_iclr_up, k_k, k_a, r_k, gn_g, gn_b,
              w_proj_a, w_proj_b, w_out, post_g, post_b):
    cos, sin = rope_tables(positions)
    for l in range(DEPTH):
        x = hybrid_layer(x, c, cos, sin, w_ada[l], b_ada[l], w_in[l], q_norm_g[l], w_uq[l],
                         kv_norm_g[l], w_ukv[l], mu_rwkv[l], w0[l], w_decay_up[l], a0[l],
                         w_iclr_up[l], k_k[l], k_a[l], r_k[l], gn_g[l], gn_b[l],
                         w_proj_a[l], w_proj_b[l], w_out[l], post_g[l], post_b[l])
    return x
```

```python
import functools
import math

import jax
import jax.numpy as jnp
import numpy as np
from jax import lax
from jax.experimental import pallas as pl
from jax.experimental.pallas import tpu as pltpu

F32 = jnp.float32
BF16 = jnp.bfloat16

D_MODEL = 1024
LN_EPS = 1e-5
RMS_EPS = 1e-6
GN_EPS = 64e-5

MLA_HEADS = 8
MLA_NOPE = 64
MLA_ROPE = 32
MLA_V = 64
MLA_QK = MLA_NOPE + MLA_ROPE
MLA_Q_RANK = 256
MLA_KV_RANK = 128
MLA_WIDTH = MLA_HEADS * MLA_V
ROPE_THETA = 10000.0
ATTN_CHUNK = 64

RWKV_HEADS = 8
RWKV_HEAD = 64
RWKV_WIDTH = RWKV_HEADS * RWKV_HEAD
DECAY_LORA = 64
ICLR_LORA = 64
RWKV_SHIFT_WIDTH = 3 * RWKV_WIDTH + DECAY_LORA + ICLR_LORA
WKV_CHUNK = 64

DEPTH = 1
ALPHA = (2.0 * DEPTH) ** 0.25

LANES = 128
HEAD_PAD = 128
PACK_HEADS = 2
PACK_LANES = PACK_HEADS * RWKV_HEAD

QKR_WIDTH = 512
VMEM_LIMIT = 56 * 1024 * 1024

NEG_BIG = -1e30


def _const_spec(shape):
    n = len(shape)
    return pl.BlockSpec(shape, lambda *_: (0,) * n)


def _params(*sem):
    return pltpu.CompilerParams(dimension_semantics=sem, vmem_limit_bytes=VMEM_LIMIT)


def _adaln_kernel(c_ref, w_ref, b_ref, o_ref):
    c = c_ref[...]
    sc = c * jax.nn.sigmoid(c)
    o_ref[...] = jnp.dot(sc.astype(BF16), w_ref[...], preferred_element_type=F32) + b_ref[...]


def _adaln(c, w_ada, b_ada):
    b = c.shape[0]
    return pl.pallas_call(
        _adaln_kernel,
        out_shape=jax.ShapeDtypeStruct((b, 3 * D_MODEL), F32),
        compiler_params=pltpu.CompilerParams(vmem_limit_bytes=VMEM_LIMIT),
    )(c, w_ada.astype(BF16), b_ada.reshape(1, -1))


_PROJ_WIDTHS = (QKR_WIDTH, MLA_WIDTH, RWKV_SHIFT_WIDTH, RWKV_WIDTH, D_MODEL, D_MODEL)


def _ln_proj_kernel(x_ref, scale_ref, shift_ref, w_ref, *out_refs):
    x = x_ref[0]
    xc = x - jnp.mean(x, -1, keepdims=True)
    h = xc * lax.rsqrt(jnp.mean(xc * xc, -1, keepdims=True) + LN_EPS)
    hb = (h * (1.0 + scale_ref[0]) + shift_ref[0]).astype(BF16)
    off = 0
    for o_ref, width in zip(out_refs, _PROJ_WIDTHS):
        o_ref[0] = jnp.dot(hb, w_ref[:, off:off + width], preferred_element_type=F32).astype(o_ref.dtype)
        off += width


def _ln_proj(x, scale, shift, w_in_pad, tm):
    b, s, d = x.shape
    tok = lambda w: pl.BlockSpec((1, tm, w), lambda bi, i: (bi, i, 0))
    vec = pl.BlockSpec((1, 1, d), lambda bi, i: (bi, 0, 0))
    return pl.pallas_call(
        _ln_proj_kernel,
        grid=(b, s // tm),
        in_specs=[tok(d), vec, vec, _const_spec(w_in_pad.shape)],
        out_specs=[tok(w) for w in _PROJ_WIDTHS],
        out_shape=[jax.ShapeDtypeStruct((b, s, w), BF16) for w in _PROJ_WIDTHS],
        compiler_params=_params("parallel", "parallel"),
    )(x, scale, shift, w_in_pad)


def _mla_prep_kernel(qkr_ref, pos_ref, inv_ref, qg_ref, kvg_ref, wq_ref, wkv_ref, e_ref,
                     q_out, k_out, v_out):
    t = qkr_ref[0].astype(F32)
    qc = t[:, :MLA_Q_RANK]
    kvc = t[:, MLA_Q_RANK:MLA_Q_RANK + MLA_KV_RANK]
    kr = t[:, MLA_Q_RANK + MLA_KV_RANK:]
    qn = qc * lax.rsqrt(jnp.mean(qc * qc, -1, keepdims=True) + RMS_EPS) * qg_ref[...]
    kvn = kvc * lax.rsqrt(jnp.mean(kvc * kvc, -1, keepdims=True) + RMS_EPS) * kvg_ref[...]

    ang = pos_ref[0] * inv_ref[...]
    lane = lax.broadcasted_iota(jnp.int32, (1, HEAD_PAD), 1)
    is_nope = lane < MLA_NOPE
    is_pe = jnp.logical_and(lane >= MLA_NOPE, lane < MLA_QK)
    cosp = jnp.where(is_nope, 1.0, jnp.where(is_pe, jnp.cos(ang), 0.0))
    sinp = jnp.where(is_pe, jnp.sin(ang), 0.0)
    ones_col = jnp.where(lane == MLA_V, 1.0, 0.0)

    qa = jnp.dot(qn.astype(BF16), wq_ref[...], preferred_element_type=F32)
    kva = jnp.dot(kvn.astype(BF16), wkv_ref[...], preferred_element_type=F32)
    kp = jnp.dot(kr.astype(BF16), e_ref[...], preferred_element_type=F32)
    kpe = kp[:, :HEAD_PAD] * cosp + kp[:, HEAD_PAD:] * sinp
    full = MLA_HEADS * HEAD_PAD
    scale = MLA_QK ** -0.5
    for h in range(MLA_HEADS):
        sl = slice(h * HEAD_PAD, (h + 1) * HEAD_PAD)
        sr = slice(full + h * HEAD_PAD, full + (h + 1) * HEAD_PAD)
        q_out[0, :, sl] = ((qa[:, sl] * cosp + qa[:, sr] * sinp) * scale).astype(BF16)
        k_out[0, :, sl] = (kva[:, sl] + kpe).astype(BF16)
        v_out[0, :, sl] = (kva[:, sr] + ones_col).astype(BF16)


def _mla_weights(w_uq, w_ukv):
    half = MLA_ROPE // 2
    wq = w_uq.reshape(MLA_Q_RANK, MLA_HEADS, MLA_QK)
    zq = jnp.zeros((MLA_Q_RANK, MLA_HEADS, HEAD_PAD - MLA_QK), F32)
    plain = jnp.concatenate([wq, zq], -1)
    t1 = wq[:, :, MLA_NOPE:MLA_NOPE + half]
    t2 = wq[:, :, MLA_NOPE + half:]
    rot = jnp.concatenate([jnp.zeros((MLA_Q_RANK, MLA_HEADS, MLA_NOPE), F32), -t2, t1, zq], -1)
    wq_cat = jnp.concatenate([plain.reshape(MLA_Q_RANK, -1), rot.reshape(MLA_Q_RANK, -1)], -1).astype(BF16)

    wkv = w_ukv.reshape(MLA_KV_RANK, MLA_HEADS, MLA_NOPE + MLA_V)
    zk = jnp.zeros((MLA_KV_RANK, MLA_HEADS, HEAD_PAD - MLA_NOPE), F32)
    wk = jnp.concatenate([wkv[:, :, :MLA_NOPE], zk], -1).reshape(MLA_KV_RANK, -1)
    wv = jnp.concatenate([wkv[:, :, MLA_NOPE:], zk], -1).reshape(MLA_KV_RANK, -1)
    wkv_cat = jnp.concatenate([wk, wv], -1).astype(BF16)
    return wq_cat, wkv_cat


def _rope_constants():
    half = MLA_ROPE // 2
    inv = ROPE_THETA ** (-jnp.arange(0, MLA_ROPE, 2, dtype=F32) / MLA_ROPE)
    inv128 = jnp.concatenate([jnp.zeros((MLA_NOPE,), F32), inv, inv,
                              jnp.zeros((HEAD_PAD - MLA_QK,), F32)]).reshape(1, HEAD_PAD)
    e = np.zeros((HEAD_PAD, 2 * HEAD_PAD), np.float32)
    for f in range(MLA_ROPE):
        e[f, MLA_NOPE + f] = 1.0
    for f in range(half):
        e[half + f, HEAD_PAD + MLA_NOPE + f] = -1.0
        e[f, HEAD_PAD + MLA_NOPE + half + f] = 1.0
    return inv128, jnp.asarray(e, BF16)


def _mla_prep(qkr, pos, q_norm_g, kv_norm_g, w_uq, w_ukv, tm):
    b, s, _ = qkr.shape
    wq_cat, wkv_cat = _mla_weights(w_uq, w_ukv)
    inv128, e = _rope_constants()
    width = MLA_HEADS * HEAD_PAD
    tok = lambda w: pl.BlockSpec((1, tm, w), lambda bi, i: (bi, i, 0))
    return pl.pallas_call(
        _mla_prep_kernel,
        grid=(b, s // tm),
        in_specs=[tok(QKR_WIDTH), tok(1), _const_spec((1, HEAD_PAD)), _const_spec((1, MLA_Q_RANK)),
                  _const_spec((1, MLA_KV_RANK)), _const_spec(wq_cat.shape), _const_spec(wkv_cat.shape),
                  _const_spec(e.shape)],
        out_specs=[tok(width)] * 3,
        out_shape=[jax.ShapeDtypeStruct((b, s, width), BF16)] * 3,
        compiler_params=_params("parallel", "parallel"),
    )(qkr, pos, inv128, q_norm_g.reshape(1, -1), kv_norm_g.reshape(1, -1), wq_cat, wkv_cat, e)


def _attn_kernel(q_ref, k_ref, v_ref, o_ref, *, tq):
    i = pl.program_id(2)
    row_chunk = lax.broadcasted_iota(jnp.int32, (tq, tq), 0) // ATTN_CHUNK
    col_chunk = lax.broadcasted_iota(jnp.int32, (tq, tq), 1) // ATTN_CHUNK
    diag_mask = col_chunk <= row_chunk
    outs = []
    for hh in range(2):
        ls = slice(hh * HEAD_PAD, (hh + 1) * HEAD_PAD)
        q = q_ref[0, :, ls]

        def step(j, carry, masked, ls=ls, q=q):
            m, acc = carry
            start = pl.multiple_of(j * tq, tq)
            kj = k_ref[0, pl.ds(start, tq), ls]
            vj = v_ref[0, pl.ds(start, tq), ls]
            s = lax.dot_general(q, kj, (((1,), (1,)), ((), ())), preferred_element_type=F32)
            if masked:
                s = jnp.where(diag_mask, s, NEG_BIG)
            m_new = jnp.maximum(m, jnp.max(s, axis=-1, keepdims=True))
            p = jnp.exp(s - m_new)
            alpha = jnp.exp(m - m_new)
            acc = acc * alpha + jnp.dot(p.astype(BF16), vj, preferred_element_type=F32)
            return m_new, acc

        carry = (jnp.full((tq, 1), NEG_BIG, F32), jnp.zeros((tq, HEAD_PAD), F32))
        carry = lax.fori_loop(0, i, functools.partial(step, masked=False), carry)
        _, acc = step(i, carry, True)
        outs.append(acc / acc[:, MLA_V:MLA_V + 1])
    lane = lax.broadcasted_iota(jnp.int32, (1, HEAD_PAD), 1)
    out = jnp.where(lane < MLA_V, outs[0], pltpu.roll(outs[1], MLA_V, axis=1))
    o_ref[0] = out.astype(o_ref.dtype)


def _attention(q, k, v, tq):
    b, s, _ = q.shape
    pairs = MLA_HEADS // 2
    return pl.pallas_call(
        functools.partial(_attn_kernel, tq=tq),
        grid=(b, pairs, s // tq),
        in_specs=[pl.BlockSpec((1, tq, 2 * HEAD_PAD), lambda bi, g, i: (bi, i, g)),
                  pl.BlockSpec((1, s, 2 * HEAD_PAD), lambda bi, g, i: (bi, 0, g)),
                  pl.BlockSpec((1, s, 2 * HEAD_PAD), lambda bi, g, i: (bi, 0, g))],
        out_specs=pl.BlockSpec((1, tq, 2 * MLA_V), lambda bi, g, i: (bi, i, g)),
        out_shape=jax.ShapeDtypeStruct((b, s, MLA_WIDTH), BF16),
        compiler_params=_params("parallel", "parallel", "arbitrary"),
    )(q, k, v)


def _lane_head(width):
    return lax.broadcasted_iota(jnp.int32, (1, width), 1) // RWKV_HEAD


def _bd_stack(x):
    head = _lane_head(x.shape[1])
    return jnp.concatenate([jnp.where(head == h, x, 0.0) for h in range(PACK_HEADS)], axis=0).astype(BF16)


def _pdot(a, b):
    return jnp.dot(a.astype(BF16), _bd_stack(b), preferred_element_type=F32)


def _pdot_nt(a, b):
    return lax.dot_general(a.astype(BF16), _bd_stack(b), (((1,), (1,)), ((), ())),
                           preferred_element_type=F32)


def _split3(x):
    x1 = x.astype(BF16)
    r1 = x - x1.astype(F32)
    x2 = r1.astype(BF16)
    x3 = (r1 - x2.astype(F32)).astype(BF16)
    return x1, x2, x3


def _wkv_prep_kernel(rw_ref, prev_ref, mu_ref, wlora_ref, w0_ref, a0_ref, kk_ref, ka_ref, rk_ref,
                     ones_ref, tri_ref, rhat_ref, y0_ref, q_ref, p_ref, bonus_ref, *, tm):
    i = pl.program_id(1)
    c = WKV_CHUNK
    w = RWKV_WIDTH
    u_raw = rw_ref[0].astype(F32)
    prev_row = prev_ref[0, 7:8, :].astype(F32) * (i > 0).astype(F32)
    row = lax.broadcasted_iota(jnp.int32, (tm, 1), 0)
    u_prev = jnp.where(row == 0, prev_row, pltpu.roll(u_raw, 1, axis=0))
    u = u_raw + (u_prev - u_raw) * mu_ref[...]
    r = u[:, :w]
    k = u[:, w:2 * w]
    v = u[:, 2 * w:3 * w]
    lora_in = u[:, 3 * w:]
    lane = lax.broadcasted_iota(jnp.int32, (1, DECAY_LORA + ICLR_LORA), 1)
    lora_in = jnp.where(lane < DECAY_LORA, jnp.tanh(lora_in), lora_in)
    lora = jnp.dot(lora_in.astype(BF16), wlora_ref[...], preferred_element_type=F32)
    lw = -math.exp(-0.5) * jax.nn.sigmoid(w0_ref[...] + lora[:, :w])
    a = jax.nn.sigmoid(a0_ref[...] + lora[:, w:])

    ones_bd = ones_ref[...]
    headsum = lambda t: jnp.dot(t.astype(BF16), ones_bd, preferred_element_type=F32)
    kk = k * kk_ref[...]
    kk = kk / jnp.maximum(jnp.sqrt(headsum(kk * kk)), 1e-12)
    k = k * (1.0 + (a - 1.0) * ka_ref[...])
    bonus_ref[0] = (headsum(r * k * rk_ref[...]) * v).astype(bonus_ref.dtype)
    a_vec = -kk
    b_vec = kk * a

    tri = tri_ref[...]
    cum = sum(jnp.dot(tri, part, preferred_element_type=F32) for part in _split3(lw))

    t_idx = lax.broadcasted_iota(jnp.int32, (c, PACK_LANES), 0)
    s_idx = lax.broadcasted_iota(jnp.int32, (c, PACK_LANES), 1) % RWKV_HEAD
    strict = s_idx < t_idx
    incl = s_idx <= t_idx
    eye = (s_idx == t_idx).astype(F32)
    prow = lax.broadcasted_iota(jnp.int32, (PACK_LANES, PACK_LANES), 0)
    pcol = lax.broadcasted_iota(jnp.int32, (PACK_LANES, PACK_LANES), 1)
    same_head = (prow // RWKV_HEAD) == (pcol // RWKV_HEAD)
    on_diag = prow == pcol
    head = _lane_head(PACK_LANES)

    for cc in range(tm // c):
        rs = slice(cc * c, (cc + 1) * c)
        cum_c = cum[rs]
        lw_c = lw[rs]
        cum_last = cum_c[c - 1:c, :]
        e_pos = jnp.exp(cum_c)
        e_neg = jnp.exp(-cum_c)
        e_excl = jnp.exp(cum_c - lw_c)
        e_last = jnp.exp(cum_last - cum_c)
        w_end = jnp.exp(cum_last)
        at = a_vec[rs] * e_excl
        bt = b_vec[rs] * e_neg
        kt = k[rs] * e_neg
        rt = r[rs] * e_pos
        bl = b_vec[rs] * e_last
        kl = k[rs] * e_last
        v_c = v[rs]
        for g in range(w // PACK_LANES):
            ls = slice(g * PACK_LANES, (g + 1) * PACK_LANES)
            ar = jnp.concatenate([at[:, ls], rt[:, ls]], axis=0)
            m_b = _pdot_nt(ar, bt[:, ls])
            m_k = _pdot_nt(ar, kt[:, ls])
            l_ab = jnp.where(strict, m_b[:c], 0.0)
            l_ak = jnp.where(strict, m_k[:c], 0.0)
            m_rb = jnp.where(incl, m_b[c:], 0.0)
            m_rk = jnp.where(incl, m_k[c:], 0.0)
            t_inv = eye + l_ab
            x = l_ab
            for _ in range(int(math.log2(c)) - 1):
                x = _pdot(x, x)
                t_inv = t_inv + _pdot(t_inv, x)
            akv = _pdot(l_ak, v_c[:, ls])
            a_hat = _pdot(t_inv, at[:, ls])
            u0 = _pdot(t_inv, akv)
            rhat_ref[0, rs, ls] = (rt[:, ls] + _pdot(m_rb, a_hat)).astype(rhat_ref.dtype)
            y0_ref[0, rs, ls] = (_pdot(m_rb, u0) + _pdot(m_rk, v_c[:, ls])).astype(y0_ref.dtype)
            bl_b = bl[:, ls].astype(BF16)
            p_full = jnp.dot(a_hat.T.astype(BF16), bl_b, preferred_element_type=F32)
            p_bd = jnp.where(same_head, p_full, 0.0) + jnp.where(on_diag, w_end[:, ls], 0.0)
            p_ref[0, cc * PACK_LANES:(cc + 1) * PACK_LANES, ls] = p_bd.astype(p_ref.dtype)
            uv_t = jnp.concatenate([u0, v_c[:, ls]], axis=0).T
            bk = jnp.concatenate([bl[:, ls], kl[:, ls]], axis=0).astype(BF16)
            f = jnp.dot(uv_t.astype(BF16), bk, preferred_element_type=F32)
            q_pack = sum(jnp.where(head == h, f[h * RWKV_HEAD:(h + 1) * RWKV_HEAD], 0.0)
                         for h in range(PACK_HEADS))
            q_ref[0, rs, ls] = q_pack.astype(q_ref.dtype)


def _wkv_prep(rw, mu, w0, w_decay_up, a0, w_iclr_up, k_k, k_a, r_k, tm):
    b, s, width = rw.shape
    w = RWKV_WIDTH
    wlora = jnp.concatenate(
        [jnp.concatenate([w_decay_up, jnp.zeros((DECAY_LORA, w), F32)], 1),
         jnp.concatenate([jnp.zeros((ICLR_LORA, w), F32), w_iclr_up], 1)], 0).astype(BF16)
    hid = np.arange(w) // RWKV_HEAD
    ones_bd = jnp.asarray((hid[:, None] == hid[None, :]).astype(np.float32), BF16)
    tid = np.arange(tm)
    tri = jnp.asarray(((tid[:, None] >= tid[None, :]) &
                       (tid[:, None] // WKV_CHUNK == tid[None, :] // WKV_CHUNK)).astype(np.float32), BF16)
    row = lambda t: t.reshape(1, -1)
    tok = lambda wd: pl.BlockSpec((1, tm, wd), lambda bi, i: (bi, i, 0))
    prev = pl.BlockSpec((1, 8, width), lambda bi, i: (bi, jnp.maximum(i * (tm // 8) - 1, 0), 0))
    n_pack_rows = (s // WKV_CHUNK) * PACK_LANES
    p_spec = pl.BlockSpec((1, (tm // WKV_CHUNK) * PACK_LANES, w), lambda bi, i: (bi, i, 0))
    return pl.pallas_call(
        functools.partial(_wkv_prep_kernel, tm=tm),
        grid=(b, s // tm),
        in_specs=[tok(width), prev, _const_spec((1, width)), _const_spec(wlora.shape)]
                 + [_const_spec((1, w))] * 5 + [_const_spec(ones_bd.shape), _const_spec(tri.shape)],
        out_specs=[tok(w), tok(w), tok(w), p_spec, tok(w)],
        out_shape=[jax.ShapeDtypeStruct((b, s, w), BF16),
                   jax.ShapeDtypeStruct((b, s, w), F32),
                   jax.ShapeDtypeStruct((b, s, w), F32),
                   jax.ShapeDtypeStruct((b, n_pack_rows, w), BF16),
                   jax.ShapeDtypeStruct((b, s, w), BF16)],
        compiler_params=_params("parallel", "parallel"),
    )(rw, rw, row(mu), wlora, row(w0), row(a0), row(k_k), row(k_a), row(r_k), ones_bd, tri)


def _wkv_scan_kernel(rhat_ref, y0_ref, q_ref, p_ref, y_ref, s_ref):
    @pl.when(pl.program_id(0) == 0)
    def _():
        s_ref[...] = jnp.zeros_like(s_ref)

    nb = s_ref.shape[0]
    for b in range(nb):
        for g in range(RWKV_WIDTH // PACK_LANES):
            ls = slice(g * PACK_LANES, (g + 1) * PACK_LANES)
            s = s_ref[b, :, ls]
            y = _pdot_nt(rhat_ref[b, :, ls], s) + y0_ref[b, :, ls]
            y_ref[b, :, ls] = y.astype(y_ref.dtype)
            s_ref[b, :, ls] = (jnp.dot(s.astype(BF16), p_ref[b, :, ls], preferred_element_type=F32)
                               + q_ref[b, :, ls])


def _wkv_scan(rhat, y0, q, p):
    b, s, w = rhat.shape
    c = WKV_CHUNK
    blk = pl.BlockSpec((b, c, w), lambda ci: (0, ci, 0))
    return pl.pallas_call(
        _wkv_scan_kernel,
        grid=(s // c,),
        in_specs=[blk, blk, blk, pl.BlockSpec((b, PACK_LANES, w), lambda ci: (0, ci, 0))],
        out_specs=blk,
        out_shape=jax.ShapeDtypeStruct((b, s, w), F32),
        scratch_shapes=[pltpu.VMEM((b, RWKV_HEAD, w), F32)],
        compiler_params=_params("arbitrary"),
    )(rhat, y0, q, p)


def _epilogue_kernel(x_ref, gate_ref, attn_ref, ga_ref, y_ref, bonus_ref, gb_ref, ma_ref, mb_ref,
                     ones_ref, gng_ref, gnb_ref, wpa_ref, wpb_ref, wout_ref, pg_ref, pb_ref, o_ref):
    silu = lambda t: t * jax.nn.sigmoid(t)
    ones_bd = ones_ref[...]
    headmean = lambda t: jnp.dot(t.astype(BF16), ones_bd, preferred_element_type=F32) * (1.0 / RWKV_HEAD)

    y = y_ref[0]
    yc = y - headmean(y)
    yn = yc * lax.rsqrt(headmean(yc * yc) + GN_EPS)
    yb = yn * gng_ref[...] + gnb_ref[...] + bonus_ref[0].astype(F32)
    yb = yb * silu(gb_ref[0].astype(F32))
    ya = attn_ref[0].astype(F32) * silu(ga_ref[0].astype(F32))
    ya_p = jnp.dot(ya.astype(BF16), wpa_ref[...], preferred_element_type=F32)
    yb_p = jnp.dot(yb.astype(BF16), wpb_ref[...], preferred_element_type=F32)
    merged = (jax.nn.sigmoid(ma_ref[0].astype(F32)) * ya_p
              + jax.nn.sigmoid(mb_ref[0].astype(F32)) * yb_p)
    sub = jnp.dot(merged.astype(BF16), wout_ref[...], preferred_element_type=F32)
    z = ALPHA * x_ref[0] + (1.0 + gate_ref[0]) * sub
    zc = z - jnp.mean(z, -1, keepdims=True)
    zn = zc * lax.rsqrt(jnp.mean(zc * zc, -1, keepdims=True) + LN_EPS)
    o_ref[0] = (zn * pg_ref[...] + pb_ref[...]).astype(o_ref.dtype)


def _epilogue(x, gate, attn, ga, y, bonus, gb, ma, mb, gn_g, gn_b, w_proj_a, w_proj_b, w_out,
              post_g, post_b, tm):
    b, s, d = x.shape
    w = RWKV_WIDTH
    hid = np.arange(w) // RWKV_HEAD
    ones_bd = jnp.asarray((hid[:, None] == hid[None, :]).astype(np.float32), BF16)
    row = lambda t: t.reshape(1, -1)
    tok = lambda wd: pl.BlockSpec((1, tm, wd), lambda bi, i: (bi, i, 0))
    vec = pl.BlockSpec((1, 1, d), lambda bi, i: (bi, 0, 0))
    return pl.pallas_call(
        _epilogue_kernel,
        grid=(b, s // tm),
        in_specs=[tok(d), vec, tok(w), tok(w), tok(w), tok(w), tok(w), tok(d), tok(d),
                  _const_spec((w, w)), _const_spec((1, w)), _const_spec((1, w)),
                  _const_spec((w, d)), _const_spec((w, d)), _const_spec((d, d)),
                  _const_spec((1, d)), _const_spec((1, d))],
        out_specs=tok(d),
        out_shape=jax.ShapeDtypeStruct((b, s, d), x.dtype),
        compiler_params=_params("parallel", "parallel"),
    )(x, gate, attn, ga, y, bonus, gb, ma, mb, ones_bd, row(gn_g), row(gn_b),
      w_proj_a.astype(BF16), w_proj_b.astype(BF16), w_out.astype(BF16), row(post_g), row(post_b))


def _pad_w_in(w_in):
    used = MLA_Q_RANK + MLA_KV_RANK + MLA_ROPE
    pad = jnp.zeros((w_in.shape[0], QKR_WIDTH - used), w_in.dtype)
    return jnp.concatenate([w_in[:, :used], pad, w_in[:, used:]], axis=1).astype(BF16)


def _layer(x, c, pos, w_ada, b_ada, w_in, q_norm_g, w_uq, kv_norm_g, w_ukv, mu_rwkv, w0, w_decay_up,
           a0, w_iclr_up, k_k, k_a, r_k, gn_g, gn_b, w_proj_a, w_proj_b, w_out, post_g, post_b):
    b, s, d = x.shape
    tm = min(512, s)
    tq = min(256, s)
    ada = _adaln(c, w_ada, b_ada)
    shift, scale, gate = (ada[:, j * d:(j + 1) * d].reshape(b, 1, d) for j in range(3))
    qkr, ga, rw, gb, ma, mb = _ln_proj(x, scale, shift, _pad_w_in(w_in), tm)
    q, k, v = _mla_prep(qkr, pos, q_norm_g, kv_norm_g, w_uq, w_ukv, tm)
    attn = _attention(q, k, v, tq)
    rhat, y0, qs, p, bonus = _wkv_prep(rw, mu_rwkv, w0, w_decay_up, a0, w_iclr_up, k_k, k_a,
                                       r_k.reshape(-1), min(256, s))
    y = _wkv_scan(rhat, y0, qs, p)
    return _epilogue(x, gate, attn, ga, y, bonus, gb, ma, mb, gn_g, gn_b, w_proj_a, w_proj_b, w_out,
                     post_g, post_b, tm)


def kernel(x, c, positions, w_ada, b_ada, w_in, q_norm_g, w_uq, kv_norm_g, w_ukv, mu_rwkv, w0,
           w_decay_up, a0, w_iclr_up, k_k, k_a, r_k, gn_g, gn_b, w_proj_a, w_proj_b, w_out, post_g,
           post_b):
    pos = positions.astype(F32)[..., None]
    for l in range(w_ada.shape[0]):
        x = _layer(x, c, pos, w_ada[l], b_ada[l], w_in[l], q_norm_g[l], w_uq[l], kv_norm_g[l],
                   w_ukv[l], mu_rwkv[l], w0[l], w_decay_up[l], a0[l], w_iclr_up[l], k_k[l], k_a[l],
                   r_k[l], gn_g[l], gn_b[l], w_proj_a[l], w_proj_b[l], w_out[l], post_g[l], post_b[l])
    return x
```

```python
import functools
import math

import jax
import jax.numpy as jnp
import numpy as np
from jax import lax
from jax.experimental import pallas as pl
from jax.experimental.pallas import tpu as pltpu

F32 = jnp.float32
BF16 = jnp.bfloat16

D_MODEL = 1024
LN_EPS = 1e-5
RMS_EPS = 1e-6
GN_EPS = 64e-5

MLA_HEADS = 8
MLA_NOPE = 64
MLA_ROPE = 32
MLA_V = 64
MLA_QK = MLA_NOPE + MLA_ROPE
MLA_Q_RANK = 256
MLA_KV_RANK = 128
MLA_WIDTH = MLA_HEADS * MLA_V
ROPE_THETA = 10000.0
ATTN_CHUNK = 64

RWKV_HEADS = 8
RWKV_HEAD = 64
RWKV_WIDTH = RWKV_HEADS * RWKV_HEAD
DECAY_LORA = 64
ICLR_LORA = 64
RWKV_SHIFT_WIDTH = 3 * RWKV_WIDTH + DECAY_LORA + ICLR_LORA
WKV_CHUNK = 64

DEPTH = 1
ALPHA = (2.0 * DEPTH) ** 0.25

LANES = 128
HEAD_PAD = 128
PACK_HEADS = 2
PACK_LANES = PACK_HEADS * RWKV_HEAD

QKR_WIDTH = 512
VMEM_LIMIT = 56 * 1024 * 1024

NEG_BIG = -1e30


def _const_spec(shape):
    n = len(shape)
    return pl.BlockSpec(shape, lambda *_: (0,) * n)


def _params(*sem):
    return pltpu.CompilerParams(dimension_semantics=sem, vmem_limit_bytes=VMEM_LIMIT)


def _adaln_kernel(c_ref, w_ref, b_ref, o_ref):
    c = c_ref[...]
    sc = c * jax.nn.sigmoid(c)
    o_ref[...] = jnp.dot(sc.astype(BF16), w_ref[...], preferred_element_type=F32) + b_ref[...]


def _adaln(c, w_ada, b_ada):
    b = c.shape[0]
    return pl.pallas_call(
        _adaln_kernel,
        out_shape=jax.ShapeDtypeStruct((b, 3 * D_MODEL), F32),
        compiler_params=pltpu.CompilerParams(vmem_limit_bytes=VMEM_LIMIT),
    )(c, w_ada.astype(BF16), b_ada.reshape(1, -1))


_PROJ_WIDTHS = (QKR_WIDTH, MLA_WIDTH, RWKV_SHIFT_WIDTH, RWKV_WIDTH, D_MODEL, D_MODEL)


def _ln_proj_kernel(x_ref, scale_ref, shift_ref, w_ref, *out_refs):
    x = x_ref[0]
    xc = x - jnp.mean(x, -1, keepdims=True)
    h = xc * lax.rsqrt(jnp.mean(xc * xc, -1, keepdims=True) + LN_EPS)
    hb = (h * (1.0 + scale_ref[0]) + shift_ref[0]).astype(BF16)
    off = 0
    for o_ref, width in zip(out_refs, _PROJ_WIDTHS):
        o_ref[0] = jnp.dot(hb, w_ref[:, off:off + width], preferred_element_type=F32).astype(o_ref.dtype)
        off += width


def _ln_proj(x, scale, shift, w_in_pad, tm):
    b, s, d = x.shape
    tok = lambda w: pl.BlockSpec((1, tm, w), lambda bi, i: (bi, i, 0))
    vec = pl.BlockSpec((1, 1, d), lambda bi, i: (bi, 0, 0))
    return pl.pallas_call(
        _ln_proj_kernel,
        grid=(b, s // tm),
        in_specs=[tok(d), vec, vec, _const_spec(w_in_pad.shape)],
        out_specs=[tok(w) for w in _PROJ_WIDTHS],
        out_shape=[jax.ShapeDtypeStruct((b, s, w), BF16) for w in _PROJ_WIDTHS],
        compiler_params=_params("parallel", "parallel"),
    )(x, scale, shift, w_in_pad)


def _mla_prep_kernel(qkr_ref, pos_ref, inv_ref, qg_ref, kvg_ref, wq_ref, wkv_ref, e_ref,
                     q_out, k_out, v_out):
    t = qkr_ref[0].astype(F32)
    qc = t[:, :MLA_Q_RANK]
    kvc = t[:, MLA_Q_RANK:MLA_Q_RANK + MLA_KV_RANK]
    kr = t[:, MLA_Q_RANK + MLA_KV_RANK:]
    qn = qc * lax.rsqrt(jnp.mean(qc * qc, -1, keepdims=True) + RMS_EPS) * qg_ref[...]
    kvn = kvc * lax.rsqrt(jnp.mean(kvc * kvc, -1, keepdims=True) + RMS_EPS) * kvg_ref[...]

    ang = pos_ref[0] * inv_ref[...]
    lane = lax.broadcasted_iota(jnp.int32, (1, HEAD_PAD), 1)
    is_nope = lane < MLA_NOPE
    is_pe = jnp.logical_and(lane >= MLA_NOPE, lane < MLA_QK)
    cosp = jnp.where(is_nope, 1.0, jnp.where(is_pe, jnp.cos(ang), 0.0))
    sinp = jnp.where(is_pe, jnp.sin(ang), 0.0)
    ones_col = jnp.where(lane == MLA_V, 1.0, 0.0)

    qa = jnp.dot(qn.astype(BF16), wq_ref[...], preferred_element_type=F32)
    kva = jnp.dot(kvn.astype(BF16), wkv_ref[...], preferred_element_type=F32)
    kp = jnp.dot(kr.astype(BF16), e_ref[...], preferred_element_type=F32)
    kpe = kp[:, :HEAD_PAD] * cosp + kp[:, HEAD_PAD:] * sinp
    full = MLA_HEADS * HEAD_PAD
    scale = MLA_QK ** -0.5
    for h in range(MLA_HEADS):
        sl = slice(h * HEAD_PAD, (h + 1) * HEAD_PAD)
        sr = slice(full + h * HEAD_PAD, full + (h + 1) * HEAD_PAD)
        q_out[0, :, sl] = ((qa[:, sl] * cosp + qa[:, sr] * sinp) * scale).astype(BF16)
        k_out[0, :, sl] = (kva[:, sl] + kpe).astype(BF16)
        v_out[0, :, sl] = (kva[:, sr] + ones_col).astype(BF16)


def _mla_weights(w_uq, w_ukv):
    half = MLA_ROPE // 2
    wq = w_uq.reshape(MLA_Q_RANK, MLA_HEADS, MLA_QK)
    zq = jnp.zeros((MLA_Q_RANK, MLA_HEADS, HEAD_PAD - MLA_QK), F32)
    plain = jnp.concatenate([wq, zq], -1)
    t1 = wq[:, :, MLA_NOPE:MLA_NOPE + half]
    t2 = wq[:, :, MLA_NOPE + half:]
    rot = jnp.concatenate([jnp.zeros((MLA_Q_RANK, MLA_HEADS, MLA_NOPE), F32), -t2, t1, zq], -1)
    wq_cat = jnp.concatenate([plain.reshape(MLA_Q_RANK, -1), rot.reshape(MLA_Q_RANK, -1)], -1).astype(BF16)

    wkv = w_ukv.reshape(MLA_KV_RANK, MLA_HEADS, MLA_NOPE + MLA_V)
    zk = jnp.zeros((MLA_KV_RANK, MLA_HEADS, HEAD_PAD - MLA_NOPE), F32)
    wk = jnp.concatenate([wkv[:, :, :MLA_NOPE], zk], -1).reshape(MLA_KV_RANK, -1)
    wv = jnp.concatenate([wkv[:, :, MLA_NOPE:], zk], -1).reshape(MLA_KV_RANK, -1)
    wkv_cat = jnp.concatenate([wk, wv], -1).astype(BF16)
    return wq_cat, wkv_cat


def _rope_constants():
    half = MLA_ROPE // 2
    inv = ROPE_THETA ** (-jnp.arange(0, MLA_ROPE, 2, dtype=F32) / MLA_ROPE)
    inv128 = jnp.concatenate([jnp.zeros((MLA_NOPE,), F32), inv, inv,
                              jnp.zeros((HEAD_PAD - MLA_QK,), F32)]).reshape(1, HEAD_PAD)
    e = np.zeros((HEAD_PAD, 2 * HEAD_PAD), np.float32)
    for f in range(MLA_ROPE):
        e[f, MLA_NOPE + f] = 1.0
    for f in range(half):
        e[half + f, HEAD_PAD + MLA_NOPE + f] = -1.0
        e[f, HEAD_PAD + MLA_NOPE + half + f] = 1.0
    return inv128, jnp.asarray(e, BF16)


def _mla_prep(qkr, pos, q_norm_g, kv_norm_g, w_uq, w_ukv, tm):
    b, s, _ = qkr.shape
    wq_cat, wkv_cat = _mla_weights(w_uq, w_ukv)
    inv128, e = _rope_constants()
    width = MLA_HEADS * HEAD_PAD
    tok = lambda w: pl.BlockSpec((1, tm, w), lambda bi, i: (bi, i, 0))
    return pl.pallas_call(
        _mla_prep_kernel,
        grid=(b, s // tm),
        in_specs=[tok(QKR_WIDTH), tok(1), _const_spec((1, HEAD_PAD)), _const_spec((1, MLA_Q_RANK)),
                  _const_spec((1, MLA_KV_RANK)), _const_spec(wq_cat.shape), _const_spec(wkv_cat.shape),
                  _const_spec(e.shape)],
        out_specs=[tok(width)] * 3,
        out_shape=[jax.ShapeDtypeStruct((b, s, width), BF16)] * 3,
        compiler_params=_params("parallel", "parallel"),
    )(qkr, pos, inv128, q_norm_g.reshape(1, -1), kv_norm_g.reshape(1, -1), wq_cat, wkv_cat, e)


def _attn_kernel(q_ref, k_ref, v_ref, o_ref, *, ts, nsub):
    i = pl.program_id(2)
    row_chunk = lax.broadcasted_iota(jnp.int32, (ts, ts), 0) // ATTN_CHUNK
    col_chunk = lax.broadcasted_iota(jnp.int32, (ts, ts), 1) // ATTN_CHUNK
    diag_mask = col_chunk <= row_chunk
    lanes = [slice(hh * HEAD_PAD, (hh + 1) * HEAD_PAD) for hh in range(2)]
    chains = [(hh, sb) for hh in range(2) for sb in range(nsub)]
    qs = {(hh, sb): q_ref[0, sb * ts:(sb + 1) * ts, lanes[hh]] for hh, sb in chains}

    def step(j, carry, masked_sub, first_sub):
        start = pl.multiple_of(j * ts, ts)
        kj = [k_ref[0, pl.ds(start, ts), lanes[hh]] for hh in range(2)]
        vj = [v_ref[0, pl.ds(start, ts), lanes[hh]] for hh in range(2)]
        active = [ch for ch in chains if ch[1] >= first_sub]
        s = {ch: lax.dot_general(qs[ch], kj[ch[0]], (((1,), (1,)), ((), ())),
                                 preferred_element_type=F32) for ch in active}
        for ch in active:
            if ch[1] == masked_sub:
                s[ch] = jnp.where(diag_mask, s[ch], NEG_BIG)
        m_new = {ch: jnp.maximum(carry[ch][0], jnp.max(s[ch], axis=-1, keepdims=True)) for ch in active}
        p = {ch: jnp.exp(s[ch] - m_new[ch]).astype(BF16) for ch in active}
        alpha = {ch: jnp.exp(carry[ch][0] - m_new[ch]) for ch in active}
        out = dict(carry)
        for ch in active:
            pv = jnp.dot(p[ch], vj[ch[0]], preferred_element_type=F32)
            out[ch] = (m_new[ch], carry[ch][1] * alpha[ch] + pv)
        return out

    def loop_body(j, flat):
        carry = {ch: (flat[2 * n], flat[2 * n + 1]) for n, ch in enumerate(chains)}
        carry = step(j, carry, masked_sub=-1, first_sub=0)
        return tuple(t for ch in chains for t in carry[ch])

    init = tuple(t for _ in chains
                 for t in (jnp.full((ts, 1), NEG_BIG, F32), jnp.zeros((ts, HEAD_PAD), F32)))
    flat = lax.fori_loop(0, nsub * i, loop_body, init)
    carry = {ch: (flat[2 * n], flat[2 * n + 1]) for n, ch in enumerate(chains)}
    for t in range(nsub):
        carry = step(nsub * i + t, carry, masked_sub=t, first_sub=t)

    lane = lax.broadcasted_iota(jnp.int32, (1, HEAD_PAD), 1)
    for sb in range(nsub):
        o0, o1 = (carry[(hh, sb)][1] for hh in range(2))
        o0 = o0 / o0[:, MLA_V:MLA_V + 1]
        o1 = o1 / o1[:, MLA_V:MLA_V + 1]
        out = jnp.where(lane < MLA_V, o0, pltpu.roll(o1, MLA_V, axis=1))
        o_ref[0, sb * ts:(sb + 1) * ts, :] = out.astype(o_ref.dtype)


def _attention(q, k, v, tq, ts):
    b, s, _ = q.shape
    pairs = MLA_HEADS // 2
    return pl.pallas_call(
        functools.partial(_attn_kernel, ts=ts, nsub=tq // ts),
        grid=(b, pairs, s // tq),
        in_specs=[pl.BlockSpec((1, tq, 2 * HEAD_PAD), lambda bi, g, i: (bi, i, g)),
                  pl.BlockSpec((1, s, 2 * HEAD_PAD), lambda bi, g, i: (bi, 0, g)),
                  pl.BlockSpec((1, s, 2 * HEAD_PAD), lambda bi, g, i: (bi, 0, g))],
        out_specs=pl.BlockSpec((1, tq, 2 * MLA_V), lambda bi, g, i: (bi, i, g)),
        out_shape=jax.ShapeDtypeStruct((b, s, MLA_WIDTH), BF16),
        compiler_params=_params("parallel", "parallel", "arbitrary"),
    )(q, k, v)


def _lane_head(width):
    return lax.broadcasted_iota(jnp.int32, (1, width), 1) // RWKV_HEAD


def _bd_stack(x):
    head = _lane_head(x.shape[1])
    return jnp.concatenate([jnp.where(head == h, x, 0.0) for h in range(PACK_HEADS)], axis=0).astype(BF16)


def _pdot(a, b):
    return jnp.dot(a.astype(BF16), _bd_stack(b), preferred_element_type=F32)


def _pdot_nt(a, b):
    return lax.dot_general(a.astype(BF16), _bd_stack(b), (((1,), (1,)), ((), ())),
                           preferred_element_type=F32)


def _split3(x):
    x1 = x.astype(BF16)
    r1 = x - x1.astype(F32)
    x2 = r1.astype(BF16)
    x3 = (r1 - x2.astype(F32)).astype(BF16)
    return x1, x2, x3


def _wkv_prep_kernel(rw_ref, prev_ref, mu_ref, wlora_ref, w0_ref, a0_ref, kk_ref, ka_ref, rk_ref,
                     ones_ref, tri_ref, rhat_ref, y0_ref, q_ref, p_ref, bonus_ref, *, tm):
    i = pl.program_id(1)
    c = WKV_CHUNK
    w = RWKV_WIDTH
    u_raw = rw_ref[0].astype(F32)
    prev_row = prev_ref[0, 7:8, :].astype(F32) * (i > 0).astype(F32)
    row = lax.broadcasted_iota(jnp.int32, (tm, 1), 0)
    u_prev = jnp.where(row == 0, prev_row, pltpu.roll(u_raw, 1, axis=0))
    u = u_raw + (u_prev - u_raw) * mu_ref[...]
    r = u[:, :w]
    k = u[:, w:2 * w]
    v = u[:, 2 * w:3 * w]
    lora_in = u[:, 3 * w:]
    lane = lax.broadcasted_iota(jnp.int32, (1, DECAY_LORA + ICLR_LORA), 1)
    lora_in = jnp.where(lane < DECAY_LORA, jnp.tanh(lora_in), lora_in)
    lora = jnp.dot(lora_in.astype(BF16), wlora_ref[...], preferred_element_type=F32)
    lw = -math.exp(-0.5) * jax.nn.sigmoid(w0_ref[...] + lora[:, :w])
    a = jax.nn.sigmoid(a0_ref[...] + lora[:, w:])

    ones_bd = ones_ref[...]
    headsum = lambda t: jnp.dot(t.astype(BF16), ones_bd, preferred_element_type=F32)
    kk = k * kk_ref[...]
    kk = kk / jnp.maximum(jnp.sqrt(headsum(kk * kk)), 1e-12)
    k = k * (1.0 + (a - 1.0) * ka_ref[...])
    bonus_ref[0] = (headsum(r * k * rk_ref[...]) * v).astype(bonus_ref.dtype)
    a_vec = -kk
    b_vec = kk * a

    tri = tri_ref[...]
    cum = sum(jnp.dot(tri, part, preferred_element_type=F32) for part in _split3(lw))

    t_idx = lax.broadcasted_iota(jnp.int32, (c, PACK_LANES), 0)
    s_idx = lax.broadcasted_iota(jnp.int32, (c, PACK_LANES), 1) % RWKV_HEAD
    strict = s_idx < t_idx
    incl = s_idx <= t_idx
    eye = (s_idx == t_idx).astype(F32)
    prow = lax.broadcasted_iota(jnp.int32, (PACK_LANES, PACK_LANES), 0)
    pcol = lax.broadcasted_iota(jnp.int32, (PACK_LANES, PACK_LANES), 1)
    same_head = (prow // RWKV_HEAD) == (pcol // RWKV_HEAD)
    on_diag = prow == pcol
    head = _lane_head(PACK_LANES)

    nch = tm // c
    cum_last = jnp.concatenate(
        [jnp.broadcast_to(cum[(cc + 1) * c - 1:(cc + 1) * c, :], (c, w)) for cc in range(nch)], axis=0)
    e_pos = jnp.exp(cum)
    e_neg = jnp.exp(-cum)
    e_last = jnp.exp(cum_last - cum)
    at_all = a_vec * jnp.exp(cum - lw)
    bt_all = b_vec * e_neg
    kt_all = k * e_neg
    rt_all = r * e_pos
    bl_all = b_vec * e_last
    kl_all = k * e_last
    w_end_all = jnp.exp(cum_last)

    probs = [(cc, g) for cc in range(nch) for g in range(w // PACK_LANES)]
    sl = {pr: (slice(pr[0] * c, (pr[0] + 1) * c), slice(pr[1] * PACK_LANES, (pr[1] + 1) * PACK_LANES))
          for pr in probs}
    at = {pr: at_all[sl[pr]] for pr in probs}
    rt = {pr: rt_all[sl[pr]] for pr in probs}
    vv = {pr: v[sl[pr]] for pr in probs}
    m_b = {pr: _pdot_nt(jnp.concatenate([at[pr], rt[pr]], axis=0), bt_all[sl[pr]]) for pr in probs}
    m_k = {pr: _pdot_nt(jnp.concatenate([at[pr], rt[pr]], axis=0), kt_all[sl[pr]]) for pr in probs}
    l_ab = {pr: jnp.where(strict, m_b[pr][:c], 0.0) for pr in probs}
    t_inv = {pr: eye + l_ab[pr] for pr in probs}
    x = l_ab
    for _ in range(int(math.log2(c)) - 1):
        x = {pr: _pdot(x[pr], x[pr]) for pr in probs}
        t_inv = {pr: t_inv[pr] + _pdot(t_inv[pr], x[pr]) for pr in probs}
    akv = {pr: _pdot(jnp.where(strict, m_k[pr][:c], 0.0), vv[pr]) for pr in probs}
    a_hat = {pr: _pdot(t_inv[pr], at[pr]) for pr in probs}
    u0 = {pr: _pdot(t_inv[pr], akv[pr]) for pr in probs}
    for pr in probs:
        rs, ls = sl[pr]
        m_rb = jnp.where(incl, m_b[pr][c:], 0.0)
        m_rk = jnp.where(incl, m_k[pr][c:], 0.0)
        rhat_ref[0, rs, ls] = (rt[pr] + _pdot(m_rb, a_hat[pr])).astype(rhat_ref.dtype)
        y0_ref[0, rs, ls] = (_pdot(m_rb, u0[pr]) + _pdot(m_rk, vv[pr])).astype(y0_ref.dtype)
    for pr in probs:
        rs, ls = sl[pr]
        bl = bl_all[sl[pr]]
        p_full = jnp.dot(a_hat[pr].T.astype(BF16), bl.astype(BF16), preferred_element_type=F32)
        w_end = w_end_all[rs.start:rs.start + 1, ls]
        p_bd = jnp.where(same_head, p_full, 0.0) + jnp.where(on_diag, w_end, 0.0)
        p_ref[0, pr[0] * PACK_LANES:(pr[0] + 1) * PACK_LANES, ls] = p_bd.astype(p_ref.dtype)
        uv_t = jnp.concatenate([u0[pr], vv[pr]], axis=0).T
        bk = jnp.concatenate([bl, kl_all[sl[pr]]], axis=0).astype(BF16)
        f = jnp.dot(uv_t.astype(BF16), bk, preferred_element_type=F32)
        q_pack = sum(jnp.where(head == h, f[h * RWKV_HEAD:(h + 1) * RWKV_HEAD], 0.0)
                     for h in range(PACK_HEADS))
        q_ref[0, rs, ls] = q_pack.astype(q_ref.dtype)


def _wkv_prep(rw, mu, w0, w_decay_up, a0, w_iclr_up, k_k, k_a, r_k, tm):
    b, s, width = rw.shape
    w = RWKV_WIDTH
    wlora = jnp.concatenate(
        [jnp.concatenate([w_decay_up, jnp.zeros((DECAY_LORA, w), F32)], 1),
         jnp.concatenate([jnp.zeros((ICLR_LORA, w), F32), w_iclr_up], 1)], 0).astype(BF16)
    hid = np.arange(w) // RWKV_HEAD
    ones_bd = jnp.asarray((hid[:, None] == hid[None, :]).astype(np.float32), BF16)
    tid = np.arange(tm)
    tri = jnp.asarray(((tid[:, None] >= tid[None, :]) &
                       (tid[:, None] // WKV_CHUNK == tid[None, :] // WKV_CHUNK)).astype(np.float32), BF16)
    row = lambda t: t.reshape(1, -1)
    tok = lambda wd: pl.BlockSpec((1, tm, wd), lambda bi, i: (bi, i, 0))
    prev = pl.BlockSpec((1, 8, width), lambda bi, i: (bi, jnp.maximum(i * (tm // 8) - 1, 0), 0))
    n_pack_rows = (s // WKV_CHUNK) * PACK_LANES
    p_spec = pl.BlockSpec((1, (tm // WKV_CHUNK) * PACK_LANES, w), lambda bi, i: (bi, i, 0))
    return pl.pallas_call(
        functools.partial(_wkv_prep_kernel, tm=tm),
        grid=(b, s // tm),
        in_specs=[tok(width), prev, _const_spec((1, width)), _const_spec(wlora.shape)]
                 + [_const_spec((1, w))] * 5 + [_const_spec(ones_bd.shape), _const_spec(tri.shape)],
        out_specs=[tok(w), tok(w), tok(w), p_spec, tok(w)],
        out_shape=[jax.ShapeDtypeStruct((b, s, w), BF16),
                   jax.ShapeDtypeStruct((b, s, w), F32),
                   jax.ShapeDtypeStruct((b, s, w), F32),
                   jax.ShapeDtypeStruct((b, n_pack_rows, w), BF16),
                   jax.ShapeDtypeStruct((b, s, w), BF16)],
        compiler_params=_params("parallel", "parallel"),
    )(rw, rw, row(mu), wlora, row(w0), row(a0), row(k_k), row(k_a), row(r_k), ones_bd, tri)


def _wkv_scan_kernel(rhat_ref, y0_ref, q_ref, p_ref, y_ref, s_ref):
    @pl.when(pl.program_id(0) == 0)
    def _():
        s_ref[...] = jnp.zeros_like(s_ref)

    nb = s_ref.shape[0]
    for b in range(nb):
        for g in range(RWKV_WIDTH // PACK_LANES):
            ls = slice(g * PACK_LANES, (g + 1) * PACK_LANES)
            s = s_ref[b, :, ls]
            y = _pdot_nt(rhat_ref[b, :, ls], s) + y0_ref[b, :, ls]
            y_ref[b, :, ls] = y.astype(y_ref.dtype)
            s_ref[b, :, ls] = (jnp.dot(s.astype(BF16), p_ref[b, :, ls], preferred_element_type=F32)
                               + q_ref[b, :, ls])


def _wkv_scan(rhat, y0, q, p):
    b, s, w = rhat.shape
    c = WKV_CHUNK
    blk = pl.BlockSpec((b, c, w), lambda ci: (0, ci, 0))
    return pl.pallas_call(
        _wkv_scan_kernel,
        grid=(s // c,),
        in_specs=[blk, blk, blk, pl.BlockSpec((b, PACK_LANES, w), lambda ci: (0, ci, 0))],
        out_specs=blk,
        out_shape=jax.ShapeDtypeStruct((b, s, w), F32),
        scratch_shapes=[pltpu.VMEM((b, RWKV_HEAD, w), F32)],
        compiler_params=_params("arbitrary"),
    )(rhat, y0, q, p)


def _epilogue_kernel(x_ref, gate_ref, attn_ref, ga_ref, y_ref, bonus_ref, gb_ref, ma_ref, mb_ref,
                     ones_ref, gng_ref, gnb_ref, wpa_ref, wpb_ref, wout_ref, pg_ref, pb_ref, o_ref):
    silu = lambda t: t * jax.nn.sigmoid(t)
    ones_bd = ones_ref[...]
    headmean = lambda t: jnp.dot(t.astype(BF16), ones_bd, preferred_element_type=F32) * (1.0 / RWKV_HEAD)

    y = y_ref[0]
    yc = y - headmean(y)
    yn = yc * lax.rsqrt(headmean(yc * yc) + GN_EPS)
    yb = yn * gng_ref[...] + gnb_ref[...] + bonus_ref[0].astype(F32)
    yb = yb * silu(gb_ref[0].astype(F32))
    ya = attn_ref[0].astype(F32) * silu(ga_ref[0].astype(F32))
    ya_p = jnp.dot(ya.astype(BF16), wpa_ref[...], preferred_element_type=F32)
    yb_p = jnp.dot(yb.astype(BF16), wpb_ref[...], preferred_element_type=F32)
    merged = (jax.nn.sigmoid(ma_ref[0].astype(F32)) * ya_p
              + jax.nn.sigmoid(mb_ref[0].astype(F32)) * yb_p)
    sub = jnp.dot(merged.astype(BF16), wout_ref[...], preferred_element_type=F32)
    z = ALPHA * x_ref[0] + (1.0 + gate_ref[0]) * sub
    zc = z - jnp.mean(z, -1, keepdims=True)
    zn = zc * lax.rsqrt(jnp.mean(zc * zc, -1, keepdims=True) + LN_EPS)
    o_ref[0] = (zn * pg_ref[...] + pb_ref[...]).astype(o_ref.dtype)


def _epilogue(x, gate, attn, ga, y, bonus, gb, ma, mb, gn_g, gn_b, w_proj_a, w_proj_b, w_out,
              post_g, post_b, tm):
    b, s, d = x.shape
    w = RWKV_WIDTH
    hid = np.arange(w) // RWKV_HEAD
    ones_bd = jnp.asarray((hid[:, None] == hid[None, :]).astype(np.float32), BF16)
    row = lambda t: t.reshape(1, -1)
    tok = lambda wd: pl.BlockSpec((1, tm, wd), lambda bi, i: (bi, i, 0))
    vec = pl.BlockSpec((1, 1, d), lambda bi, i: (bi, 0, 0))
    return pl.pallas_call(
        _epilogue_kernel,
        grid=(b, s // tm),
        in_specs=[tok(d), vec, tok(w), tok(w), tok(w), tok(w), tok(w), tok(d), tok(d),
                  _const_spec((w, w)), _const_spec((1, w)), _const_spec((1, w)),
                  _const_spec((w, d)), _const_spec((w, d)), _const_spec((d, d)),
                  _const_spec((1, d)), _const_spec((1, d))],
        out_specs=tok(d),
        out_shape=jax.ShapeDtypeStruct((b, s, d), x.dtype),
        compiler_params=_params("parallel", "parallel"),
    )(x, gate, attn, ga, y, bonus, gb, ma, mb, ones_bd, row(gn_g), row(gn_b),
      w_proj_a.astype(BF16), w_proj_b.astype(BF16), w_out.astype(BF16), row(post_g), row(post_b))


def _pad_w_in(w_in):
    used = MLA_Q_RANK + MLA_KV_RANK + MLA_ROPE
    pad = jnp.zeros((w_in.shape[0], QKR_WIDTH - used), w_in.dtype)
    return jnp.concatenate([w_in[:, :used], pad, w_in[:, used:]], axis=1).astype(BF16)


def _layer(x, c, pos, w_ada, b_ada, w_in, q_norm_g, w_uq, kv_norm_g, w_ukv, mu_rwkv, w0, w_decay_up,
           a0, w_iclr_up, k_k, k_a, r_k, gn_g, gn_b, w_proj_a, w_proj_b, w_out, post_g, post_b):
    b, s, d = x.shape
    tm = min(512, s)
    ts = min(256, s)
    tq = min(512, s)
    ada = _adaln(c, w_ada, b_ada)
    shift, scale, gate = (ada[:, j * d:(j + 1) * d].reshape(b, 1, d) for j in range(3))
    qkr, ga, rw, gb, ma, mb = _ln_proj(x, scale, shift, _pad_w_in(w_in), tm)
    q, k, v = _mla_prep(qkr, pos, q_norm_g, kv_norm_g, w_uq, w_ukv, tm)
    attn = _attention(q, k, v, tq, ts)
    rhat, y0, qs, p, bonus = _wkv_prep(rw, mu_rwkv, w0, w_decay_up, a0, w_iclr_up, k_k, k_a,
                                       r_k.reshape(-1), min(256, s))
    y = _wkv_scan(rhat, y0, qs, p)
    return _epilogue(x, gate, attn, ga, y, bonus, gb, ma, mb, gn_g, gn_b, w_proj_a, w_proj_b, w_out,
                     post_g, post_b, tm)


def kernel(x, c, positions, w_ada, b_ada, w_in, q_norm_g, w_uq, kv_norm_g, w_ukv, mu_rwkv, w0,
           w_decay_up, a0, w_iclr_up, k_k, k_a, r_k, gn_g, gn_b, w_proj_a, w_proj_b, w_out, post_g,
           post_b):
    pos = positions.astype(F32)[..., None]
    for l in range(w_ada.shape[0]):
        x = _layer(x, c, pos, w_ada[l], b_ada[l], w_in[l], q_norm_g[l], w_uq[l], kv_norm_g[l],
                   w_ukv[l], mu_rwkv[l], w0[l], w_decay_up[l], a0[l], w_iclr_up[l], k_k[l], k_a[l],
                   r_k[l], gn_g[l], gn_b[l], w_proj_a[l], w_proj_b[l], w_out[l], post_g[l], post_b[l])
    return x
```

```python
import functools
import math

import jax
import jax.numpy as jnp
import numpy as np
from jax import lax
from jax.experimental import pallas as pl
from jax.experimental.pallas import tpu as pltpu

F32 = jnp.float32
BF16 = jnp.bfloat16

D_MODEL = 1024
LN_EPS = 1e-5
RMS_EPS = 1e-6
GN_EPS = 64e-5

MLA_HEADS = 8
MLA_NOPE = 64
MLA_ROPE = 32
MLA_V = 64
MLA_QK = MLA_NOPE + MLA_ROPE
MLA_Q_RANK = 256
MLA_KV_RANK = 128
MLA_WIDTH = MLA_HEADS * MLA_V
ROPE_THETA = 10000.0
ATTN_CHUNK = 64

RWKV_HEADS = 8
RWKV_HEAD = 64
RWKV_WIDTH = RWKV_HEADS * RWKV_HEAD
DECAY_LORA = 64
ICLR_LORA = 64
RWKV_SHIFT_WIDTH = 3 * RWKV_WIDTH + DECAY_LORA + ICLR_LORA
WKV_CHUNK = 64

DEPTH = 1
ALPHA = (2.0 * DEPTH) ** 0.25

LANES = 128
HEAD_PAD = 128
V_ROWS = 80
KEY_TILE = 256
PACK_HEADS = 2
PACK_LANES = PACK_HEADS * RWKV_HEAD

QKR_WIDTH = 512
VMEM_LIMIT = 56 * 1024 * 1024

NEG_BIG = -1e30


def _const_spec(shape):
    n = len(shape)
    return pl.BlockSpec(shape, lambda *_: (0,) * n)


def _params(*sem):
    return pltpu.CompilerParams(dimension_semantics=sem, vmem_limit_bytes=VMEM_LIMIT)


def _adaln_kernel(c_ref, w_ref, b_ref, o_ref):
    c = c_ref[...]
    sc = c * jax.nn.sigmoid(c)
    o_ref[...] = jnp.dot(sc.astype(BF16), w_ref[...], preferred_element_type=F32) + b_ref[...]


def _adaln(c, w_ada, b_ada):
    b = c.shape[0]
    return pl.pallas_call(
        _adaln_kernel,
        out_shape=jax.ShapeDtypeStruct((b, 3 * D_MODEL), F32),
        compiler_params=pltpu.CompilerParams(vmem_limit_bytes=VMEM_LIMIT),
    )(c, w_ada.astype(BF16), b_ada.reshape(1, -1))


_PROJ_WIDTHS = (QKR_WIDTH, MLA_WIDTH, RWKV_SHIFT_WIDTH, RWKV_WIDTH, D_MODEL, D_MODEL)


def _ln_proj_kernel(x_ref, scale_ref, shift_ref, w_ref, *out_refs):
    x = x_ref[0]
    xc = x - jnp.mean(x, -1, keepdims=True)
    h = xc * lax.rsqrt(jnp.mean(xc * xc, -1, keepdims=True) + LN_EPS)
    hb = (h * (1.0 + scale_ref[0]) + shift_ref[0]).astype(BF16)
    off = 0
    for o_ref, width in zip(out_refs, _PROJ_WIDTHS):
        o_ref[0] = jnp.dot(hb, w_ref[:, off:off + width], preferred_element_type=F32).astype(o_ref.dtype)
        off += width


def _ln_proj(x, scale, shift, w_in_pad, tm):
    b, s, d = x.shape
    tok = lambda w: pl.BlockSpec((1, tm, w), lambda bi, i: (bi, i, 0))
    vec = pl.BlockSpec((1, 1, d), lambda bi, i: (bi, 0, 0))
    return pl.pallas_call(
        _ln_proj_kernel,
        grid=(b, s // tm),
        in_specs=[tok(d), vec, vec, _const_spec(w_in_pad.shape)],
        out_specs=[tok(w) for w in _PROJ_WIDTHS],
        out_shape=[jax.ShapeDtypeStruct((b, s, w), BF16) for w in _PROJ_WIDTHS],
        compiler_params=_params("parallel", "parallel"),
    )(x, scale, shift, w_in_pad)


def _mla_prep_kernel(qkr_ref, pos_ref, inv_ref, qg_ref, kvg_ref, wq_ref, wk_ref, wvt_ref, e_ref,
                     qt_out, k_out, vt_out):
    t = qkr_ref[0].astype(F32)
    qc = t[:, :MLA_Q_RANK]
    kvc = t[:, MLA_Q_RANK:MLA_Q_RANK + MLA_KV_RANK]
    kr = t[:, MLA_Q_RANK + MLA_KV_RANK:]
    qn = qc * lax.rsqrt(jnp.mean(qc * qc, -1, keepdims=True) + RMS_EPS) * qg_ref[...]
    kvn = kvc * lax.rsqrt(jnp.mean(kvc * kvc, -1, keepdims=True) + RMS_EPS) * kvg_ref[...]

    ang = pos_ref[0] * inv_ref[...]
    lane = lax.broadcasted_iota(jnp.int32, (1, HEAD_PAD), 1)
    is_nope = lane < MLA_NOPE
    is_pe = jnp.logical_and(lane >= MLA_NOPE, lane < MLA_QK)
    cosp = jnp.where(is_nope, 1.0, jnp.where(is_pe, jnp.cos(ang), 0.0))
    sinp = jnp.where(is_pe, jnp.sin(ang), 0.0)

    qa = jnp.dot(qn.astype(BF16), wq_ref[...], preferred_element_type=F32)
    ka = jnp.dot(kvn.astype(BF16), wk_ref[...], preferred_element_type=F32)
    kp = jnp.dot(kr.astype(BF16), e_ref[...], preferred_element_type=F32)
    kpe = kp[:, :HEAD_PAD] * cosp + kp[:, HEAD_PAD:] * sinp
    full = MLA_HEADS * HEAD_PAD
    scale = MLA_QK ** -0.5 * math.log2(math.e)
    for h in range(MLA_HEADS):
        sl = slice(h * HEAD_PAD, (h + 1) * HEAD_PAD)
        sr = slice(full + h * HEAD_PAD, full + (h + 1) * HEAD_PAD)
        q_t = ((qa[:, sl] * cosp + qa[:, sr] * sinp) * scale).T.astype(BF16)
        for kt in range(q_t.shape[1] // KEY_TILE):
            qt_out[0, kt, sl, :] = q_t[:, kt * KEY_TILE:(kt + 1) * KEY_TILE]
        k_out[0, :, sl] = (ka[:, sl] + kpe).astype(BF16)

    vt = jnp.dot(wvt_ref[...], kvn.T.astype(BF16), preferred_element_type=F32)
    vrow = lax.broadcasted_iota(jnp.int32, (MLA_HEADS * V_ROWS, 1), 0) % V_ROWS
    vt = (vt + jnp.where(vrow == MLA_V, 1.0, 0.0)).astype(BF16)
    for kt in range(vt.shape[1] // KEY_TILE):
        vt_out[0, kt] = vt[:, kt * KEY_TILE:(kt + 1) * KEY_TILE]


def _mla_weights(w_uq, w_ukv):
    half = MLA_ROPE // 2
    wq = w_uq.reshape(MLA_Q_RANK, MLA_HEADS, MLA_QK)
    zq = jnp.zeros((MLA_Q_RANK, MLA_HEADS, HEAD_PAD - MLA_QK), F32)
    plain = jnp.concatenate([wq, zq], -1)
    t1 = wq[:, :, MLA_NOPE:MLA_NOPE + half]
    t2 = wq[:, :, MLA_NOPE + half:]
    rot = jnp.concatenate([jnp.zeros((MLA_Q_RANK, MLA_HEADS, MLA_NOPE), F32), -t2, t1, zq], -1)
    wq_cat = jnp.concatenate([plain.reshape(MLA_Q_RANK, -1), rot.reshape(MLA_Q_RANK, -1)], -1).astype(BF16)

    wkv = w_ukv.reshape(MLA_KV_RANK, MLA_HEADS, MLA_NOPE + MLA_V)
    zk = jnp.zeros((MLA_KV_RANK, MLA_HEADS, HEAD_PAD - MLA_NOPE), F32)
    wk = jnp.concatenate([wkv[:, :, :MLA_NOPE], zk], -1).reshape(MLA_KV_RANK, -1).astype(BF16)
    zv = jnp.zeros((MLA_KV_RANK, MLA_HEADS, V_ROWS - MLA_V), F32)
    wvt = jnp.concatenate([wkv[:, :, MLA_NOPE:], zv], -1).reshape(MLA_KV_RANK, -1).T.astype(BF16)
    return wq_cat, wk, wvt


def _rope_constants():
    half = MLA_ROPE // 2
    inv = ROPE_THETA ** (-jnp.arange(0, MLA_ROPE, 2, dtype=F32) / MLA_ROPE)
    inv128 = jnp.concatenate([jnp.zeros((MLA_NOPE,), F32), inv, inv,
                              jnp.zeros((HEAD_PAD - MLA_QK,), F32)]).reshape(1, HEAD_PAD)
    e = np.zeros((HEAD_PAD, 2 * HEAD_PAD), np.float32)
    for f in range(MLA_ROPE):
        e[f, MLA_NOPE + f] = 1.0
    for f in range(half):
        e[half + f, HEAD_PAD + MLA_NOPE + f] = -1.0
        e[f, HEAD_PAD + MLA_NOPE + half + f] = 1.0
    return inv128, jnp.asarray(e, BF16)


def _mla_prep(qkr, pos, q_norm_g, kv_norm_g, w_uq, w_ukv, tm):
    b, s, _ = qkr.shape
    wq_cat, wk, wvt = _mla_weights(w_uq, w_ukv)
    inv128, e = _rope_constants()
    width = MLA_HEADS * HEAD_PAD
    vrows = MLA_HEADS * V_ROWS
    tok = lambda w: pl.BlockSpec((1, tm, w), lambda bi, i: (bi, i, 0))
    return pl.pallas_call(
        _mla_prep_kernel,
        grid=(b, s // tm),
        in_specs=[tok(QKR_WIDTH), tok(1), _const_spec((1, HEAD_PAD)), _const_spec((1, MLA_Q_RANK)),
                  _const_spec((1, MLA_KV_RANK)), _const_spec(wq_cat.shape), _const_spec(wk.shape),
                  _const_spec(wvt.shape), _const_spec(e.shape)],
        out_specs=[pl.BlockSpec((1, tm // KEY_TILE, width, KEY_TILE), lambda bi, i: (bi, i, 0, 0)),
                   tok(width),
                   pl.BlockSpec((1, tm // KEY_TILE, vrows, KEY_TILE), lambda bi, i: (bi, i, 0, 0))],
        out_shape=[jax.ShapeDtypeStruct((b, s // KEY_TILE, width, KEY_TILE), BF16),
                   jax.ShapeDtypeStruct((b, s, width), BF16),
                   jax.ShapeDtypeStruct((b, s // KEY_TILE, vrows, KEY_TILE), BF16)],
        compiler_params=_params("parallel", "parallel"),
    )(qkr, pos, inv128, q_norm_g.reshape(1, -1), kv_norm_g.reshape(1, -1), wq_cat, wk, wvt, e)


ATTN_SUBTILES = 2


def _attn_kernel(qt_ref, k_ref, vt_ref, o_ref, s_buf):
    ts = KEY_TILE
    nsub = ATTN_SUBTILES
    i = pl.program_id(2)
    key_chunk = lax.broadcasted_iota(jnp.int32, (ts, ts), 0) // ATTN_CHUNK
    qry_chunk = lax.broadcasted_iota(jnp.int32, (ts, ts), 1) // ATTN_CHUNK
    diag_mask = key_chunk <= qry_chunk
    lanes = [slice(hh * HEAD_PAD, (hh + 1) * HEAD_PAD) for hh in range(2)]
    chains = [(hh, sb) for hh in range(2) for sb in range(nsub)]
    qts = {(hh, sb): qt_ref[0, sb, lanes[hh], :] for hh, sb in chains}

    def scores(j, slot, first_sub):
        start = pl.multiple_of(j * ts, ts)
        kj = [k_ref[0, pl.ds(start, ts), lanes[hh]] for hh in range(2)]
        for n, ch in enumerate(chains):
            if ch[1] >= first_sub:
                s_buf[slot, n] = jnp.dot(kj[ch[0]], qts[ch], preferred_element_type=F32)

    def consume(j, slot, carry, masked_sub, first_sub):
        vj = [vt_ref[0, j, hh * V_ROWS:(hh + 1) * V_ROWS, :] for hh in range(2)]
        active = [(n, ch) for n, ch in enumerate(chains) if ch[1] >= first_sub]
        s = {ch: s_buf[slot, n] for n, ch in active}
        for _, ch in active:
            if ch[1] == masked_sub:
                s[ch] = jnp.where(diag_mask, s[ch], NEG_BIG)
        m_new = {ch: jnp.maximum(carry[ch][0], jnp.max(s[ch], axis=0, keepdims=True)) for _, ch in active}
        p = {ch: jnp.exp2(s[ch] - m_new[ch]).astype(BF16) for _, ch in active}
        alpha = {ch: jnp.exp2(carry[ch][0] - m_new[ch]) for _, ch in active}
        out = dict(carry)
        for _, ch in active:
            pv = jnp.dot(vj[ch[0]], p[ch], preferred_element_type=F32)
            out[ch] = (m_new[ch], carry[ch][1] * alpha[ch] + pv)
        return out

    def loop_body(t, flat):
        carry = {ch: (flat[2 * n], flat[2 * n + 1]) for n, ch in enumerate(chains)}
        scores(2 * t + 1, 1, 0)
        carry = consume(2 * t, 0, carry, -1, 0)
        scores(2 * t + 2, 0, 0)
        carry = consume(2 * t + 1, 1, carry, -1, 0)
        return tuple(v for ch in chains for v in carry[ch])

    scores(0, 0, 0)
    init = tuple(v for _ in chains
                 for v in (jnp.full((1, ts), NEG_BIG, F32), jnp.zeros((V_ROWS, ts), F32)))
    flat = lax.fori_loop(0, i, loop_body, init)
    carry = {ch: (flat[2 * n], flat[2 * n + 1]) for n, ch in enumerate(chains)}
    scores(2 * i + 1, 1, 1)
    carry = consume(2 * i, 0, carry, 0, 0)
    carry = consume(2 * i + 1, 1, carry, 1, 1)

    for sb in range(nsub):
        normed = []
        for hh in range(2):
            acc = carry[(hh, sb)][1]
            normed.append(acc[:MLA_V] / acc[MLA_V:MLA_V + 1])
        out_t = jnp.concatenate(normed, axis=0)
        o_ref[0, sb * ts:(sb + 1) * ts, :] = out_t.T.astype(o_ref.dtype)


def _attention(qt, k, vt):
    b, s, _ = k.shape
    pairs = MLA_HEADS // 2
    nsub = ATTN_SUBTILES
    tq = nsub * KEY_TILE
    return pl.pallas_call(
        _attn_kernel,
        grid=(b, pairs, s // tq),
        in_specs=[pl.BlockSpec((1, nsub, 2 * HEAD_PAD, KEY_TILE), lambda bi, g, i: (bi, i, g, 0)),
                  pl.BlockSpec((1, s, 2 * HEAD_PAD), lambda bi, g, i: (bi, 0, g)),
                  pl.BlockSpec((1, s // KEY_TILE, 2 * V_ROWS, KEY_TILE), lambda bi, g, i: (bi, 0, g, 0))],
        out_specs=pl.BlockSpec((1, tq, 2 * MLA_V), lambda bi, g, i: (bi, i, g)),
        out_shape=jax.ShapeDtypeStruct((b, s, MLA_WIDTH), BF16),
        scratch_shapes=[pltpu.VMEM((2, 2 * nsub, KEY_TILE, KEY_TILE), F32)],
        compiler_params=_params("parallel", "parallel", "arbitrary"),
    )(qt, k, vt)


def _lane_head(width):
    return lax.broadcasted_iota(jnp.int32, (1, width), 1) // RWKV_HEAD


def _bd_stack(x):
    head = _lane_head(x.shape[1])
    return jnp.concatenate([jnp.where(head == h, x, 0.0) for h in range(PACK_HEADS)], axis=0).astype(BF16)


def _pdot(a, b):
    return jnp.dot(a.astype(BF16), _bd_stack(b), preferred_element_type=F32)


def _pdot_nt(a, b):
    return lax.dot_general(a.astype(BF16), _bd_stack(b), (((1,), (1,)), ((), ())),
                           preferred_element_type=F32)


def _split3(x):
    x1 = x.astype(BF16)
    r1 = x - x1.astype(F32)
    x2 = r1.astype(BF16)
    x3 = (r1 - x2.astype(F32)).astype(BF16)
    return x1, x2, x3


def _wkv_prep_kernel(rw_ref, prev_ref, mu_ref, wlora_ref, w0_ref, a0_ref, kk_ref, ka_ref, rk_ref,
                     ones_ref, tri_ref, rhat_ref, y0_ref, q_ref, p_ref, bonus_ref, *, tm):
    i = pl.program_id(1)
    c = WKV_CHUNK
    w = RWKV_WIDTH
    u_raw = rw_ref[0].astype(F32)
    prev_row = prev_ref[0, 7:8, :].astype(F32) * (i > 0).astype(F32)
    row = lax.broadcasted_iota(jnp.int32, (tm, 1), 0)
    u_prev = jnp.where(row == 0, prev_row, pltpu.roll(u_raw, 1, axis=0))
    u = u_raw + (u_prev - u_raw) * mu_ref[...]
    r = u[:, :w]
    k = u[:, w:2 * w]
    v = u[:, 2 * w:3 * w]
    lora_in = u[:, 3 * w:]
    lane = lax.broadcasted_iota(jnp.int32, (1, DECAY_LORA + ICLR_LORA), 1)
    lora_in = jnp.where(lane < DECAY_LORA, jnp.tanh(lora_in), lora_in)
    lora = jnp.dot(lora_in.astype(BF16), wlora_ref[...], preferred_element_type=F32)
    lw = -math.exp(-0.5) * jax.nn.sigmoid(w0_ref[...] + lora[:, :w])
    a = jax.nn.sigmoid(a0_ref[...] + lora[:, w:])

    ones_bd = ones_ref[...]
    headsum = lambda t: jnp.dot(t.astype(BF16), ones_bd, preferred_element_type=F32)
    kk = k * kk_ref[...]
    kk = kk / jnp.maximum(jnp.sqrt(headsum(kk * kk)), 1e-12)
    k = k * (1.0 + (a - 1.0) * ka_ref[...])
    bonus_ref[0] = (headsum(r * k * rk_ref[...]) * v).astype(bonus_ref.dtype)
    a_vec = -kk
    b_vec = kk * a

    tri = tri_ref[...]
    cum = sum(jnp.dot(tri, part, preferred_element_type=F32) for part in _split3(lw))

    t_idx = lax.broadcasted_iota(jnp.int32, (c, PACK_LANES), 0)
    s_idx = lax.broadcasted_iota(jnp.int32, (c, PACK_LANES), 1) % RWKV_HEAD
    strict = s_idx < t_idx
    incl = s_idx <= t_idx
    eye = (s_idx == t_idx).astype(F32)
    prow = lax.broadcasted_iota(jnp.int32, (PACK_LANES, PACK_LANES), 0)
    pcol = lax.broadcasted_iota(jnp.int32, (PACK_LANES, PACK_LANES), 1)
    same_head = (prow // RWKV_HEAD) == (pcol // RWKV_HEAD)
    on_diag = prow == pcol
    head = _lane_head(PACK_LANES)

    nch = tm // c
    cum_last = jnp.concatenate(
        [jnp.broadcast_to(cum[(cc + 1) * c - 1:(cc + 1) * c, :], (c, w)) for cc in range(nch)], axis=0)
    e_pos = jnp.exp(cum)
    e_neg = jnp.exp(-cum)
    e_last = jnp.exp(cum_last - cum)
    at_all = a_vec * jnp.exp(cum - lw)
    bt_all = b_vec * e_neg
    kt_all = k * e_neg
    rt_all = r * e_pos
    bl_all = b_vec * e_last
    kl_all = k * e_last
    w_end_all = jnp.exp(cum_last)

    probs = [(cc, g) for cc in range(nch) for g in range(w // PACK_LANES)]
    sl = {pr: (slice(pr[0] * c, (pr[0] + 1) * c), slice(pr[1] * PACK_LANES, (pr[1] + 1) * PACK_LANES))
          for pr in probs}
    at = {pr: at_all[sl[pr]] for pr in probs}
    rt = {pr: rt_all[sl[pr]] for pr in probs}
    vv = {pr: v[sl[pr]] for pr in probs}
    m_b = {pr: _pdot_nt(jnp.concatenate([at[pr], rt[pr]], axis=0), bt_all[sl[pr]]) for pr in probs}
    m_k = {pr: _pdot_nt(jnp.concatenate([at[pr], rt[pr]], axis=0), kt_all[sl[pr]]) for pr in probs}
    l_ab = {pr: jnp.where(strict, m_b[pr][:c], 0.0) for pr in probs}
    t_inv = {pr: eye + l_ab[pr] for pr in probs}
    x = l_ab
    for _ in range(int(math.log2(c)) - 1):
        x = {pr: _pdot(x[pr], x[pr]) for pr in probs}
        t_inv = {pr: t_inv[pr] + _pdot(t_inv[pr], x[pr]) for pr in probs}
    akv = {pr: _pdot(jnp.where(strict, m_k[pr][:c], 0.0), vv[pr]) for pr in probs}
    a_hat = {pr: _pdot(t_inv[pr], at[pr]) for pr in probs}
    u0 = {pr: _pdot(t_inv[pr], akv[pr]) for pr in probs}
    for pr in probs:
        rs, ls = sl[pr]
        m_rb = jnp.where(incl, m_b[pr][c:], 0.0)
        m_rk = jnp.where(incl, m_k[pr][c:], 0.0)
        rhat_ref[0, rs, ls] = (rt[pr] + _pdot(m_rb, a_hat[pr])).astype(rhat_ref.dtype)
        y0_ref[0, rs, ls] = (_pdot(m_rb, u0[pr]) + _pdot(m_rk, vv[pr])).astype(y0_ref.dtype)
    for pr in probs:
        rs, ls = sl[pr]
        bl = bl_all[sl[pr]]
        p_full = jnp.dot(a_hat[pr].T.astype(BF16), bl.astype(BF16), preferred_element_type=F32)
        w_end = w_end_all[rs.start:rs.start + 1, ls]
        p_bd = jnp.where(same_head, p_full, 0.0) + jnp.where(on_diag, w_end, 0.0)
        p_ref[0, pr[0] * PACK_LANES:(pr[0] + 1) * PACK_LANES, ls] = p_bd.astype(p_ref.dtype)
        uv_t = jnp.concatenate([u0[pr], vv[pr]], axis=0).T
        bk = jnp.concatenate([bl, kl_all[sl[pr]]], axis=0).astype(BF16)
        f = jnp.dot(uv_t.astype(BF16), bk, preferred_element_type=F32)
        q_pack = sum(jnp.where(head == h, f[h * RWKV_HEAD:(h + 1) * RWKV_HEAD], 0.0)
                     for h in range(PACK_HEADS))
        q_ref[0, rs, ls] = q_pack.astype(q_ref.dtype)


def _wkv_prep(rw, mu, w0, w_decay_up, a0, w_iclr_up, k_k, k_a, r_k, tm):
    b, s, width = rw.shape
    w = RWKV_WIDTH
    wlora = jnp.concatenate(
        [jnp.concatenate([w_decay_up, jnp.zeros((DECAY_LORA, w), F32)], 1),
         jnp.concatenate([jnp.zeros((ICLR_LORA, w), F32), w_iclr_up], 1)], 0).astype(BF16)
    hid = np.arange(w) // RWKV_HEAD
    ones_bd = jnp.asarray((hid[:, None] == hid[None, :]).astype(np.float32), BF16)
    tid = np.arange(tm)
    tri = jnp.asarray(((tid[:, None] >= tid[None, :]) &
                       (tid[:, None] // WKV_CHUNK == tid[None, :] // WKV_CHUNK)).astype(np.float32), BF16)
    row = lambda t: t.reshape(1, -1)
    tok = lambda wd: pl.BlockSpec((1, tm, wd), lambda bi, i: (bi, i, 0))
    prev = pl.BlockSpec((1, 8, width), lambda bi, i: (bi, jnp.maximum(i * (tm // 8) - 1, 0), 0))
    n_pack_rows = (s // WKV_CHUNK) * PACK_LANES
    p_spec = pl.BlockSpec((1, (tm // WKV_CHUNK) * PACK_LANES, w), lambda bi, i: (bi, i, 0))
    return pl.pallas_call(
        functools.partial(_wkv_prep_kernel, tm=tm),
        grid=(b, s // tm),
        in_specs=[tok(width), prev, _const_spec((1, width)), _const_spec(wlora.shape)]
                 + [_const_spec((1, w))] * 5 + [_const_spec(ones_bd.shape), _const_spec(tri.shape)],
        out_specs=[tok(w), tok(w), tok(w), p_spec, tok(w)],
        out_shape=[jax.ShapeDtypeStruct((b, s, w), BF16),
                   jax.ShapeDtypeStruct((b, s, w), F32),
                   jax.ShapeDtypeStruct((b, s, w), F32),
                   jax.ShapeDtypeStruct((b, n_pack_rows, w), BF16),
                   jax.ShapeDtypeStruct((b, s, w), BF16)],
        compiler_params=_params("parallel", "parallel"),
    )(rw, rw, row(mu), wlora, row(w0), row(a0), row(k_k), row(k_a), row(r_k), ones_bd, tri)


def _wkv_scan_kernel(rhat_ref, y0_ref, q_ref, p_ref, y_ref, s_ref):
    @pl.when(pl.program_id(0) == 0)
    def _():
        s_ref[...] = jnp.zeros_like(s_ref)

    nb = s_ref.shape[0]
    for b in range(nb):
        for g in range(RWKV_WIDTH // PACK_LANES):
            ls = slice(g * PACK_LANES, (g + 1) * PACK_LANES)
            s = s_ref[b, :, ls]
            y = _pdot_nt(rhat_ref[b, :, ls], s) + y0_ref[b, :, ls]
            y_ref[b, :, ls] = y.astype(y_ref.dtype)
            s_ref[b, :, ls] = (jnp.dot(s.astype(BF16), p_ref[b, :, ls], preferred_element_type=F32)
                               + q_ref[b, :, ls])


def _wkv_scan(rhat, y0, q, p):
    b, s, w = rhat.shape
    c = WKV_CHUNK
    blk = pl.BlockSpec((b, c, w), lambda ci: (0, ci, 0))
    return pl.pallas_call(
        _wkv_scan_kernel,
        grid=(s // c,),
        in_specs=[blk, blk, blk, pl.BlockSpec((b, PACK_LANES, w), lambda ci: (0, ci, 0))],
        out_specs=blk,
        out_shape=jax.ShapeDtypeStruct((b, s, w), F32),
        scratch_shapes=[pltpu.VMEM((b, RWKV_HEAD, w), F32)],
        compiler_params=_params("arbitrary"),
    )(rhat, y0, q, p)


def _epilogue_kernel(x_ref, gate_ref, attn_ref, ga_ref, y_ref, bonus_ref, gb_ref, ma_ref, mb_ref,
                     ones_ref, gng_ref, gnb_ref, wpa_ref, wpb_ref, wout_ref, pg_ref, pb_ref, o_ref):
    silu = lambda t: t * jax.nn.sigmoid(t)
    ones_bd = ones_ref[...]
    headmean = lambda t: jnp.dot(t.astype(BF16), ones_bd, preferred_element_type=F32) * (1.0 / RWKV_HEAD)

    y = y_ref[0]
    yc = y - headmean(y)
    yn = yc * lax.rsqrt(headmean(yc * yc) + GN_EPS)
    yb = yn * gng_ref[...] + gnb_ref[...] + bonus_ref[0].astype(F32)
    yb = yb * silu(gb_ref[0].astype(F32))
    ya = attn_ref[0].astype(F32) * silu(ga_ref[0].astype(F32))
    ya_p = jnp.dot(ya.astype(BF16), wpa_ref[...], preferred_element_type=F32)
    yb_p = jnp.dot(yb.astype(BF16), wpb_ref[...], preferred_element_type=F32)
    merged = (jax.nn.sigmoid(ma_ref[0].astype(F32)) * ya_p
              + jax.nn.sigmoid(mb_ref[0].astype(F32)) * yb_p)
    sub = jnp.dot(merged.astype(BF16), wout_ref[...], preferred_element_type=F32)
    z = ALPHA * x_ref[0] + (1.0 + gate_ref[0]) * sub
    zc = z - jnp.mean(z, -1, keepdims=True)
    zn = zc * lax.rsqrt(jnp.mean(zc * zc, -1, keepdims=True) + LN_EPS)
    o_ref[0] = (zn * pg_ref[...] + pb_ref[...]).astype(o_ref.dtype)


def _epilogue(x, gate, attn, ga, y, bonus, gb, ma, mb, gn_g, gn_b, w_proj_a, w_proj_b, w_out,
              post_g, post_b, tm):
    b, s, d = x.shape
    w = RWKV_WIDTH
    hid = np.arange(w) // RWKV_HEAD
    ones_bd = jnp.asarray((hid[:, None] == hid[None, :]).astype(np.float32), BF16)
    row = lambda t: t.reshape(1, -1)
    tok = lambda wd: pl.BlockSpec((1, tm, wd), lambda bi, i: (bi, i, 0))
    vec = pl.BlockSpec((1, 1, d), lambda bi, i: (bi, 0, 0))
    return pl.pallas_call(
        _epilogue_kernel,
        grid=(b, s // tm),
        in_specs=[tok(d), vec, tok(w), tok(w), tok(w), tok(w), tok(w), tok(d), tok(d),
                  _const_spec((w, w)), _const_spec((1, w)), _const_spec((1, w)),
                  _const_spec((w, d)), _const_spec((w, d)), _const_spec((d, d)),
                  _const_spec((1, d)), _const_spec((1, d))],
        out_specs=tok(d),
        out_shape=jax.ShapeDtypeStruct((b, s, d), x.dtype),
        compiler_params=_params("parallel", "parallel"),
    )(x, gate, attn, ga, y, bonus, gb, ma, mb, ones_bd, row(gn_g), row(gn_b),
      w_proj_a.astype(BF16), w_proj_b.astype(BF16), w_out.astype(BF16), row(post_g), row(post_b))


def _pad_w_in(w_in):
    used = MLA_Q_RANK + MLA_KV_RANK + MLA_ROPE
    pad = jnp.zeros((w_in.shape[0], QKR_WIDTH - used), w_in.dtype)
    return jnp.concatenate([w_in[:, :used], pad, w_in[:, used:]], axis=1).astype(BF16)


def _layer(x, c, pos, w_ada, b_ada, w_in, q_norm_g, w_uq, kv_norm_g, w_ukv, mu_rwkv, w0, w_decay_up,
           a0, w_iclr_up, k_k, k_a, r_k, gn_g, gn_b, w_proj_a, w_proj_b, w_out, post_g, post_b):
    b, s, d = x.shape
    tm = min(512, s)
    ada = _adaln(c, w_ada, b_ada)
    shift, scale, gate = (ada[:, j * d:(j + 1) * d].reshape(b, 1, d) for j in range(3))
    qkr, ga, rw, gb, ma, mb = _ln_proj(x, scale, shift, _pad_w_in(w_in), tm)
    qt, k, vt = _mla_prep(qkr, pos, q_norm_g, kv_norm_g, w_uq, w_ukv, tm)
    attn = _attention(qt, k, vt)
    rhat, y0, qs, p, bonus = _wkv_prep(rw, mu_rwkv, w0, w_decay_up, a0, w_iclr_up, k_k, k_a,
                                       r_k.reshape(-1), min(256, s))
    y = _wkv_scan(rhat, y0, qs, p)
    return _epilogue(x, gate, attn, ga, y, bonus, gb, ma, mb, gn_g, gn_b, w_proj_a, w_proj_b, w_out,
                     post_g, post_b, tm)


def kernel(x, c, positions, w_ada, b_ada, w_in, q_norm_g, w_uq, kv_norm_g, w_ukv, mu_rwkv, w0,
           w_decay_up, a0, w_iclr_up, k_k, k_a, r_k, gn_g, gn_b, w_proj_a, w_proj_b, w_out, post_g,
           post_b):
    pos = positions.astype(F32)[..., None]
    for l in range(w_ada.shape[0]):
        x = _layer(x, c, pos, w_ada[l], b_ada[l], w_in[l], q_norm_g[l], w_uq[l], kv_norm_g[l],
                   w_ukv[l], mu_rwkv[l], w0[l], w_decay_up[l], a0[l], w_iclr_up[l], k_k[l], k_a[l],
                   r_k[l], gn_g[l], gn_b[l], w_proj_a[l], w_proj_b[l], w_out[l], post_g[l], post_b[l])
    return x
```

```python
import functools
import math

import jax
import jax.numpy as jnp
import numpy as np
from jax import lax
from jax.experimental import pallas as pl
from jax.experimental.pallas import tpu as pltpu

F32 = jnp.float32
BF16 = jnp.bfloat16

D_MODEL = 1024
LN_EPS = 1e-5
RMS_EPS = 1e-6
GN_EPS = 64e-5

MLA_HEADS = 8
MLA_NOPE = 64
MLA_ROPE = 32
MLA_V = 64
MLA_QK = MLA_NOPE + MLA_ROPE
MLA_Q_RANK = 256
MLA_KV_RANK = 128
MLA_WIDTH = MLA_HEADS * MLA_V
ROPE_THETA = 10000.0
ATTN_CHUNK = 64

RWKV_HEADS = 8
RWKV_HEAD = 64
RWKV_WIDTH = RWKV_HEADS * RWKV_HEAD
DECAY_LORA = 64
ICLR_LORA = 64
RWKV_SHIFT_WIDTH = 3 * RWKV_WIDTH + DECAY_LORA + ICLR_LORA
WKV_CHUNK = 64

DEPTH = 1
ALPHA = (2.0 * DEPTH) ** 0.25

LANES = 128
HEAD_PAD = 128
V_ROWS = 80
KEY_TILE = 256
PACK_HEADS = 2
PACK_LANES = PACK_HEADS * RWKV_HEAD

QKR_WIDTH = 512
VMEM_LIMIT = 56 * 1024 * 1024

NEG_BIG = -1e30


def _const_spec(shape):
    n = len(shape)
    return pl.BlockSpec(shape, lambda *_: (0,) * n)


def _params(*sem):
    return pltpu.CompilerParams(dimension_semantics=sem, vmem_limit_bytes=VMEM_LIMIT)


def _adaln_kernel(c_ref, w_ref, b_ref, o_ref):
    c = c_ref[...]
    sc = c * jax.nn.sigmoid(c)
    o_ref[...] = jnp.dot(sc.astype(BF16), w_ref[...], preferred_element_type=F32) + b_ref[...]


def _adaln(c, w_ada, b_ada):
    b = c.shape[0]
    return pl.pallas_call(
        _adaln_kernel,
        out_shape=jax.ShapeDtypeStruct((b, 3 * D_MODEL), F32),
        compiler_params=pltpu.CompilerParams(vmem_limit_bytes=VMEM_LIMIT),
    )(c, w_ada.astype(BF16), b_ada.reshape(1, -1))


_PROJ_WIDTHS = (QKR_WIDTH, MLA_WIDTH, RWKV_SHIFT_WIDTH, RWKV_WIDTH, D_MODEL, D_MODEL)


def _ln_proj_kernel(x_ref, scale_ref, shift_ref, w_ref, *out_refs):
    x = x_ref[0]
    xc = x - jnp.mean(x, -1, keepdims=True)
    h = xc * lax.rsqrt(jnp.mean(xc * xc, -1, keepdims=True) + LN_EPS)
    hb = (h * (1.0 + scale_ref[0]) + shift_ref[0]).astype(BF16)
    off = 0
    for o_ref, width in zip(out_refs, _PROJ_WIDTHS):
        o_ref[0] = jnp.dot(hb, w_ref[:, off:off + width], preferred_element_type=F32).astype(o_ref.dtype)
        off += width


def _ln_proj(x, scale, shift, w_in_pad, tm):
    b, s, d = x.shape
    tok = lambda w: pl.BlockSpec((1, tm, w), lambda bi, i: (bi, i, 0))
    vec = pl.BlockSpec((1, 1, d), lambda bi, i: (bi, 0, 0))
    return pl.pallas_call(
        _ln_proj_kernel,
        grid=(b, s // tm),
        in_specs=[tok(d), vec, vec, _const_spec(w_in_pad.shape)],
        out_specs=[tok(w) for w in _PROJ_WIDTHS],
        out_shape=[jax.ShapeDtypeStruct((b, s, w), BF16) for w in _PROJ_WIDTHS],
        compiler_params=_params("parallel", "parallel"),
    )(x, scale, shift, w_in_pad)


def _mla_prep_kernel(qkr_ref, pos_ref, inv_ref, qg_ref, kvg_ref, wqp_ref, wqr_ref, wk_ref, wvt_ref,
                     qt_out, k_out, vt_out):
    tm = qkr_ref.shape[1]
    half = MLA_ROPE // 2
    t = qkr_ref[0].astype(F32)
    qc = t[:, :MLA_Q_RANK]
    kvc = t[:, MLA_Q_RANK:MLA_Q_RANK + MLA_KV_RANK]
    kr = t[:, MLA_Q_RANK + MLA_KV_RANK:]
    qn = qc * lax.rsqrt(jnp.mean(qc * qc, -1, keepdims=True) + RMS_EPS) * qg_ref[...]
    kvn = kvc * lax.rsqrt(jnp.mean(kvc * kvc, -1, keepdims=True) + RMS_EPS) * kvg_ref[...]
    qn_t = qn.T.astype(BF16)
    kvn_t = kvn.T.astype(BF16)
    kr_t = kr.T

    ang = inv_ref[...] * pos_ref[0]
    cos_h = jnp.cos(ang)
    sin_h = jnp.sin(ang)
    cos_r = jnp.concatenate([cos_h, cos_h], axis=0)
    sin_r = jnp.concatenate([sin_h, sin_h], axis=0)

    plain_t = jnp.dot(wqp_ref[...], qn_t, preferred_element_type=F32)
    rot_t = jnp.dot(wqr_ref[...], qn_t, preferred_element_type=F32)
    scale = MLA_QK ** -0.5 * math.log2(math.e)
    zpad = jnp.zeros((HEAD_PAD - MLA_QK, tm), BF16)
    for h in range(MLA_HEADS):
        base = h * HEAD_PAD
        nope = (plain_t[base:base + MLA_NOPE] * scale).astype(BF16)
        pe = ((plain_t[base + MLA_NOPE:base + MLA_QK] * cos_r
               + rot_t[h * MLA_ROPE:(h + 1) * MLA_ROPE] * sin_r) * scale).astype(BF16)
        q_t = jnp.concatenate([nope, pe, zpad], axis=0)
        for kt in range(tm // KEY_TILE):
            qt_out[0, kt, base:base + HEAD_PAD, :] = q_t[:, kt * KEY_TILE:(kt + 1) * KEY_TILE]

    k1 = kr_t[:half]
    k2 = kr_t[half:MLA_ROPE]
    kpe_t = jnp.concatenate([jnp.zeros((MLA_NOPE, tm), F32),
                             k1 * cos_h - k2 * sin_h, k1 * sin_h + k2 * cos_h,
                             jnp.zeros((HEAD_PAD - MLA_QK, tm), F32)], axis=0)
    kpe = kpe_t.T
    ka = jnp.dot(kvn.astype(BF16), wk_ref[...], preferred_element_type=F32)
    for h in range(MLA_HEADS):
        sl = slice(h * HEAD_PAD, (h + 1) * HEAD_PAD)
        k_out[0, :, sl] = (ka[:, sl] + kpe).astype(BF16)

    vt = jnp.dot(wvt_ref[...], kvn_t, preferred_element_type=F32)
    vrow = lax.broadcasted_iota(jnp.int32, (MLA_HEADS * V_ROWS, 1), 0) % V_ROWS
    vt = (vt + jnp.where(vrow == MLA_V, 1.0, 0.0)).astype(BF16)
    for kt in range(tm // KEY_TILE):
        vt_out[0, kt] = vt[:, kt * KEY_TILE:(kt + 1) * KEY_TILE]


def _mla_weights(w_uq, w_ukv):
    half = MLA_ROPE // 2
    wq = w_uq.reshape(MLA_Q_RANK, MLA_HEADS, MLA_QK)
    zq = jnp.zeros((MLA_Q_RANK, MLA_HEADS, HEAD_PAD - MLA_QK), F32)
    wq_plain_t = jnp.concatenate([wq, zq], -1).reshape(MLA_Q_RANK, -1).T.astype(BF16)
    t1 = wq[:, :, MLA_NOPE:MLA_NOPE + half]
    t2 = wq[:, :, MLA_NOPE + half:]
    wq_rot_t = jnp.concatenate([-t2, t1], -1).reshape(MLA_Q_RANK, -1).T.astype(BF16)

    wkv = w_ukv.reshape(MLA_KV_RANK, MLA_HEADS, MLA_NOPE + MLA_V)
    zk = jnp.zeros((MLA_KV_RANK, MLA_HEADS, HEAD_PAD - MLA_NOPE), F32)
    wk = jnp.concatenate([wkv[:, :, :MLA_NOPE], zk], -1).reshape(MLA_KV_RANK, -1).astype(BF16)
    zv = jnp.zeros((MLA_KV_RANK, MLA_HEADS, V_ROWS - MLA_V), F32)
    wvt = jnp.concatenate([wkv[:, :, MLA_NOPE:], zv], -1).reshape(MLA_KV_RANK, -1).T.astype(BF16)
    return wq_plain_t, wq_rot_t, wk, wvt


def _mla_prep(qkr, pos, q_norm_g, kv_norm_g, w_uq, w_ukv, tm):
    b, s, _ = qkr.shape
    wqp, wqr, wk, wvt = _mla_weights(w_uq, w_ukv)
    half = MLA_ROPE // 2
    inv = (ROPE_THETA ** (-jnp.arange(0, MLA_ROPE, 2, dtype=F32) / MLA_ROPE)).reshape(half, 1)
    width = MLA_HEADS * HEAD_PAD
    vrows = MLA_HEADS * V_ROWS
    tok = lambda w: pl.BlockSpec((1, tm, w), lambda bi, i: (bi, i, 0))
    return pl.pallas_call(
        _mla_prep_kernel,
        grid=(b, s // tm),
        in_specs=[tok(QKR_WIDTH), pl.BlockSpec((1, 1, tm), lambda bi, i: (bi, 0, i)),
                  _const_spec((half, 1)), _const_spec((1, MLA_Q_RANK)),
                  _const_spec((1, MLA_KV_RANK)), _const_spec(wqp.shape), _const_spec(wqr.shape),
                  _const_spec(wk.shape), _const_spec(wvt.shape)],
        out_specs=[pl.BlockSpec((1, tm // KEY_TILE, width, KEY_TILE), lambda bi, i: (bi, i, 0, 0)),
                   tok(width),
                   pl.BlockSpec((1, tm // KEY_TILE, vrows, KEY_TILE), lambda bi, i: (bi, i, 0, 0))],
        out_shape=[jax.ShapeDtypeStruct((b, s // KEY_TILE, width, KEY_TILE), BF16),
                   jax.ShapeDtypeStruct((b, s, width), BF16),
                   jax.ShapeDtypeStruct((b, s // KEY_TILE, vrows, KEY_TILE), BF16)],
        compiler_params=_params("parallel", "parallel"),
    )(qkr, pos, inv, q_norm_g.reshape(1, -1), kv_norm_g.reshape(1, -1), wqp, wqr, wk, wvt)


ATTN_SUBTILES = 2


def _attn_kernel(qt_ref, k_ref, vt_ref, o_ref, s_buf):
    ts = KEY_TILE
    nsub = ATTN_SUBTILES
    n_qt = o_ref.shape[1] // (nsub * ts)
    key_chunk = lax.broadcasted_iota(jnp.int32, (ts, ts), 0) // ATTN_CHUNK
    qry_chunk = lax.broadcasted_iota(jnp.int32, (ts, ts), 1) // ATTN_CHUNK
    diag_mask = key_chunk <= qry_chunk
    lanes = [slice(hh * HEAD_PAD, (hh + 1) * HEAD_PAD) for hh in range(2)]
    chains = [(hh, sb) for hh in range(2) for sb in range(nsub)]

    def scores(i, j, slot, first_sub):
        if isinstance(j, int):
            start = j * ts
        else:
            start = pl.multiple_of(j * ts, ts)
        kj = [k_ref[0, pl.ds(start, ts), lanes[hh]] for hh in range(2)]
        for n, ch in enumerate(chains):
            if ch[1] >= first_sub:
                q_t = qt_ref[0, i * nsub + ch[1], lanes[ch[0]], :]
                s_buf[slot, n] = jnp.dot(kj[ch[0]], q_t, preferred_element_type=F32)

    def consume(j, slot, carry, masked_sub, first_sub):
        vj = [vt_ref[0, j, hh * V_ROWS:(hh + 1) * V_ROWS, :] for hh in range(2)]
        active = [(n, ch) for n, ch in enumerate(chains) if ch[1] >= first_sub]
        s = {ch: s_buf[slot, n] for n, ch in active}
        for _, ch in active:
            if ch[1] == masked_sub:
                s[ch] = jnp.where(diag_mask, s[ch], NEG_BIG)
        m_new = {ch: jnp.maximum(carry[ch][0], jnp.max(s[ch], axis=0, keepdims=True)) for _, ch in active}
        p = {ch: jnp.exp2(s[ch] - m_new[ch]).astype(BF16) for _, ch in active}
        alpha = {ch: jnp.exp2(carry[ch][0] - m_new[ch]) for _, ch in active}
        out = dict(carry)
        for _, ch in active:
            pv = jnp.dot(vj[ch[0]], p[ch], preferred_element_type=F32)
            out[ch] = (m_new[ch], carry[ch][1] * alpha[ch] + pv)
        return out

    init = tuple(v for _ in chains
                 for v in (jnp.full((1, ts), NEG_BIG, F32), jnp.zeros((V_ROWS, ts), F32)))
    scores(0, 0, 0, 0)
    for i in range(n_qt):
        sa, sb_ = (0, 1) if i % 2 == 0 else (2, 3)
        next_sa = 2 if i % 2 == 0 else 0

        def loop_body(t, flat, i=i, sa=sa, sb_=sb_):
            carry = {ch: (flat[2 * n], flat[2 * n + 1]) for n, ch in enumerate(chains)}
            scores(i, 2 * t + 1, sb_, 0)
            carry = consume(2 * t, sa, carry, -1, 0)
            scores(i, 2 * t + 2, sa, 0)
            carry = consume(2 * t + 1, sb_, carry, -1, 0)
            return tuple(v for ch in chains for v in carry[ch])

        flat = lax.fori_loop(0, i, loop_body, init)
        carry = {ch: (flat[2 * n], flat[2 * n + 1]) for n, ch in enumerate(chains)}
        scores(i, 2 * i + 1, sb_, 1)
        if i + 1 < n_qt:
            scores(i + 1, 0, next_sa, 0)
        carry = consume(2 * i, sa, carry, 0, 0)
        carry = consume(2 * i + 1, sb_, carry, 1, 1)

        for sb in range(nsub):
            normed = []
            for hh in range(2):
                acc = carry[(hh, sb)][1]
                normed.append(acc[:MLA_V] / acc[MLA_V:MLA_V + 1])
            out_t = jnp.concatenate(normed, axis=0)
            row0 = (i * nsub + sb) * ts
            o_ref[0, row0:row0 + ts, :] = out_t.T.astype(o_ref.dtype)


def _attention(qt, k, vt):
    b, s, _ = k.shape
    pairs = MLA_HEADS // 2
    nsub = ATTN_SUBTILES
    return pl.pallas_call(
        _attn_kernel,
        grid=(b, pairs),
        in_specs=[pl.BlockSpec((1, s // KEY_TILE, 2 * HEAD_PAD, KEY_TILE), lambda bi, g: (bi, 0, g, 0)),
                  pl.BlockSpec((1, s, 2 * HEAD_PAD), lambda bi, g: (bi, 0, g)),
                  pl.BlockSpec((1, s // KEY_TILE, 2 * V_ROWS, KEY_TILE), lambda bi, g: (bi, 0, g, 0))],
        out_specs=pl.BlockSpec((1, s, 2 * MLA_V), lambda bi, g: (bi, 0, g)),
        out_shape=jax.ShapeDtypeStruct((b, s, MLA_WIDTH), BF16),
        scratch_shapes=[pltpu.VMEM((4, 2 * nsub, KEY_TILE, KEY_TILE), F32)],
        compiler_params=_params("parallel", "parallel"),
    )(qt, k, vt)


def _lane_head(width):
    return lax.broadcasted_iota(jnp.int32, (1, width), 1) // RWKV_HEAD


def _bd_stack(x):
    head = _lane_head(x.shape[1])
    xb = x.astype(BF16)
    return jnp.concatenate([jnp.where(head == h, xb, jnp.zeros_like(xb)) for h in range(PACK_HEADS)], axis=0)


def _pdot(a, b):
    return jnp.dot(a.astype(BF16), _bd_stack(b), preferred_element_type=F32)


def _pdot2(a, b1, b2):
    rhs = jnp.concatenate([_bd_stack(b1), _bd_stack(b2)], axis=1)
    out = jnp.dot(a.astype(BF16), rhs, preferred_element_type=F32)
    return out[:, :PACK_LANES], out[:, PACK_LANES:]


def _pdot_nt(a, b):
    return lax.dot_general(a.astype(BF16), _bd_stack(b), (((1,), (1,)), ((), ())),
                           preferred_element_type=F32)


def _split3(x):
    x1 = x.astype(BF16)
    r1 = x - x1.astype(F32)
    x2 = r1.astype(BF16)
    x3 = (r1 - x2.astype(F32)).astype(BF16)
    return x1, x2, x3


def _wkv_prep_kernel(rw_ref, prev_ref, mu_ref, wlora_ref, w0_ref, a0_ref, kk_ref, ka_ref, rk_ref,
                     ones_ref, tri_ref, rhat_ref, y0_ref, q_ref, p_ref, bonus_ref, *, tm):
    i = pl.program_id(1)
    c = WKV_CHUNK
    w = RWKV_WIDTH
    u_raw = rw_ref[0].astype(F32)
    prev_row = prev_ref[0, 7:8, :].astype(F32) * (i > 0).astype(F32)
    row = lax.broadcasted_iota(jnp.int32, (tm, 1), 0)
    u_prev = jnp.where(row == 0, prev_row, pltpu.roll(u_raw, 1, axis=0))
    u = u_raw + (u_prev - u_raw) * mu_ref[...]
    r = u[:, :w]
    k = u[:, w:2 * w]
    v = u[:, 2 * w:3 * w]
    lora_in = u[:, 3 * w:]
    lane = lax.broadcasted_iota(jnp.int32, (1, DECAY_LORA + ICLR_LORA), 1)
    lora_in = jnp.where(lane < DECAY_LORA, jnp.tanh(lora_in), lora_in)
    lora = jnp.dot(lora_in.astype(BF16), wlora_ref[...], preferred_element_type=F32)
    lw = -math.exp(-0.5) * jax.nn.sigmoid(w0_ref[...] + lora[:, :w])
    a = jax.nn.sigmoid(a0_ref[...] + lora[:, w:])

    ones_bd = ones_ref[...]
    headsum = lambda t: jnp.dot(t.astype(BF16), ones_bd, preferred_element_type=F32)
    kk = k * kk_ref[...]
    kk = kk * lax.rsqrt(jnp.maximum(headsum(kk * kk), 1e-24))
    k = k * (1.0 + (a - 1.0) * ka_ref[...])
    bonus_ref[0] = (headsum(r * k * rk_ref[...]) * v).astype(bonus_ref.dtype)
    a_vec = -kk
    b_vec = kk * a

    tri = tri_ref[...]
    cum = sum(jnp.dot(tri, part, preferred_element_type=F32) for part in _split3(lw))

    t_idx = lax.broadcasted_iota(jnp.int32, (c, PACK_LANES), 0)
    s_idx = lax.broadcasted_iota(jnp.int32, (c, PACK_LANES), 1) % RWKV_HEAD
    strict = s_idx < t_idx
    incl = s_idx <= t_idx
    eye = (s_idx == t_idx).astype(F32)
    prow = lax.broadcasted_iota(jnp.int32, (PACK_LANES, PACK_LANES), 0)
    pcol = lax.broadcasted_iota(jnp.int32, (PACK_LANES, PACK_LANES), 1)
    same_head = (prow // RWKV_HEAD) == (pcol // RWKV_HEAD)
    on_diag = prow == pcol
    head = _lane_head(PACK_LANES)

    nch = tm // c
    cum_last = jnp.concatenate(
        [jnp.broadcast_to(cum[(cc + 1) * c - 1:(cc + 1) * c, :], (c, w)) for cc in range(nch)], axis=0)
    e_pos = jnp.exp(cum)
    e_neg = jnp.exp(-cum)
    e_last = jnp.exp(cum_last - cum)
    at_all = a_vec * jnp.exp(cum - lw)
    bt_all = b_vec * e_neg
    kt_all = k * e_neg
    rt_all = r * e_pos
    bl_all = b_vec * e_last
    kl_all = k * e_last
    w_end_all = jnp.exp(cum_last)

    probs = [(cc, g) for cc in range(nch) for g in range(w // PACK_LANES)]
    sl = {pr: (slice(pr[0] * c, (pr[0] + 1) * c), slice(pr[1] * PACK_LANES, (pr[1] + 1) * PACK_LANES))
          for pr in probs}
    at = {pr: at_all[sl[pr]] for pr in probs}
    rt = {pr: rt_all[sl[pr]] for pr in probs}
    vv = {pr: v[sl[pr]] for pr in probs}
    m_bk = {pr: lax.dot_general(
        jnp.concatenate([at[pr], rt[pr]], axis=0).astype(BF16),
        jnp.concatenate([_bd_stack(bt_all[sl[pr]]), _bd_stack(kt_all[sl[pr]])], axis=0),
        (((1,), (1,)), ((), ())), preferred_element_type=F32) for pr in probs}
    m_b = {pr: m_bk[pr][:, :PACK_LANES] for pr in probs}
    m_k = {pr: m_bk[pr][:, PACK_LANES:] for pr in probs}
    l_ab = {pr: jnp.where(strict, m_b[pr][:c], 0.0) for pr in probs}
    t_inv = {pr: eye + l_ab[pr] for pr in probs}
    x = {pr: _pdot(l_ab[pr], l_ab[pr]) for pr in probs}
    levels = int(math.log2(c)) - 1
    for lvl in range(levels):
        if lvl < levels - 1:
            tx = {pr: _pdot2(x[pr], t_inv[pr], x[pr]) for pr in probs}
            t_inv = {pr: t_inv[pr] + tx[pr][0] for pr in probs}
            x = {pr: tx[pr][1] for pr in probs}
        else:
            t_inv = {pr: t_inv[pr] + _pdot(x[pr], t_inv[pr]) for pr in probs}
    akv = {pr: _pdot(jnp.where(strict, m_k[pr][:c], 0.0), vv[pr]) for pr in probs}
    au = {pr: _pdot2(t_inv[pr], at[pr], akv[pr]) for pr in probs}
    a_hat = {pr: au[pr][0] for pr in probs}
    u0 = {pr: au[pr][1] for pr in probs}
    for pr in probs:
        rs, ls = sl[pr]
        m_rb = jnp.where(incl, m_b[pr][c:], 0.0)
        m_rk = jnp.where(incl, m_k[pr][c:], 0.0)
        r_add, y_add = _pdot2(m_rb, a_hat[pr], u0[pr])
        rhat_ref[0, rs, ls] = (rt[pr] + r_add).astype(rhat_ref.dtype)
        y0_ref[0, rs, ls] = (y_add + _pdot(m_rk, vv[pr])).astype(y0_ref.dtype)
    for pr in probs:
        rs, ls = sl[pr]
        bl = bl_all[sl[pr]]
        p_full = jnp.dot(a_hat[pr].T.astype(BF16), bl.astype(BF16), preferred_element_type=F32)
        w_end = w_end_all[rs.start:rs.start + 1, ls]
        p_bd = jnp.where(same_head, p_full, 0.0) + jnp.where(on_diag, w_end, 0.0)
        p_ref[0, pr[0] * PACK_LANES:(pr[0] + 1) * PACK_LANES, ls] = p_bd.astype(p_ref.dtype)
        uv_t = jnp.concatenate([u0[pr], vv[pr]], axis=0).T
        bk = jnp.concatenate([bl, kl_all[sl[pr]]], axis=0).astype(BF16)
        f = jnp.dot(uv_t.astype(BF16), bk, preferred_element_type=F32)
        q_pack = sum(jnp.where(head == h, f[h * RWKV_HEAD:(h + 1) * RWKV_HEAD], 0.0)
                     for h in range(PACK_HEADS))
        q_ref[0, rs, ls] = q_pack.astype(q_ref.dtype)


def _wkv_prep(rw, mu, w0, w_decay_up, a0, w_iclr_up, k_k, k_a, r_k, tm):
    b, s, width = rw.shape
    w = RWKV_WIDTH
    wlora = jnp.concatenate(
        [jnp.concatenate([w_decay_up, jnp.zeros((DECAY_LORA, w), F32)], 1),
         jnp.concatenate([jnp.zeros((ICLR_LORA, w), F32), w_iclr_up], 1)], 0).astype(BF16)
    hid = np.arange(w) // RWKV_HEAD
    ones_bd = jnp.asarray((hid[:, None] == hid[None, :]).astype(np.float32), BF16)
    tid = np.arange(tm)
    tri = jnp.asarray(((tid[:, None] >= tid[None, :]) &
                       (tid[:, None] // WKV_CHUNK == tid[None, :] // WKV_CHUNK)).astype(np.float32), BF16)
    row = lambda t: t.reshape(1, -1)
    tok = lambda wd: pl.BlockSpec((1, tm, wd), lambda bi, i: (bi, i, 0))
    prev = pl.BlockSpec((1, 8, width), lambda bi, i: (bi, jnp.maximum(i * (tm // 8) - 1, 0), 0))
    n_pack_rows = (s // WKV_CHUNK) * PACK_LANES
    p_spec = pl.BlockSpec((1, (tm // WKV_CHUNK) * PACK_LANES, w), lambda bi, i: (bi, i, 0))
    return pl.pallas_call(
        functools.partial(_wkv_prep_kernel, tm=tm),
        grid=(b, s // tm),
        in_specs=[tok(width), prev, _const_spec((1, width)), _const_spec(wlora.shape)]
                 + [_const_spec((1, w))] * 5 + [_const_spec(ones_bd.shape), _const_spec(tri.shape)],
        out_specs=[tok(w), tok(w), tok(w), p_spec, tok(w)],
        out_shape=[jax.ShapeDtypeStruct((b, s, w), BF16),
                   jax.ShapeDtypeStruct((b, s, w), F32),
                   jax.ShapeDtypeStruct((b, s, w), F32),
                   jax.ShapeDtypeStruct((b, n_pack_rows, w), BF16),
                   jax.ShapeDtypeStruct((b, s, w), BF16)],
        compiler_params=_params("parallel", "parallel"),
    )(rw, rw, row(mu), wlora, row(w0), row(a0), row(k_k), row(k_a), row(r_k), ones_bd, tri)


def _wkv_scan_kernel(rhat_ref, y0_ref, q_ref, p_ref, y_ref, s_ref):
    @pl.when(pl.program_id(0) == 0)
    def _():
        s_ref[...] = jnp.zeros_like(s_ref)

    nb = s_ref.shape[0]
    for b in range(nb):
        for g in range(RWKV_WIDTH // PACK_LANES):
            ls = slice(g * PACK_LANES, (g + 1) * PACK_LANES)
            s = s_ref[b, :, ls]
            y = _pdot_nt(rhat_ref[b, :, ls], s) + y0_ref[b, :, ls]
            y_ref[b, :, ls] = y.astype(y_ref.dtype)
            s_ref[b, :, ls] = (jnp.dot(s.astype(BF16), p_ref[b, :, ls], preferred_element_type=F32)
                               + q_ref[b, :, ls])


def _wkv_scan(rhat, y0, q, p):
    b, s, w = rhat.shape
    c = WKV_CHUNK
    blk = pl.BlockSpec((b, c, w), lambda ci: (0, ci, 0))
    return pl.pallas_call(
        _wkv_scan_kernel,
        grid=(s // c,),
        in_specs=[blk, blk, blk, pl.BlockSpec((b, PACK_LANES, w), lambda ci: (0, ci, 0))],
        out_specs=blk,
        out_shape=jax.ShapeDtypeStruct((b, s, w), F32),
        scratch_shapes=[pltpu.VMEM((b, RWKV_HEAD, w), F32)],
        compiler_params=_params("arbitrary"),
    )(rhat, y0, q, p)


def _epilogue_kernel(x_ref, gate_ref, attn_ref, ga_ref, y_ref, bonus_ref, gb_ref, ma_ref, mb_ref,
                     ones_ref, gng_ref, gnb_ref, wpa_ref, wpb_ref, wout_ref, pg_ref, pb_ref, o_ref):
    silu = lambda t: t * jax.nn.sigmoid(t)
    ones_bd = ones_ref[...]
    headmean = lambda t: jnp.dot(t.astype(BF16), ones_bd, preferred_element_type=F32) * (1.0 / RWKV_HEAD)

    y = y_ref[0]
    yc = y - headmean(y)
    yn = yc * lax.rsqrt(headmean(yc * yc) + GN_EPS)
    yb = yn * gng_ref[...] + gnb_ref[...] + bonus_ref[0].astype(F32)
    yb = yb.astype(BF16) * silu(gb_ref[0])
    ya = attn_ref[0] * silu(ga_ref[0])
    ya_p = jnp.dot(ya, wpa_ref[...], preferred_element_type=F32)
    yb_p = jnp.dot(yb, wpb_ref[...], preferred_element_type=F32)
    merged = (jax.nn.sigmoid(ma_ref[0]) * ya_p.astype(BF16)
              + jax.nn.sigmoid(mb_ref[0]) * yb_p.astype(BF16))
    sub = jnp.dot(merged, wout_ref[...], preferred_element_type=F32)
    z = ALPHA * x_ref[0] + (1.0 + gate_ref[0]) * sub
    zc = z - jnp.mean(z, -1, keepdims=True)
    zn = zc * lax.rsqrt(jnp.mean(zc * zc, -1, keepdims=True) + LN_EPS)
    o_ref[0] = (zn * pg_ref[...] + pb_ref[...]).astype(o_ref.dtype)


def _epilogue(x, gate, attn, ga, y, bonus, gb, ma, mb, gn_g, gn_b, w_proj_a, w_proj_b, w_out,
              post_g, post_b, tm):
    b, s, d = x.shape
    w = RWKV_WIDTH
    hid = np.arange(w) // RWKV_HEAD
    ones_bd = jnp.asarray((hid[:, None] == hid[None, :]).astype(np.float32), BF16)
    row = lambda t: t.reshape(1, -1)
    tok = lambda wd: pl.BlockSpec((1, tm, wd), lambda bi, i: (bi, i, 0))
    vec = pl.BlockSpec((1, 1, d), lambda bi, i: (bi, 0, 0))
    return pl.pallas_call(
        _epilogue_kernel,
        grid=(b, s // tm),
        in_specs=[tok(d), vec, tok(w), tok(w), tok(w), tok(w), tok(w), tok(d), tok(d),
                  _const_spec((w, w)), _const_spec((1, w)), _const_spec((1, w)),
                  _const_spec((w, d)), _const_spec((w, d)), _const_spec((d, d)),
                  _const_spec((1, d)), _const_spec((1, d))],
        out_specs=tok(d),
        out_shape=jax.ShapeDtypeStruct((b, s, d), x.dtype),
        compiler_params=_params("parallel", "parallel"),
    )(x, gate, attn, ga, y, bonus, gb, ma, mb, ones_bd, row(gn_g), row(gn_b),
      w_proj_a.astype(BF16), w_proj_b.astype(BF16), w_out.astype(BF16), row(post_g), row(post_b))


def _pad_w_in(w_in):
    used = MLA_Q_RANK + MLA_KV_RANK + MLA_ROPE
    pad = jnp.zeros((w_in.shape[0], QKR_WIDTH - used), w_in.dtype)
    return jnp.concatenate([w_in[:, :used], pad, w_in[:, used:]], axis=1).astype(BF16)


def _layer(x, c, pos, w_ada, b_ada, w_in, q_norm_g, w_uq, kv_norm_g, w_ukv, mu_rwkv, w0, w_decay_up,
           a0, w_iclr_up, k_k, k_a, r_k, gn_g, gn_b, w_proj_a, w_proj_b, w_out, post_g, post_b):
    b, s, d = x.shape
    tm = min(512, s)
    ada = _adaln(c, w_ada, b_ada)
    shift, scale, gate = (ada[:, j * d:(j + 1) * d].reshape(b, 1, d) for j in range(3))
    qkr, ga, rw, gb, ma, mb = _ln_proj(x, scale, shift, _pad_w_in(w_in), tm)
    qt, k, vt = _mla_prep(qkr, pos, q_norm_g, kv_norm_g, w_uq, w_ukv, tm)
    attn = _attention(qt, k, vt)
    rhat, y0, qs, p, bonus = _wkv_prep(rw, mu_rwkv, w0, w_decay_up, a0, w_iclr_up, k_k, k_a,
                                       r_k.reshape(-1), min(256, s))
    y = _wkv_scan(rhat, y0, qs, p)
    return _epilogue(x, gate, attn, ga, y, bonus, gb, ma, mb, gn_g, gn_b, w_proj_a, w_proj_b, w_out,
                     post_g, post_b, tm)


def kernel(x, c, positions, w_ada, b_ada, w_in, q_norm_g, w_uq, kv_norm_g, w_ukv, mu_rwkv, w0,
           w_decay_up, a0, w_iclr_up, k_k, k_a, r_k, gn_g, gn_b, w_proj_a, w_proj_b, w_out, post_g,
           post_b):
    pos = positions.astype(F32)[:, None, :]
    for l in range(w_ada.shape[0]):
        x = _layer(x, c, pos, w_ada[l], b_ada[l], w_in[l], q_norm_g[l], w_uq[l], kv_norm_g[l],
                   w_ukv[l], mu_rwkv[l], w0[l], w_decay_up[l], a0[l], w_iclr_up[l], k_k[l], k_a[l],
                   r_k[l], gn_g[l], gn_b[l], w_proj_a[l], w_proj_b[l], w_out[l], post_g[l], post_b[l])
    return x
```

```python
import functools
import math

import jax
import jax.numpy as jnp
import numpy as np
from jax import lax
from jax.experimental import pallas as pl
from jax.experimental.pallas import tpu as pltpu

F32 = jnp.float32
BF16 = jnp.bfloat16

D_MODEL = 1024
LN_EPS = 1e-5
RMS_EPS = 1e-6
GN_EPS = 64e-5

MLA_HEADS = 8
MLA_NOPE = 64
MLA_ROPE = 32
MLA_V = 64
MLA_QK = MLA_NOPE + MLA_ROPE
MLA_Q_RANK = 256
MLA_KV_RANK = 128
MLA_WIDTH = MLA_HEADS * MLA_V
ROPE_THETA = 10000.0
ATTN_CHUNK = 64

RWKV_HEADS = 8
RWKV_HEAD = 64
RWKV_WIDTH = RWKV_HEADS * RWKV_HEAD
DECAY_LORA = 64
ICLR_LORA = 64
RWKV_SHIFT_WIDTH = 3 * RWKV_WIDTH + DECAY_LORA + ICLR_LORA
WKV_CHUNK = 64

DEPTH = 1
ALPHA = (2.0 * DEPTH) ** 0.25

LANES = 128
HEAD_PAD = 128
V_ROWS = 80
KEY_TILE = 256
PACK_HEADS = 2
PACK_LANES = PACK_HEADS * RWKV_HEAD

QKR_WIDTH = 512
VMEM_LIMIT = 56 * 1024 * 1024

NEG_BIG = -1e30


def _const_spec(shape):
    n = len(shape)
    return pl.BlockSpec(shape, lambda *_: (0,) * n)


def _params(*sem):
    return pltpu.CompilerParams(dimension_semantics=sem, vmem_limit_bytes=VMEM_LIMIT)


def _adaln_kernel(c_ref, w_ref, b_ref, o_ref):
    c = c_ref[...]
    sc = c * jax.nn.sigmoid(c)
    o_ref[...] = jnp.dot(sc.astype(BF16), w_ref[...], preferred_element_type=F32) + b_ref[...]


def _adaln(c, w_ada, b_ada):
    b = c.shape[0]
    return pl.pallas_call(
        _adaln_kernel,
        out_shape=jax.ShapeDtypeStruct((b, 3 * D_MODEL), F32),
        compiler_params=pltpu.CompilerParams(vmem_limit_bytes=VMEM_LIMIT),
    )(c, w_ada.astype(BF16), b_ada.reshape(1, -1))


_PROJ_WIDTHS = (QKR_WIDTH, MLA_WIDTH, RWKV_SHIFT_WIDTH, RWKV_WIDTH, D_MODEL, D_MODEL)


def _ln_proj_kernel(x_ref, scale_ref, shift_ref, w_ref, *out_refs):
    x = x_ref[0]
    xc = x - jnp.mean(x, -1, keepdims=True)
    h = xc * lax.rsqrt(jnp.mean(xc * xc, -1, keepdims=True) + LN_EPS)
    hb = (h * (1.0 + scale_ref[0]) + shift_ref[0]).astype(BF16)
    off = 0
    for o_ref, width in zip(out_refs, _PROJ_WIDTHS):
        o_ref[0] = jnp.dot(hb, w_ref[:, off:off + width], preferred_element_type=F32).astype(o_ref.dtype)
        off += width


def _ln_proj(x, scale, shift, w_in_pad, tm):
    b, s, d = x.shape
    tok = lambda w: pl.BlockSpec((1, tm, w), lambda bi, i: (bi, i, 0))
    vec = pl.BlockSpec((1, 1, d), lambda bi, i: (bi, 0, 0))
    return pl.pallas_call(
        _ln_proj_kernel,
        grid=(b, s // tm),
        in_specs=[tok(d), vec, vec, _const_spec(w_in_pad.shape)],
        out_specs=[tok(w) for w in _PROJ_WIDTHS],
        out_shape=[jax.ShapeDtypeStruct((b, s, w), BF16) for w in _PROJ_WIDTHS],
        compiler_params=_params("parallel", "parallel"),
    )(x, scale, shift, w_in_pad)


def _mla_prep_kernel(qkr_ref, pos_ref, inv_ref, qg_ref, kvg_ref, wqp_ref, wqr_ref, wk_ref, wvt_ref,
                     qt_out, k_out, vt_out):
    tm = qkr_ref.shape[1]
    half = MLA_ROPE // 2
    t = qkr_ref[0].astype(F32)
    qc = t[:, :MLA_Q_RANK]
    kvc = t[:, MLA_Q_RANK:MLA_Q_RANK + MLA_KV_RANK]
    kr = t[:, MLA_Q_RANK + MLA_KV_RANK:]
    qn = qc * lax.rsqrt(jnp.mean(qc * qc, -1, keepdims=True) + RMS_EPS) * qg_ref[...]
    kvn = kvc * lax.rsqrt(jnp.mean(kvc * kvc, -1, keepdims=True) + RMS_EPS) * kvg_ref[...]
    qn_t = qn.T.astype(BF16)
    kvn_t = kvn.T.astype(BF16)
    kr_t = kr.T

    ang = inv_ref[...] * pos_ref[0]
    cos_h = jnp.cos(ang)
    sin_h = jnp.sin(ang)
    cos_r = jnp.concatenate([cos_h, cos_h], axis=0)
    sin_r = jnp.concatenate([sin_h, sin_h], axis=0)

    plain_t = jnp.dot(wqp_ref[...], qn_t, preferred_element_type=F32)
    rot_t = jnp.dot(wqr_ref[...], qn_t, preferred_element_type=F32)
    scale = MLA_QK ** -0.5 * math.log2(math.e)
    zpad = jnp.zeros((HEAD_PAD - MLA_QK, tm), BF16)
    for h in range(MLA_HEADS):
        base = h * HEAD_PAD
        nope = (plain_t[base:base + MLA_NOPE] * scale).astype(BF16)
        pe = ((plain_t[base + MLA_NOPE:base + MLA_QK] * cos_r
               + rot_t[h * MLA_ROPE:(h + 1) * MLA_ROPE] * sin_r) * scale).astype(BF16)
        q_t = jnp.concatenate([nope, pe, zpad], axis=0)
        for kt in range(tm // KEY_TILE):
            qt_out[0, kt, base:base + HEAD_PAD, :] = q_t[:, kt * KEY_TILE:(kt + 1) * KEY_TILE]

    k1 = kr_t[:half]
    k2 = kr_t[half:MLA_ROPE]
    kpe_t = jnp.concatenate([jnp.zeros((MLA_NOPE, tm), F32),
                             k1 * cos_h - k2 * sin_h, k1 * sin_h + k2 * cos_h,
                             jnp.zeros((HEAD_PAD - MLA_QK, tm), F32)], axis=0)
    kpe = kpe_t.T
    ka = jnp.dot(kvn.astype(BF16), wk_ref[...], preferred_element_type=F32)
    for h in range(MLA_HEADS):
        sl = slice(h * HEAD_PAD, (h + 1) * HEAD_PAD)
        k_out[0, :, sl] = (ka[:, sl] + kpe).astype(BF16)

    vt = jnp.dot(wvt_ref[...], kvn_t, preferred_element_type=F32)
    vrow = lax.broadcasted_iota(jnp.int32, (MLA_HEADS * V_ROWS, 1), 0) % V_ROWS
    vt = (vt + jnp.where(vrow == MLA_V, 1.0, 0.0)).astype(BF16)
    for kt in range(tm // KEY_TILE):
        vt_out[0, kt] = vt[:, kt * KEY_TILE:(kt + 1) * KEY_TILE]


def _mla_weights(w_uq, w_ukv):
    half = MLA_ROPE // 2
    wq = w_uq.reshape(MLA_Q_RANK, MLA_HEADS, MLA_QK)
    zq = jnp.zeros((MLA_Q_RANK, MLA_HEADS, HEAD_PAD - MLA_QK), F32)
    wq_plain_t = jnp.concatenate([wq, zq], -1).reshape(MLA_Q_RANK, -1).T.astype(BF16)
    t1 = wq[:, :, MLA_NOPE:MLA_NOPE + half]
    t2 = wq[:, :, MLA_NOPE + half:]
    wq_rot_t = jnp.concatenate([-t2, t1], -1).reshape(MLA_Q_RANK, -1).T.astype(BF16)

    wkv = w_ukv.reshape(MLA_KV_RANK, MLA_HEADS, MLA_NOPE + MLA_V)
    zk = jnp.zeros((MLA_KV_RANK, MLA_HEADS, HEAD_PAD - MLA_NOPE), F32)
    wk = jnp.concatenate([wkv[:, :, :MLA_NOPE], zk], -1).reshape(MLA_KV_RANK, -1).astype(BF16)
    zv = jnp.zeros((MLA_KV_RANK, MLA_HEADS, V_ROWS - MLA_V), F32)
    wvt = jnp.concatenate([wkv[:, :, MLA_NOPE:], zv], -1).reshape(MLA_KV_RANK, -1).T.astype(BF16)
    return wq_plain_t, wq_rot_t, wk, wvt


def _mla_prep(qkr, pos, q_norm_g, kv_norm_g, w_uq, w_ukv, tm):
    b, s, _ = qkr.shape
    wqp, wqr, wk, wvt = _mla_weights(w_uq, w_ukv)
    half = MLA_ROPE // 2
    inv = (ROPE_THETA ** (-jnp.arange(0, MLA_ROPE, 2, dtype=F32) / MLA_ROPE)).reshape(half, 1)
    width = MLA_HEADS * HEAD_PAD
    vrows = MLA_HEADS * V_ROWS
    tok = lambda w: pl.BlockSpec((1, tm, w), lambda bi, i: (bi, i, 0))
    return pl.pallas_call(
        _mla_prep_kernel,
        grid=(b, s // tm),
        in_specs=[tok(QKR_WIDTH), pl.BlockSpec((1, 1, tm), lambda bi, i: (bi, 0, i)),
                  _const_spec((half, 1)), _const_spec((1, MLA_Q_RANK)),
                  _const_spec((1, MLA_KV_RANK)), _const_spec(wqp.shape), _const_spec(wqr.shape),
                  _const_spec(wk.shape), _const_spec(wvt.shape)],
        out_specs=[pl.BlockSpec((1, tm // KEY_TILE, width, KEY_TILE), lambda bi, i: (bi, i, 0, 0)),
                   tok(width),
                   pl.BlockSpec((1, tm // KEY_TILE, vrows, KEY_TILE), lambda bi, i: (bi, i, 0, 0))],
        out_shape=[jax.ShapeDtypeStruct((b, s // KEY_TILE, width, KEY_TILE), BF16),
                   jax.ShapeDtypeStruct((b, s, width), BF16),
                   jax.ShapeDtypeStruct((b, s // KEY_TILE, vrows, KEY_TILE), BF16)],
        compiler_params=_params("parallel", "parallel"),
    )(qkr, pos, inv, q_norm_g.reshape(1, -1), kv_norm_g.reshape(1, -1), wqp, wqr, wk, wvt)


ATTN_SUBTILES = 2
ATTN_PAIRS_PER_ITER = 2


def _attn_kernel(qt_ref, k_ref, vt_ref, o_ref, s_buf):
    ts = KEY_TILE
    nsub = ATTN_SUBTILES
    n_qt = o_ref.shape[1] // (nsub * ts)
    key_chunk = lax.broadcasted_iota(jnp.int32, (ts, ts), 0) // ATTN_CHUNK
    qry_chunk = lax.broadcasted_iota(jnp.int32, (ts, ts), 1) // ATTN_CHUNK
    diag_mask = key_chunk <= qry_chunk
    lanes = [slice(hh * HEAD_PAD, (hh + 1) * HEAD_PAD) for hh in range(2)]
    chains = [(hh, sb) for hh in range(2) for sb in range(nsub)]

    def scores(i, j, slot, first_sub):
        if isinstance(j, int):
            start = j * ts
        else:
            start = pl.multiple_of(j * ts, ts)
        kj = [k_ref[0, pl.ds(start, ts), lanes[hh]] for hh in range(2)]
        for n, ch in enumerate(chains):
            if ch[1] >= first_sub:
                q_t = qt_ref[0, i * nsub + ch[1], lanes[ch[0]], :]
                s_buf[slot, n] = jnp.dot(kj[ch[0]], q_t, preferred_element_type=F32)

    def consume(j, slot, carry, masked_sub, first_sub):
        vj = [vt_ref[0, j, hh * V_ROWS:(hh + 1) * V_ROWS, :] for hh in range(2)]
        active = [(n, ch) for n, ch in enumerate(chains) if ch[1] >= first_sub]
        s = {ch: s_buf[slot, n] for n, ch in active}
        for _, ch in active:
            if ch[1] == masked_sub:
                s[ch] = jnp.where(diag_mask, s[ch], NEG_BIG)
        m_new = {ch: jnp.maximum(carry[ch][0], jnp.max(s[ch], axis=0, keepdims=True)) for _, ch in active}
        p = {ch: jnp.exp2(s[ch] - m_new[ch]).astype(BF16) for _, ch in active}
        alpha = {ch: jnp.exp2(carry[ch][0] - m_new[ch]) for _, ch in active}
        out = dict(carry)
        for _, ch in active:
            pv = jnp.dot(vj[ch[0]], p[ch], preferred_element_type=F32)
            out[ch] = (m_new[ch], carry[ch][1] * alpha[ch] + pv)
        return out

    init = tuple(v for _ in chains
                 for v in (jnp.full((1, ts), NEG_BIG, F32), jnp.zeros((V_ROWS, ts), F32)))
    scores(0, 0, 0, 0)
    for i in range(n_qt):
        sa, sb_ = (0, 1) if i % 2 == 0 else (2, 3)
        next_sa = 2 if i % 2 == 0 else 0

        def pair_step(t, carry, i=i, sa=sa, sb_=sb_):
            scores(i, 2 * t + 1, sb_, 0)
            carry = consume(2 * t, sa, carry, -1, 0)
            scores(i, 2 * t + 2, sa, 0)
            return consume(2 * t + 1, sb_, carry, -1, 0)

        def loop_body(t, flat):
            carry = {ch: (flat[2 * n], flat[2 * n + 1]) for n, ch in enumerate(chains)}
            for u in range(ATTN_PAIRS_PER_ITER):
                carry = pair_step(ATTN_PAIRS_PER_ITER * t + u, carry)
            return tuple(v for ch in chains for v in carry[ch])

        n_iter, n_rest = divmod(i, ATTN_PAIRS_PER_ITER)
        flat = lax.fori_loop(0, n_iter, loop_body, init)
        carry = {ch: (flat[2 * n], flat[2 * n + 1]) for n, ch in enumerate(chains)}
        for u in range(n_rest):
            carry = pair_step(ATTN_PAIRS_PER_ITER * n_iter + u, carry)
        scores(i, 2 * i + 1, sb_, 1)
        if i + 1 < n_qt:
            scores(i + 1, 0, next_sa, 0)
        carry = consume(2 * i, sa, carry, 0, 0)
        carry = consume(2 * i + 1, sb_, carry, 1, 1)

        for sb in range(nsub):
            normed = []
            for hh in range(2):
                acc = carry[(hh, sb)][1]
                normed.append(acc[:MLA_V] / acc[MLA_V:MLA_V + 1])
            out_t = jnp.concatenate(normed, axis=0)
            row0 = (i * nsub + sb) * ts
            o_ref[0, row0:row0 + ts, :] = out_t.T.astype(o_ref.dtype)


def _attention(qt, k, vt):
    b, s, _ = k.shape
    pairs = MLA_HEADS // 2
    nsub = ATTN_SUBTILES
    return pl.pallas_call(
        _attn_kernel,
        grid=(b, pairs),
        in_specs=[pl.BlockSpec((1, s // KEY_TILE, 2 * HEAD_PAD, KEY_TILE), lambda bi, g: (bi, 0, g, 0)),
                  pl.BlockSpec((1, s, 2 * HEAD_PAD), lambda bi, g: (bi, 0, g)),
                  pl.BlockSpec((1, s // KEY_TILE, 2 * V_ROWS, KEY_TILE), lambda bi, g: (bi, 0, g, 0))],
        out_specs=pl.BlockSpec((1, s, 2 * MLA_V), lambda bi, g: (bi, 0, g)),
        out_shape=jax.ShapeDtypeStruct((b, s, MLA_WIDTH), BF16),
        scratch_shapes=[pltpu.VMEM((4, 2 * nsub, KEY_TILE, KEY_TILE), F32)],
        compiler_params=_params("parallel", "parallel"),
    )(qt, k, vt)


def _lane_head(width):
    return lax.broadcasted_iota(jnp.int32, (1, width), 1) // RWKV_HEAD


def _bd_stack(x):
    head = _lane_head(x.shape[1])
    xb = x.astype(BF16)
    return jnp.concatenate([jnp.where(head == h, xb, jnp.zeros_like(xb)) for h in range(PACK_HEADS)], axis=0)


def _pdot(a, b):
    return jnp.dot(a.astype(BF16), _bd_stack(b), preferred_element_type=F32)


def _pdot2(a, b1, b2):
    rhs = jnp.concatenate([_bd_stack(b1), _bd_stack(b2)], axis=1)
    out = jnp.dot(a.astype(BF16), rhs, preferred_element_type=F32)
    return out[:, :PACK_LANES], out[:, PACK_LANES:]


def _pdot_nt(a, b):
    return lax.dot_general(a.astype(BF16), _bd_stack(b), (((1,), (1,)), ((), ())),
                           preferred_element_type=F32)


def _split3(x):
    x1 = x.astype(BF16)
    r1 = x - x1.astype(F32)
    x2 = r1.astype(BF16)
    x3 = (r1 - x2.astype(F32)).astype(BF16)
    return x1, x2, x3


def _wkv_kernel(rw_ref, prev_ref, mu_ref, wlora_ref, w0_ref, a0_ref, kk_ref, ka_ref, rk_ref,
                ones_ref, tri_ref, y_ref, bonus_ref, s_ref, *, tm):
    i = pl.program_id(1)

    @pl.when(i == 0)
    def _():
        s_ref[...] = jnp.zeros_like(s_ref)

    c = WKV_CHUNK
    w = RWKV_WIDTH
    u_raw = rw_ref[0].astype(F32)
    prev_row = prev_ref[0, 7:8, :].astype(F32) * (i > 0).astype(F32)
    row = lax.broadcasted_iota(jnp.int32, (tm, 1), 0)
    u_prev = jnp.where(row == 0, prev_row, pltpu.roll(u_raw, 1, axis=0))
    u = u_raw + (u_prev - u_raw) * mu_ref[...]
    r = u[:, :w]
    k = u[:, w:2 * w]
    v = u[:, 2 * w:3 * w]
    lora_in = u[:, 3 * w:]
    lane = lax.broadcasted_iota(jnp.int32, (1, DECAY_LORA + ICLR_LORA), 1)
    lora_in = jnp.where(lane < DECAY_LORA, jnp.tanh(lora_in), lora_in)
    lora = jnp.dot(lora_in.astype(BF16), wlora_ref[...], preferred_element_type=F32)
    lw = -math.exp(-0.5) * jax.nn.sigmoid(w0_ref[...] + lora[:, :w])
    a = jax.nn.sigmoid(a0_ref[...] + lora[:, w:])

    ones_bd = ones_ref[...]
    headsum = lambda t: jnp.dot(t.astype(BF16), ones_bd, preferred_element_type=F32)
    kk = k * kk_ref[...]
    kk = kk * lax.rsqrt(jnp.maximum(headsum(kk * kk), 1e-24))
    k = k * (1.0 + (a - 1.0) * ka_ref[...])
    bonus_ref[0] = (headsum(r * k * rk_ref[...]) * v).astype(bonus_ref.dtype)
    a_vec = -kk
    b_vec = kk * a

    tri = tri_ref[...]
    cum = sum(jnp.dot(tri, part, preferred_element_type=F32) for part in _split3(lw))

    t_idx = lax.broadcasted_iota(jnp.int32, (c, PACK_LANES), 0)
    s_idx = lax.broadcasted_iota(jnp.int32, (c, PACK_LANES), 1) % RWKV_HEAD
    strict = s_idx < t_idx
    incl = s_idx <= t_idx
    eye = (s_idx == t_idx).astype(F32)
    prow = lax.broadcasted_iota(jnp.int32, (PACK_LANES, PACK_LANES), 0)
    pcol = lax.broadcasted_iota(jnp.int32, (PACK_LANES, PACK_LANES), 1)
    same_head = (prow // RWKV_HEAD) == (pcol // RWKV_HEAD)
    on_diag = prow == pcol
    head = _lane_head(PACK_LANES)

    nch = tm // c
    cum_last = jnp.concatenate(
        [jnp.broadcast_to(cum[(cc + 1) * c - 1:(cc + 1) * c, :], (c, w)) for cc in range(nch)], axis=0)
    e_pos = jnp.exp(cum)
    e_neg = jnp.exp(-cum)
    e_last = jnp.exp(cum_last - cum)
    at_all = a_vec * jnp.exp(cum - lw)
    bt_all = b_vec * e_neg
    kt_all = k * e_neg
    rt_all = r * e_pos
    bl_all = b_vec * e_last
    kl_all = k * e_last
    w_end_all = jnp.exp(cum_last)

    probs = [(cc, g) for cc in range(nch) for g in range(w // PACK_LANES)]
    sl = {pr: (slice(pr[0] * c, (pr[0] + 1) * c), slice(pr[1] * PACK_LANES, (pr[1] + 1) * PACK_LANES))
          for pr in probs}
    at = {pr: at_all[sl[pr]] for pr in probs}
    rt = {pr: rt_all[sl[pr]] for pr in probs}
    vv = {pr: v[sl[pr]] for pr in probs}
    m_bk = {pr: lax.dot_general(
        jnp.concatenate([at[pr], rt[pr]], axis=0).astype(BF16),
        jnp.concatenate([_bd_stack(bt_all[sl[pr]]), _bd_stack(kt_all[sl[pr]])], axis=0),
        (((1,), (1,)), ((), ())), preferred_element_type=F32) for pr in probs}
    m_b = {pr: m_bk[pr][:, :PACK_LANES] for pr in probs}
    m_k = {pr: m_bk[pr][:, PACK_LANES:] for pr in probs}
    l_ab = {pr: jnp.where(strict, m_b[pr][:c], 0.0) for pr in probs}
    t_inv = {pr: eye + l_ab[pr] for pr in probs}
    x = {pr: _pdot(l_ab[pr], l_ab[pr]) for pr in probs}
    levels = int(math.log2(c)) - 1
    for lvl in range(levels):
        if lvl < levels - 1:
            tx = {pr: _pdot2(x[pr], t_inv[pr], x[pr]) for pr in probs}
            t_inv = {pr: t_inv[pr] + tx[pr][0] for pr in probs}
            x = {pr: tx[pr][1] for pr in probs}
        else:
            t_inv = {pr: t_inv[pr] + _pdot(x[pr], t_inv[pr]) for pr in probs}
    akv = {pr: _pdot(jnp.where(strict, m_k[pr][:c], 0.0), vv[pr]) for pr in probs}
    au = {pr: _pdot2(t_inv[pr], at[pr], akv[pr]) for pr in probs}
    a_hat = {pr: au[pr][0] for pr in probs}
    u0 = {pr: au[pr][1] for pr in probs}
    r_hat, y0, p_bd, q_pack = {}, {}, {}, {}
    for pr in probs:
        m_rb = jnp.where(incl, m_b[pr][c:], 0.0)
        m_rk = jnp.where(incl, m_k[pr][c:], 0.0)
        r_add, y_add = _pdot2(m_rb, a_hat[pr], u0[pr])
        r_hat[pr] = (rt[pr] + r_add).astype(BF16)
        y0[pr] = y_add + _pdot(m_rk, vv[pr])
    for pr in probs:
        rs, ls = sl[pr]
        bl = bl_all[sl[pr]]
        p_full = jnp.dot(a_hat[pr].T.astype(BF16), bl.astype(BF16), preferred_element_type=F32)
        w_end = w_end_all[rs.start:rs.start + 1, ls]
        p_bd[pr] = (jnp.where(same_head, p_full, 0.0) + jnp.where(on_diag, w_end, 0.0)).astype(BF16)
        uv_t = jnp.concatenate([u0[pr], vv[pr]], axis=0).T
        bk = jnp.concatenate([bl, kl_all[sl[pr]]], axis=0).astype(BF16)
        f = jnp.dot(uv_t.astype(BF16), bk, preferred_element_type=F32)
        q_pack[pr] = sum(jnp.where(head == h, f[h * RWKV_HEAD:(h + 1) * RWKV_HEAD], 0.0)
                         for h in range(PACK_HEADS))

    packs = range(w // PACK_LANES)
    state = {g: s_ref[:, g * PACK_LANES:(g + 1) * PACK_LANES] for g in packs}
    for cc in range(nch):
        for g in packs:
            pr = (cc, g)
            rs, ls = sl[pr]
            y_ref[0, rs, ls] = (_pdot_nt(r_hat[pr], state[g]) + y0[pr]).astype(y_ref.dtype)
        state = {g: jnp.dot(state[g].astype(BF16), p_bd[(cc, g)], preferred_element_type=F32)
                 + q_pack[(cc, g)] for g in packs}
    for g in packs:
        s_ref[:, g * PACK_LANES:(g + 1) * PACK_LANES] = state[g]


def _wkv(rw, mu, w0, w_decay_up, a0, w_iclr_up, k_k, k_a, r_k, tm):
    b, s, width = rw.shape
    w = RWKV_WIDTH
    wlora = jnp.concatenate(
        [jnp.concatenate([w_decay_up, jnp.zeros((DECAY_LORA, w), F32)], 1),
         jnp.concatenate([jnp.zeros((ICLR_LORA, w), F32), w_iclr_up], 1)], 0).astype(BF16)
    hid = np.arange(w) // RWKV_HEAD
    ones_bd = jnp.asarray((hid[:, None] == hid[None, :]).astype(np.float32), BF16)
    tid = np.arange(tm)
    tri = jnp.asarray(((tid[:, None] >= tid[None, :]) &
                       (tid[:, None] // WKV_CHUNK == tid[None, :] // WKV_CHUNK)).astype(np.float32), BF16)
    row = lambda t: t.reshape(1, -1)
    tok = lambda wd: pl.BlockSpec((1, tm, wd), lambda bi, i: (bi, i, 0))
    prev = pl.BlockSpec((1, 8, width), lambda bi, i: (bi, jnp.maximum(i * (tm // 8) - 1, 0), 0))
    return pl.pallas_call(
        functools.partial(_wkv_kernel, tm=tm),
        grid=(b, s // tm),
        in_specs=[tok(width), prev, _const_spec((1, width)), _const_spec(wlora.shape)]
                 + [_const_spec((1, w))] * 5 + [_const_spec(ones_bd.shape), _const_spec(tri.shape)],
        out_specs=[tok(w), tok(w)],
        out_shape=[jax.ShapeDtypeStruct((b, s, w), BF16),
                   jax.ShapeDtypeStruct((b, s, w), BF16)],
        scratch_shapes=[pltpu.VMEM((RWKV_HEAD, w), F32)],
        compiler_params=_params("parallel", "arbitrary"),
    )(rw, rw, row(mu), wlora, row(w0), row(a0), row(k_k), row(k_a), row(r_k), ones_bd, tri)


def _epilogue_kernel(x_ref, gate_ref, attn_ref, ga_ref, y_ref, bonus_ref, gb_ref, ma_ref, mb_ref,
                     ones_ref, gng_ref, gnb_ref, wpa_ref, wpb_ref, wout_ref, pg_ref, pb_ref, o_ref):
    silu = lambda t: t * jax.nn.sigmoid(t)
    ones_bd = ones_ref[...]
    headmean = lambda t: jnp.dot(t.astype(BF16), ones_bd, preferred_element_type=F32) * (1.0 / RWKV_HEAD)

    y = y_ref[0].astype(F32)
    yc = y - headmean(y)
    yn = yc * lax.rsqrt(headmean(yc * yc) + GN_EPS)
    yb = yn * gng_ref[...] + gnb_ref[...] + bonus_ref[0].astype(F32)
    yb = yb * silu(gb_ref[0].astype(F32))
    ya = attn_ref[0].astype(F32) * silu(ga_ref[0].astype(F32))
    ya_p = jnp.dot(ya.astype(BF16), wpa_ref[...], preferred_element_type=F32)
    yb_p = jnp.dot(yb.astype(BF16), wpb_ref[...], preferred_element_type=F32)
    merged = (jax.nn.sigmoid(ma_ref[0].astype(F32)) * ya_p
              + jax.nn.sigmoid(mb_ref[0].astype(F32)) * yb_p)
    sub = jnp.dot(merged.astype(BF16), wout_ref[...], preferred_element_type=F32)
    z = ALPHA * x_ref[0] + (1.0 + gate_ref[0]) * sub
    zc = z - jnp.mean(z, -1, keepdims=True)
    zn = zc * lax.rsqrt(jnp.mean(zc * zc, -1, keepdims=True) + LN_EPS)
    o_ref[0] = (zn * pg_ref[...] + pb_ref[...]).astype(o_ref.dtype)


def _epilogue(x, gate, attn, ga, y, bonus, gb, ma, mb, gn_g, gn_b, w_proj_a, w_proj_b, w_out,
              post_g, post_b, tm):
    b, s, d = x.shape
    w = RWKV_WIDTH
    hid = np.arange(w) // RWKV_HEAD
    ones_bd = jnp.asarray((hid[:, None] == hid[None, :]).astype(np.float32), BF16)
    row = lambda t: t.reshape(1, -1)
    tok = lambda wd: pl.BlockSpec((1, tm, wd), lambda bi, i: (bi, i, 0))
    vec = pl.BlockSpec((1, 1, d), lambda bi, i: (bi, 0, 0))
    return pl.pallas_call(
        _epilogue_kernel,
        grid=(b, s // tm),
        in_specs=[tok(d), vec, tok(w), tok(w), tok(w), tok(w), tok(w), tok(d), tok(d),
                  _const_spec((w, w)), _const_spec((1, w)), _const_spec((1, w)),
                  _const_spec((w, d)), _const_spec((w, d)), _const_spec((d, d)),
                  _const_spec((1, d)), _const_spec((1, d))],
        out_specs=tok(d),
        out_shape=jax.ShapeDtypeStruct((b, s, d), x.dtype),
        compiler_params=_params("parallel", "parallel"),
    )(x, gate, attn, ga, y, bonus, gb, ma, mb, ones_bd, row(gn_g), row(gn_b),
      w_proj_a.astype(BF16), w_proj_b.astype(BF16), w_out.astype(BF16), row(post_g), row(post_b))


def _pad_w_in(w_in):
    used = MLA_Q_RANK + MLA_KV_RANK + MLA_ROPE
    pad = jnp.zeros((w_in.shape[0], QKR_WIDTH - used), w_in.dtype)
    return jnp.concatenate([w_in[:, :used], pad, w_in[:, used:]], axis=1).astype(BF16)


def _layer(x, c, pos, w_ada, b_ada, w_in, q_norm_g, w_uq, kv_norm_g, w_ukv, mu_rwkv, w0, w_decay_up,
           a0, w_iclr_up, k_k, k_a, r_k, gn_g, gn_b, w_proj_a, w_proj_b, w_out, post_g, post_b):
    b, s, d = x.shape
    tm = min(512, s)
    ada = _adaln(c, w_ada, b_ada)
    shift, scale, gate = (ada[:, j * d:(j + 1) * d].reshape(b, 1, d) for j in range(3))
    qkr, ga, rw, gb, ma, mb = _ln_proj(x, scale, shift, _pad_w_in(w_in), tm)
    qt, k, vt = _mla_prep(qkr, pos, q_norm_g, kv_norm_g, w_uq, w_ukv, tm)
    attn = _attention(qt, k, vt)
    y, bonus = _wkv(rw, mu_rwkv, w0, w_decay_up, a0, w_iclr_up, k_k, k_a, r_k.reshape(-1), min(256, s))
    return _epilogue(x, gate, attn, ga, y, bonus, gb, ma, mb, gn_g, gn_b, w_proj_a, w_proj_b, w_out,
                     post_g, post_b, tm)


def kernel(x, c, positions, w_ada, b_ada, w_in, q_norm_g, w_uq, kv_norm_g, w_ukv, mu_rwkv, w0,
           w_decay_up, a0, w_iclr_up, k_k, k_a, r_k, gn_g, gn_b, w_proj_a, w_proj_b, w_out, post_g,
           post_b):
    pos = positions.astype(F32)[:, None, :]
    for l in range(w_ada.shape[0]):
        x = _layer(x, c, pos, w_ada[l], b_ada[l], w_in[l], q_norm_g[l], w_uq[l], kv_norm_g[l],
                   w_ukv[l], mu_rwkv[l], w0[l], w_decay_up[l], a0[l], w_iclr_up[l], k_k[l], k_a[l],
                   r_k[l], gn_g[l], gn_b[l], w_proj_a[l], w_proj_b[l], w_out[l], post_g[l], post_b[l])
    return x
```

```python
import functools
import math

import jax
import jax.numpy as jnp
import numpy as np
from jax import lax
from jax.experimental import pallas as pl
from jax.experimental.pallas import tpu as pltpu

F32 = jnp.float32
BF16 = jnp.bfloat16

D_MODEL = 1024
LN_EPS = 1e-5
RMS_EPS = 1e-6
GN_EPS = 64e-5

MLA_HEADS = 8
MLA_NOPE = 64
MLA_ROPE = 32
MLA_V = 64
MLA_QK = MLA_NOPE + MLA_ROPE
MLA_Q_RANK = 256
MLA_KV_RANK = 128
MLA_WIDTH = MLA_HEADS * MLA_V
ROPE_THETA = 10000.0
ATTN_CHUNK = 64

RWKV_HEADS = 8
RWKV_HEAD = 64
RWKV_WIDTH = RWKV_HEADS * RWKV_HEAD
DECAY_LORA = 64
ICLR_LORA = 64
RWKV_SHIFT_WIDTH = 3 * RWKV_WIDTH + DECAY_LORA + ICLR_LORA
WKV_CHUNK = 64

DEPTH = 1
ALPHA = (2.0 * DEPTH) ** 0.25

LANES = 128
HEAD_PAD = 128
V_ROWS = 80
KEY_TILE = 256
PACK_HEADS = 2
PACK_LANES = PACK_HEADS * RWKV_HEAD

QKR_WIDTH = 512
VMEM_LIMIT = 56 * 1024 * 1024

NEG_BIG = -1e30


def _const_spec(shape):
    n = len(shape)
    return pl.BlockSpec(shape, lambda *_: (0,) * n)


def _params(*sem):
    return pltpu.CompilerParams(dimension_semantics=sem, vmem_limit_bytes=VMEM_LIMIT)


def _adaln_kernel(c_ref, w_ref, b_ref, o_ref):
    c = c_ref[...]
    sc = c * jax.nn.sigmoid(c)
    o_ref[...] = jnp.dot(sc.astype(BF16), w_ref[...], preferred_element_type=F32) + b_ref[...]


def _adaln(c, w_ada, b_ada):
    b = c.shape[0]
    return pl.pallas_call(
        _adaln_kernel,
        out_shape=jax.ShapeDtypeStruct((b, 3 * D_MODEL), F32),
        compiler_params=pltpu.CompilerParams(vmem_limit_bytes=VMEM_LIMIT),
    )(c, w_ada.astype(BF16), b_ada.reshape(1, -1))


_PROJ_WIDTHS = (MLA_WIDTH, RWKV_SHIFT_WIDTH, RWKV_WIDTH, D_MODEL, D_MODEL)
_PROJ_SCALES = (0.5, 1.0, 0.5, 0.5, 0.5)


def _ln_proj_kernel(x_ref, scale_ref, shift_ref, w_ref, pos_ref, inv_ref, qg_ref, kvg_ref, wqp_ref,
                    wqr_ref, wk_ref, wvt_ref, qt_out, k_out, vt_out, *out_refs):
    x = x_ref[0]
    xc = x - jnp.mean(x, -1, keepdims=True)
    h = xc * lax.rsqrt(jnp.mean(xc * xc, -1, keepdims=True) + LN_EPS)
    hb = (h * (1.0 + scale_ref[0]) + shift_ref[0]).astype(BF16)
    latent = jnp.dot(hb, w_ref[:, :QKR_WIDTH], preferred_element_type=F32)
    off = QKR_WIDTH
    for n, (o_ref, width, scale) in enumerate(zip(out_refs, _PROJ_WIDTHS, _PROJ_SCALES)):
        acc = jnp.dot(hb, w_ref[:, off:off + width], preferred_element_type=F32)
        if scale != 1.0:
            acc = acc * scale
        o_ref[0] = acc.astype(o_ref.dtype)
        off += width
        if n == 1:
            _mla_prep(latent, pos_ref[0], inv_ref[...], qg_ref[...], kvg_ref[...], wqp_ref, wqr_ref,
                      wk_ref, wvt_ref, qt_out, k_out, vt_out)


def _ln_proj(x, scale, shift, w_in_pad, pos, q_norm_g, kv_norm_g, w_uq, w_ukv, tm):
    b, s, d = x.shape
    wqp, wqr, wk, wvt = _mla_weights(w_uq, w_ukv)
    half = MLA_ROPE // 2
    inv = (ROPE_THETA ** (-jnp.arange(0, MLA_ROPE, 2, dtype=F32) / MLA_ROPE)).reshape(half, 1)
    width = MLA_HEADS * HEAD_PAD
    vrows = MLA_HEADS * V_ROWS
    tok = lambda w: pl.BlockSpec((1, tm, w), lambda bi, i: (bi, i, 0))
    vec = pl.BlockSpec((1, 1, d), lambda bi, i: (bi, 0, 0))
    tiles = lambda rows: pl.BlockSpec((1, tm // KEY_TILE, rows, KEY_TILE), lambda bi, i: (bi, i, 0, 0))
    return pl.pallas_call(
        _ln_proj_kernel,
        grid=(b, s // tm),
        in_specs=[tok(d), vec, vec, _const_spec(w_in_pad.shape),
                  pl.BlockSpec((1, 1, tm), lambda bi, i: (bi, 0, i)),
                  _const_spec((half, 1)), _const_spec((1, MLA_Q_RANK)), _const_spec((1, MLA_KV_RANK)),
                  _const_spec(wqp.shape), _const_spec(wqr.shape), _const_spec(wk.shape),
                  _const_spec(wvt.shape)],
        out_specs=[tiles(width), tok(width), tiles(vrows)] + [tok(w) for w in _PROJ_WIDTHS],
        out_shape=[jax.ShapeDtypeStruct((b, s // KEY_TILE, width, KEY_TILE), BF16),
                   jax.ShapeDtypeStruct((b, s, width), BF16),
                   jax.ShapeDtypeStruct((b, s // KEY_TILE, vrows, KEY_TILE), BF16)]
                  + [jax.ShapeDtypeStruct((b, s, w), BF16) for w in _PROJ_WIDTHS],
        compiler_params=_params("parallel", "parallel"),
    )(x, scale, shift, w_in_pad, pos, inv, q_norm_g.reshape(1, -1), kv_norm_g.reshape(1, -1),
      wqp, wqr, wk, wvt)


def _mla_prep(t, pos, inv, qg, kvg, wqp_ref, wqr_ref, wk_ref, wvt_ref, qt_out, k_out, vt_out):
    tm = t.shape[0]
    half = MLA_ROPE // 2
    qc = t[:, :MLA_Q_RANK]
    kvc = t[:, MLA_Q_RANK:MLA_Q_RANK + MLA_KV_RANK]
    kr = t[:, MLA_Q_RANK + MLA_KV_RANK:]
    qn = qc * lax.rsqrt(jnp.mean(qc * qc, -1, keepdims=True) + RMS_EPS) * qg
    kvn = kvc * lax.rsqrt(jnp.mean(kvc * kvc, -1, keepdims=True) + RMS_EPS) * kvg
    qn_t = qn.T.astype(BF16)
    kvn_t = kvn.T.astype(BF16)
    kr_t = kr.T

    ang = inv * pos
    cos_h = jnp.cos(ang)
    sin_h = jnp.sin(ang)
    cos_r = jnp.concatenate([cos_h, cos_h], axis=0)
    sin_r = jnp.concatenate([sin_h, sin_h], axis=0)

    plain_t = jnp.dot(wqp_ref[...], qn_t, preferred_element_type=F32)
    rot_t = jnp.dot(wqr_ref[...], qn_t, preferred_element_type=F32)
    scale = MLA_QK ** -0.5 * math.log2(math.e)
    zpad = jnp.zeros((HEAD_PAD - MLA_QK, tm), BF16)
    for h in range(MLA_HEADS):
        base = h * HEAD_PAD
        nope = (plain_t[base:base + MLA_NOPE] * scale).astype(BF16)
        pe = ((plain_t[base + MLA_NOPE:base + MLA_QK] * cos_r
               + rot_t[h * MLA_ROPE:(h + 1) * MLA_ROPE] * sin_r) * scale).astype(BF16)
        q_t = jnp.concatenate([nope, pe, zpad], axis=0)
        for kt in range(tm // KEY_TILE):
            qt_out[0, kt, base:base + HEAD_PAD, :] = q_t[:, kt * KEY_TILE:(kt + 1) * KEY_TILE]

    k1 = kr_t[:half]
    k2 = kr_t[half:MLA_ROPE]
    kpe_t = jnp.concatenate([jnp.zeros((MLA_NOPE, tm), F32),
                             k1 * cos_h - k2 * sin_h, k1 * sin_h + k2 * cos_h,
                             jnp.zeros((HEAD_PAD - MLA_QK, tm), F32)], axis=0)
    kpe = kpe_t.T
    ka = jnp.dot(kvn.astype(BF16), wk_ref[...], preferred_element_type=F32)
    for h in range(MLA_HEADS):
        sl = slice(h * HEAD_PAD, (h + 1) * HEAD_PAD)
        k_out[0, :, sl] = (ka[:, sl] + kpe).astype(BF16)

    vt = jnp.dot(wvt_ref[...], kvn_t, preferred_element_type=F32)
    vrow = lax.broadcasted_iota(jnp.int32, (MLA_HEADS * V_ROWS, 1), 0) % V_ROWS
    vt = (vt + jnp.where(vrow == MLA_V, 1.0, 0.0)).astype(BF16)
    for kt in range(tm // KEY_TILE):
        vt_out[0, kt] = vt[:, kt * KEY_TILE:(kt + 1) * KEY_TILE]


def _mla_weights(w_uq, w_ukv):
    half = MLA_ROPE // 2
    wq = w_uq.reshape(MLA_Q_RANK, MLA_HEADS, MLA_QK)
    zq = jnp.zeros((MLA_Q_RANK, MLA_HEADS, HEAD_PAD - MLA_QK), F32)
    wq_plain_t = jnp.concatenate([wq, zq], -1).reshape(MLA_Q_RANK, -1).T.astype(BF16)
    t1 = wq[:, :, MLA_NOPE:MLA_NOPE + half]
    t2 = wq[:, :, MLA_NOPE + half:]
    wq_rot_t = jnp.concatenate([-t2, t1], -1).reshape(MLA_Q_RANK, -1).T.astype(BF16)

    wkv = w_ukv.reshape(MLA_KV_RANK, MLA_HEADS, MLA_NOPE + MLA_V)
    zk = jnp.zeros((MLA_KV_RANK, MLA_HEADS, HEAD_PAD - MLA_NOPE), F32)
    wk = jnp.concatenate([wkv[:, :, :MLA_NOPE], zk], -1).reshape(MLA_KV_RANK, -1).astype(BF16)
    zv = jnp.zeros((MLA_KV_RANK, MLA_HEADS, V_ROWS - MLA_V), F32)
    wvt = jnp.concatenate([wkv[:, :, MLA_NOPE:], zv], -1).reshape(MLA_KV_RANK, -1).T.astype(BF16)
    return wq_plain_t, wq_rot_t, wk, wvt


ATTN_SUBTILES = 2
ATTN_PAIRS_PER_ITER = 2


def _attn_kernel(qt_ref, k_ref, vt_ref, o_ref, s_buf):
    ts = KEY_TILE
    nsub = ATTN_SUBTILES
    n_qt = o_ref.shape[1] // (nsub * ts)
    key_chunk = lax.broadcasted_iota(jnp.int32, (ts, ts), 0) // ATTN_CHUNK
    qry_chunk = lax.broadcasted_iota(jnp.int32, (ts, ts), 1) // ATTN_CHUNK
    diag_mask = key_chunk <= qry_chunk
    lanes = [slice(hh * HEAD_PAD, (hh + 1) * HEAD_PAD) for hh in range(2)]
    chains = [(hh, sb) for hh in range(2) for sb in range(nsub)]

    def scores(i, j, slot, first_sub):
        if isinstance(j, int):
            start = j * ts
        else:
            start = pl.multiple_of(j * ts, ts)
        kj = [k_ref[0, pl.ds(start, ts), lanes[hh]] for hh in range(2)]
        for n, ch in enumerate(chains):
            if ch[1] >= first_sub:
                q_t = qt_ref[0, i * nsub + ch[1], lanes[ch[0]], :]
                s_buf[slot, n] = jnp.dot(kj[ch[0]], q_t, preferred_element_type=F32)

    def consume(j, slot, carry, masked_sub, first_sub):
        vj = [vt_ref[0, j, hh * V_ROWS:(hh + 1) * V_ROWS, :] for hh in range(2)]
        active = [(n, ch) for n, ch in enumerate(chains) if ch[1] >= first_sub]
        s = {ch: s_buf[slot, n] for n, ch in active}
        for _, ch in active:
            if ch[1] == masked_sub:
                s[ch] = jnp.where(diag_mask, s[ch], NEG_BIG)
        m_new = {ch: jnp.maximum(carry[ch][0], jnp.max(s[ch], axis=0, keepdims=True)) for _, ch in active}
        p = {ch: jnp.exp2(s[ch] - m_new[ch]).astype(BF16) for _, ch in active}
        alpha = {ch: jnp.exp2(carry[ch][0] - m_new[ch]) for _, ch in active}
        out = dict(carry)
        for _, ch in active:
            pv = jnp.dot(vj[ch[0]], p[ch], preferred_element_type=F32)
            out[ch] = (m_new[ch], carry[ch][1] * alpha[ch] + pv)
        return out

    init = tuple(v for _ in chains
                 for v in (jnp.full((1, ts), NEG_BIG, F32), jnp.zeros((V_ROWS, ts), F32)))
    scores(0, 0, 0, 0)
    for i in range(n_qt):
        sa, sb_ = (0, 1) if i % 2 == 0 else (2, 3)
        next_sa = 2 if i % 2 == 0 else 0

        def pair_step(t, carry, i=i, sa=sa, sb_=sb_):
            scores(i, 2 * t + 1, sb_, 0)
            carry = consume(2 * t, sa, carry, -1, 0)
            scores(i, 2 * t + 2, sa, 0)
            return consume(2 * t + 1, sb_, carry, -1, 0)

        def loop_body(t, flat):
            carry = {ch: (flat[2 * n], flat[2 * n + 1]) for n, ch in enumerate(chains)}
            for u in range(ATTN_PAIRS_PER_ITER):
                carry = pair_step(ATTN_PAIRS_PER_ITER * t + u, carry)
            return tuple(v for ch in chains for v in carry[ch])

        n_iter, n_rest = divmod(i, ATTN_PAIRS_PER_ITER)
        flat = lax.fori_loop(0, n_iter, loop_body, init)
        carry = {ch: (flat[2 * n], flat[2 * n + 1]) for n, ch in enumerate(chains)}
        for u in range(n_rest):
            carry = pair_step(ATTN_PAIRS_PER_ITER * n_iter + u, carry)
        scores(i, 2 * i + 1, sb_, 1)
        if i + 1 < n_qt:
            scores(i + 1, 0, next_sa, 0)
        carry = consume(2 * i, sa, carry, 0, 0)
        carry = consume(2 * i + 1, sb_, carry, 1, 1)

        for sb in range(nsub):
            normed = []
            for hh in range(2):
                acc = carry[(hh, sb)][1]
                normed.append(acc[:MLA_V] / acc[MLA_V:MLA_V + 1])
            out_t = jnp.concatenate(normed, axis=0)
            row0 = (i * nsub + sb) * ts
            o_ref[0, row0:row0 + ts, :] = out_t.T.astype(o_ref.dtype)


def _attention(qt, k, vt):
    b, s, _ = k.shape
    pairs = MLA_HEADS // 2
    nsub = ATTN_SUBTILES
    return pl.pallas_call(
        _attn_kernel,
        grid=(b, pairs),
        in_specs=[pl.BlockSpec((1, s // KEY_TILE, 2 * HEAD_PAD, KEY_TILE), lambda bi, g: (bi, 0, g, 0)),
                  pl.BlockSpec((1, s, 2 * HEAD_PAD), lambda bi, g: (bi, 0, g)),
                  pl.BlockSpec((1, s // KEY_TILE, 2 * V_ROWS, KEY_TILE), lambda bi, g: (bi, 0, g, 0))],
        out_specs=pl.BlockSpec((1, s, 2 * MLA_V), lambda bi, g: (bi, 0, g)),
        out_shape=jax.ShapeDtypeStruct((b, s, MLA_WIDTH), BF16),
        scratch_shapes=[pltpu.VMEM((4, 2 * nsub, KEY_TILE, KEY_TILE), F32)],
        compiler_params=_params("parallel", "parallel"),
    )(qt, k, vt)


def _lane_head(width):
    return lax.broadcasted_iota(jnp.int32, (1, width), 1) // RWKV_HEAD


def _bd_stack(x):
    head = _lane_head(x.shape[1])
    xb = x.astype(BF16)
    return jnp.concatenate([jnp.where(head == h, xb, jnp.zeros_like(xb)) for h in range(PACK_HEADS)], axis=0)


def _pdot(a, b):
    return jnp.dot(a.astype(BF16), _bd_stack(b), preferred_element_type=F32)


def _pdot2(a, b1, b2):
    rhs = jnp.concatenate([_bd_stack(b1), _bd_stack(b2)], axis=1)
    out = jnp.dot(a.astype(BF16), rhs, preferred_element_type=F32)
    return out[:, :PACK_LANES], out[:, PACK_LANES:]


def _pdot_nt(a, b):
    return lax.dot_general(a.astype(BF16), _bd_stack(b), (((1,), (1,)), ((), ())),
                           preferred_element_type=F32)


def _split3(x):
    x1 = x.astype(BF16)
    r1 = x - x1.astype(F32)
    x2 = r1.astype(BF16)
    x3 = (r1 - x2.astype(F32)).astype(BF16)
    return x1, x2, x3


def _wkv_kernel(rw_ref, prev_ref, mu_ref, wlora_ref, w0_ref, a0_ref, kk_ref, ka_ref, rk_ref,
                ones_ref, tri_ref, y_ref, bonus_ref, s_ref, *, tm):
    i = pl.program_id(1)

    @pl.when(i == 0)
    def _():
        s_ref[...] = jnp.zeros_like(s_ref)

    c = WKV_CHUNK
    w = RWKV_WIDTH
    u_raw = rw_ref[0].astype(F32)
    prev_row = prev_ref[0, 7:8, :].astype(F32) * (i > 0).astype(F32)
    row = lax.broadcasted_iota(jnp.int32, (tm, 1), 0)
    u_prev = jnp.where(row == 0, prev_row, pltpu.roll(u_raw, 1, axis=0))
    u = u_raw + (u_prev - u_raw) * mu_ref[...]
    r = u[:, :w]
    k = u[:, w:2 * w]
    v = u[:, 2 * w:3 * w]
    lora_in = u[:, 3 * w:]
    lane = lax.broadcasted_iota(jnp.int32, (1, DECAY_LORA + ICLR_LORA), 1)
    lora_in = jnp.where(lane < DECAY_LORA, jnp.tanh(lora_in), lora_in)
    lora = jnp.dot(lora_in.astype(BF16), wlora_ref[...], preferred_element_type=F32)
    c_lw = -0.5 * math.exp(-0.5)
    lw = c_lw + c_lw * jnp.tanh(w0_ref[...] + lora[:, :w])
    a = 0.5 + 0.5 * jnp.tanh(a0_ref[...] + lora[:, w:])

    ones_bd = ones_ref[...]
    headsum = lambda t: jnp.dot(t.astype(BF16), ones_bd, preferred_element_type=F32)
    kk = k * kk_ref[...]
    kk = kk * lax.rsqrt(jnp.maximum(headsum(kk * kk), 1e-24))
    k = k * (1.0 + (a - 1.0) * ka_ref[...])
    bonus_ref[0] = (headsum(r * k * rk_ref[...]) * v).astype(bonus_ref.dtype)
    a_vec = -kk
    b_vec = kk * a

    tri = tri_ref[...]
    cum = sum(jnp.dot(tri, part, preferred_element_type=F32) for part in _split3(lw))

    t_idx = lax.broadcasted_iota(jnp.int32, (c, PACK_LANES), 0)
    s_idx = lax.broadcasted_iota(jnp.int32, (c, PACK_LANES), 1) % RWKV_HEAD
    strict = s_idx < t_idx
    incl = s_idx <= t_idx
    eye = (s_idx == t_idx).astype(F32)
    prow = lax.broadcasted_iota(jnp.int32, (PACK_LANES, PACK_LANES), 0)
    pcol = lax.broadcasted_iota(jnp.int32, (PACK_LANES, PACK_LANES), 1)
    same_head = (prow // RWKV_HEAD) == (pcol // RWKV_HEAD)
    on_diag = prow == pcol
    head = _lane_head(PACK_LANES)

    nch = tm // c
    cum_last = jnp.concatenate(
        [jnp.broadcast_to(cum[(cc + 1) * c - 1:(cc + 1) * c, :], (c, w)) for cc in range(nch)], axis=0)
    e_pos = jnp.exp(cum)
    e_neg = jnp.exp(-cum)
    e_last = jnp.exp(cum_last - cum)
    at_all = a_vec * jnp.exp(cum - lw)
    bt_all = b_vec * e_neg
    kt_all = k * e_neg
    rt_all = r * e_pos
    bl_all = b_vec * e_last
    kl_all = k * e_last
    w_end_all = jnp.exp(cum_last)

    probs = [(cc, g) for cc in range(nch) for g in range(w // PACK_LANES)]
    sl = {pr: (slice(pr[0] * c, (pr[0] + 1) * c), slice(pr[1] * PACK_LANES, (pr[1] + 1) * PACK_LANES))
          for pr in probs}
    at = {pr: at_all[sl[pr]] for pr in probs}
    rt = {pr: rt_all[sl[pr]] for pr in probs}
    vv = {pr: v[sl[pr]] for pr in probs}
    m_bk = {pr: lax.dot_general(
        jnp.concatenate([at[pr], rt[pr]], axis=0).astype(BF16),
        jnp.concatenate([_bd_stack(bt_all[sl[pr]]), _bd_stack(kt_all[sl[pr]])], axis=0),
        (((1,), (1,)), ((), ())), preferred_element_type=F32) for pr in probs}
    m_b = {pr: m_bk[pr][:, :PACK_LANES] for pr in probs}
    m_k = {pr: m_bk[pr][:, PACK_LANES:] for pr in probs}
    l_ab = {pr: jnp.where(strict, m_b[pr][:c], 0.0) for pr in probs}
    t_inv = {pr: eye + l_ab[pr] for pr in probs}
    x = {pr: _pdot(l_ab[pr], l_ab[pr]) for pr in probs}
    levels = int(math.log2(c)) - 1
    for lvl in range(levels):
        if lvl < levels - 1:
            tx = {pr: _pdot2(x[pr], t_inv[pr], x[pr]) for pr in probs}
            t_inv = {pr: t_inv[pr] + tx[pr][0] for pr in probs}
            x = {pr: tx[pr][1] for pr in probs}
        else:
            t_inv = {pr: t_inv[pr] + _pdot(x[pr], t_inv[pr]) for pr in probs}
    akv = {pr: _pdot(jnp.where(strict, m_k[pr][:c], 0.0), vv[pr]) for pr in probs}
    au = {pr: _pdot2(t_inv[pr], at[pr], akv[pr]) for pr in probs}
    a_hat = {pr: au[pr][0] for pr in probs}
    u0 = {pr: au[pr][1] for pr in probs}
    r_hat, y0, p_bd, q_pack = {}, {}, {}, {}
    for pr in probs:
        m_rb = jnp.where(incl, m_b[pr][c:], 0.0)
        m_rk = jnp.where(incl, m_k[pr][c:], 0.0)
        r_add, y_add = _pdot2(m_rb, a_hat[pr], u0[pr])
        r_hat[pr] = (rt[pr] + r_add).astype(BF16)
        y0[pr] = y_add + _pdot(m_rk, vv[pr])
    for pr in probs:
        rs, ls = sl[pr]
        bl = bl_all[sl[pr]]
        p_full = jnp.dot(a_hat[pr].T.astype(BF16), bl.astype(BF16), preferred_element_type=F32)
        w_end = w_end_all[rs.start:rs.start + 1, ls]
        p_bd[pr] = (jnp.where(same_head, p_full, 0.0) + jnp.where(on_diag, w_end, 0.0)).astype(BF16)
        uv_t = jnp.concatenate([u0[pr], vv[pr]], axis=0).T
        bk = jnp.concatenate([bl, kl_all[sl[pr]]], axis=0).astype(BF16)
        f = jnp.dot(uv_t.astype(BF16), bk, preferred_element_type=F32)
        q_pack[pr] = sum(jnp.where(head == h, f[h * RWKV_HEAD:(h + 1) * RWKV_HEAD], 0.0)
                         for h in range(PACK_HEADS))

    packs = range(w // PACK_LANES)
    state = {g: s_ref[:, g * PACK_LANES:(g + 1) * PACK_LANES] for g in packs}
    for cc in range(nch):
        for g in packs:
            pr = (cc, g)
            rs, ls = sl[pr]
            y_ref[0, rs, ls] = (_pdot_nt(r_hat[pr], state[g]) + y0[pr]).astype(y_ref.dtype)
        state = {g: jnp.dot(state[g].astype(BF16), p_bd[(cc, g)], preferred_element_type=F32)
                 + q_pack[(cc, g)] for g in packs}
    for g in packs:
        s_ref[:, g * PACK_LANES:(g + 1) * PACK_LANES] = state[g]


def _wkv(rw, mu, w0, w_decay_up, a0, w_iclr_up, k_k, k_a, r_k, tm):
    b, s, width = rw.shape
    w = RWKV_WIDTH
    wlora = (0.5 * jnp.concatenate(
        [jnp.concatenate([w_decay_up, jnp.zeros((DECAY_LORA, w), F32)], 1),
         jnp.concatenate([jnp.zeros((ICLR_LORA, w), F32), w_iclr_up], 1)], 0)).astype(BF16)
    w0 = 0.5 * w0
    a0 = 0.5 * a0
    hid = np.arange(w) // RWKV_HEAD
    ones_bd = jnp.asarray((hid[:, None] == hid[None, :]).astype(np.float32), BF16)
    tid = np.arange(tm)
    tri = jnp.asarray(((tid[:, None] >= tid[None, :]) &
                       (tid[:, None] // WKV_CHUNK == tid[None, :] // WKV_CHUNK)).astype(np.float32), BF16)
    row = lambda t: t.reshape(1, -1)
    tok = lambda wd: pl.BlockSpec((1, tm, wd), lambda bi, i: (bi, i, 0))
    prev = pl.BlockSpec((1, 8, width), lambda bi, i: (bi, jnp.maximum(i * (tm // 8) - 1, 0), 0))
    return pl.pallas_call(
        functools.partial(_wkv_kernel, tm=tm),
        grid=(b, s // tm),
        in_specs=[tok(width), prev, _const_spec((1, width)), _const_spec(wlora.shape)]
                 + [_const_spec((1, w))] * 5 + [_const_spec(ones_bd.shape), _const_spec(tri.shape)],
        out_specs=[tok(w), tok(w)],
        out_shape=[jax.ShapeDtypeStruct((b, s, w), BF16),
                   jax.ShapeDtypeStruct((b, s, w), BF16)],
        scratch_shapes=[pltpu.VMEM((RWKV_HEAD, w), F32)],
        compiler_params=_params("parallel", "arbitrary"),
    )(rw, rw, row(mu), wlora, row(w0), row(a0), row(k_k), row(k_a), row(r_k), ones_bd, tri)


def _epilogue_kernel(x_ref, gate_ref, attn_ref, ga_ref, y_ref, bonus_ref, gb_ref, ma_ref, mb_ref,
                     ones_ref, gng_ref, gnb_ref, wpa_ref, wpb_ref, wout_ref, pg_ref, pb_ref, o_ref):
    silu_half = lambda th: th + th * jnp.tanh(th)
    ones_bd = ones_ref[...]
    headmean = lambda t: jnp.dot(t.astype(BF16), ones_bd, preferred_element_type=F32) * (1.0 / RWKV_HEAD)

    y = y_ref[0].astype(F32)
    yc = y - headmean(y)
    yn = yc * lax.rsqrt(headmean(yc * yc) + GN_EPS)
    yb = yn * gng_ref[...] + gnb_ref[...] + bonus_ref[0].astype(F32)
    yb = yb * silu_half(gb_ref[0].astype(F32))
    ya = attn_ref[0].astype(F32) * silu_half(ga_ref[0].astype(F32))
    ya_p = jnp.dot(ya.astype(BF16), wpa_ref[...], preferred_element_type=F32)
    yb_p = jnp.dot(yb.astype(BF16), wpb_ref[...], preferred_element_type=F32)
    merged2 = ((1.0 + jnp.tanh(ma_ref[0].astype(F32))) * ya_p
               + (1.0 + jnp.tanh(mb_ref[0].astype(F32))) * yb_p)
    sub2 = jnp.dot(merged2.astype(BF16), wout_ref[...], preferred_element_type=F32)
    z = ALPHA * x_ref[0] + (0.5 * (1.0 + gate_ref[0])) * sub2
    zc = z - jnp.mean(z, -1, keepdims=True)
    zn = zc * lax.rsqrt(jnp.mean(zc * zc, -1, keepdims=True) + LN_EPS)
    o_ref[0] = (zn * pg_ref[...] + pb_ref[...]).astype(o_ref.dtype)


def _epilogue(x, gate, attn, ga, y, bonus, gb, ma, mb, gn_g, gn_b, w_proj_a, w_proj_b, w_out,
              post_g, post_b, tm):
    b, s, d = x.shape
    w = RWKV_WIDTH
    hid = np.arange(w) // RWKV_HEAD
    ones_bd = jnp.asarray((hid[:, None] == hid[None, :]).astype(np.float32), BF16)
    row = lambda t: t.reshape(1, -1)
    tok = lambda wd: pl.BlockSpec((1, tm, wd), lambda bi, i: (bi, i, 0))
    vec = pl.BlockSpec((1, 1, d), lambda bi, i: (bi, 0, 0))
    return pl.pallas_call(
        _epilogue_kernel,
        grid=(b, s // tm),
        in_specs=[tok(d), vec, tok(w), tok(w), tok(w), tok(w), tok(w), tok(d), tok(d),
                  _const_spec((w, w)), _const_spec((1, w)), _const_spec((1, w)),
                  _const_spec((w, d)), _const_spec((w, d)), _const_spec((d, d)),
                  _const_spec((1, d)), _const_spec((1, d))],
        out_specs=tok(d),
        out_shape=jax.ShapeDtypeStruct((b, s, d), x.dtype),
        compiler_params=_params("parallel", "parallel"),
    )(x, gate, attn, ga, y, bonus, gb, ma, mb, ones_bd, row(gn_g), row(gn_b),
      w_proj_a.astype(BF16), w_proj_b.astype(BF16), w_out.astype(BF16), row(post_g), row(post_b))


def _pad_w_in(w_in):
    used = MLA_Q_RANK + MLA_KV_RANK + MLA_ROPE
    pad = jnp.zeros((w_in.shape[0], QKR_WIDTH - used), w_in.dtype)
    return jnp.concatenate([w_in[:, :used], pad, w_in[:, used:]], axis=1).astype(BF16)


def _layer(x, c, pos, w_ada, b_ada, w_in, q_norm_g, w_uq, kv_norm_g, w_ukv, mu_rwkv, w0, w_decay_up,
           a0, w_iclr_up, k_k, k_a, r_k, gn_g, gn_b, w_proj_a, w_proj_b, w_out, post_g, post_b):
    b, s, d = x.shape
    tm = min(512, s)
    ada = _adaln(c, w_ada, b_ada)
    shift, scale, gate = (ada[:, j * d:(j + 1) * d].reshape(b, 1, d) for j in range(3))
    qt, k, vt, ga, rw, gb, ma, mb = _ln_proj(x, scale, shift, _pad_w_in(w_in), pos, q_norm_g, kv_norm_g,
                                             w_uq, w_ukv, tm)
    attn = _attention(qt, k, vt)
    y, bonus = _wkv(rw, mu_rwkv, w0, w_decay_up, a0, w_iclr_up, k_k, k_a, r_k.reshape(-1), min(256, s))
    return _epilogue(x, gate, attn, ga, y, bonus, gb, ma, mb, gn_g, gn_b, w_proj_a, w_proj_b, w_out,
                     post_g, post_b, tm)


def kernel(x, c, positions, w_ada, b_ada, w_in, q_norm_g, w_uq, kv_norm_g, w_ukv, mu_rwkv, w0,
           w_decay_up, a0, w_iclr_up, k_k, k_a, r_k, gn_g, gn_b, w_proj_a, w_proj_b, w_out, post_g,
           post_b):
    pos = positions.astype(F32)[:, None, :]
    for l in range(w_ada.shape[0]):
        x = _layer(x, c, pos, w_ada[l], b_ada[l], w_in[l], q_norm_g[l], w_uq[l], kv_norm_g[l],
                   w_ukv[l], mu_rwkv[l], w0[l], w_decay_up[l], a0[l], w_iclr_up[l], k_k[l], k_a[l],
                   r_k[l], gn_g[l], gn_b[l], w_proj_a[l], w_proj_b[l], w_out[l], post_g[l], post_b[l])
    return x
```

```python
import functools
import math

import jax
import jax.numpy as jnp
import numpy as np
from jax import lax
from jax.experimental import pallas as pl
from jax.experimental.pallas import tpu as pltpu

F32 = jnp.float32
BF16 = jnp.bfloat16

D_MODEL = 1024
LN_EPS = 1e-5
RMS_EPS = 1e-6
GN_EPS = 64e-5

MLA_HEADS = 8
MLA_NOPE = 64
MLA_ROPE = 32
MLA_V = 64
MLA_QK = MLA_NOPE + MLA_ROPE
MLA_Q_RANK = 256
MLA_KV_RANK = 128
MLA_WIDTH = MLA_HEADS * MLA_V
ROPE_THETA = 10000.0
ATTN_CHUNK = 64

RWKV_HEADS = 8
RWKV_HEAD = 64
RWKV_WIDTH = RWKV_HEADS * RWKV_HEAD
DECAY_LORA = 64
ICLR_LORA = 64
RWKV_SHIFT_WIDTH = 3 * RWKV_WIDTH + DECAY_LORA + ICLR_LORA
WKV_CHUNK = 64

DEPTH = 1
ALPHA = (2.0 * DEPTH) ** 0.25

LANES = 128
HEAD_PAD = 128
V_ROWS = 80
KEY_TILE = 256
PACK_HEADS = 2
PACK_LANES = PACK_HEADS * RWKV_HEAD

QKR_WIDTH = 512
VMEM_LIMIT = 56 * 1024 * 1024

NEG_BIG = -1e30


def _const_spec(shape):
    n = len(shape)
    return pl.BlockSpec(shape, lambda *_: (0,) * n)


def _params(*sem):
    return pltpu.CompilerParams(dimension_semantics=sem, vmem_limit_bytes=VMEM_LIMIT)


def _adaln_kernel(c_ref, w_ref, b_ref, o_ref):
    c = c_ref[...]
    sc = c * jax.nn.sigmoid(c)
    o_ref[...] = jnp.dot(sc, w_ref[...], preferred_element_type=F32) + b_ref[...]


def _adaln(c, w_ada, b_ada):
    b = c.shape[0]
    return pl.pallas_call(
        _adaln_kernel,
        out_shape=jax.ShapeDtypeStruct((b, 3 * D_MODEL), F32),
        compiler_params=pltpu.CompilerParams(vmem_limit_bytes=VMEM_LIMIT),
    )(c, w_ada, b_ada.reshape(1, -1))


_PROJ_WIDTHS = (MLA_WIDTH, RWKV_SHIFT_WIDTH, RWKV_WIDTH, D_MODEL, D_MODEL)
_PROJ_SCALES = (0.5, 1.0, 0.5, 0.5, 0.5)


def _ln_proj_kernel(x_ref, scale_ref, shift_ref, w_in_ref, pos_ref, inv_ref, qg_ref, kvg_ref, wqp_ref,
                    wqr_ref, wk_ref, wvt_ref, qt_out, k_out, vt_out, *refs):
    out_refs, w_ref = refs[:-1], refs[-1]

    @pl.when(jnp.logical_and(pl.program_id(0) == 0, pl.program_id(1) == 0))
    def _():
        used = MLA_Q_RANK + MLA_KV_RANK + MLA_ROPE
        shift = QKR_WIDTH - used
        lane = lax.broadcasted_iota(jnp.int32, (1, LANES), 1)
        rows = 256
        for r in range(0, D_MODEL, rows):
            rs = slice(r, r + rows)
            aligned = (used // LANES) * LANES
            w_ref[rs, :aligned] = w_in_ref[rs, :aligned]
            edge = w_in_ref[rs, aligned:aligned + LANES]
            w_ref[rs, aligned:QKR_WIDTH] = jnp.where(lane < used - aligned, edge, jnp.zeros_like(edge))
            for c0 in range(QKR_WIDTH, w_ref.shape[1], LANES):
                w_ref[rs, c0:c0 + LANES] = w_in_ref[rs, c0 - shift:c0 - shift + LANES]

    x = x_ref[0]
    xc = x - jnp.mean(x, -1, keepdims=True)
    h = xc * lax.rsqrt(jnp.mean(xc * xc, -1, keepdims=True) + LN_EPS)
    hb = (h * (1.0 + scale_ref[0]) + shift_ref[0]).astype(BF16)
    latent = jnp.dot(hb, w_ref[:, :QKR_WIDTH], preferred_element_type=F32)
    off = QKR_WIDTH
    for n, (o_ref, width, scale) in enumerate(zip(out_refs, _PROJ_WIDTHS, _PROJ_SCALES)):
        acc = jnp.dot(hb, w_ref[:, off:off + width], preferred_element_type=F32)
        if scale != 1.0:
            acc = acc * scale
        o_ref[0] = acc.astype(o_ref.dtype)
        off += width
        if n == 1:
            _mla_prep(latent, pos_ref[0], inv_ref[...], qg_ref[...], kvg_ref[...], wqp_ref, wqr_ref,
                      wk_ref, wvt_ref, qt_out, k_out, vt_out)


def _ln_proj(x, scale, shift, w_in, pos, q_norm_g, kv_norm_g, w_uq, w_ukv, tm):
    b, s, d = x.shape
    wqp, wqr, wk, wvt = _mla_weights(w_uq, w_ukv)
    half = MLA_ROPE // 2
    inv = (ROPE_THETA ** (-jnp.arange(0, MLA_ROPE, 2, dtype=F32) / MLA_ROPE)).reshape(half, 1)
    width = MLA_HEADS * HEAD_PAD
    vrows = MLA_HEADS * V_ROWS
    tok = lambda w: pl.BlockSpec((1, tm, w), lambda bi, i: (bi, i, 0))
    vec = pl.BlockSpec((1, 1, d), lambda bi, i: (bi, 0, 0))
    tiles = lambda rows: pl.BlockSpec((1, tm // KEY_TILE, rows, KEY_TILE), lambda bi, i: (bi, i, 0, 0))
    return pl.pallas_call(
        _ln_proj_kernel,
        grid=(b, s // tm),
        in_specs=[tok(d), vec, vec,
                  pl.BlockSpec(w_in.shape, lambda bi, i: (0, 0), pipeline_mode=pl.Buffered(1)),
                  pl.BlockSpec((1, 1, tm), lambda bi, i: (bi, 0, i)),
                  _const_spec((half, 1)), _const_spec((1, MLA_Q_RANK)), _const_spec((1, MLA_KV_RANK)),
                  _const_spec(wqp.shape), _const_spec(wqr.shape), _const_spec(wk.shape),
                  _const_spec(wvt.shape)],
        out_specs=[tiles(width), tok(width), tiles(vrows)] + [tok(w) for w in _PROJ_WIDTHS],
        out_shape=[jax.ShapeDtypeStruct((b, s // KEY_TILE, width, KEY_TILE), BF16),
                   jax.ShapeDtypeStruct((b, s, width), BF16),
                   jax.ShapeDtypeStruct((b, s // KEY_TILE, vrows, KEY_TILE), BF16)]
                  + [jax.ShapeDtypeStruct((b, s, w), BF16) for w in _PROJ_WIDTHS],
        scratch_shapes=[pltpu.VMEM((d, QKR_WIDTH + sum(_PROJ_WIDTHS)), BF16)],
        compiler_params=_params("arbitrary", "arbitrary"),
    )(x, scale, shift, w_in, pos, inv, q_norm_g.reshape(1, -1), kv_norm_g.reshape(1, -1),
      wqp, wqr, wk, wvt)


def _mla_prep(t, pos, inv, qg, kvg, wqp_ref, wqr_ref, wk_ref, wvt_ref, qt_out, k_out, vt_out):
    tm = t.shape[0]
    half = MLA_ROPE // 2
    qc = t[:, :MLA_Q_RANK]
    kvc = t[:, MLA_Q_RANK:MLA_Q_RANK + MLA_KV_RANK]
    kr = t[:, MLA_Q_RANK + MLA_KV_RANK:]
    qn = qc * lax.rsqrt(jnp.mean(qc * qc, -1, keepdims=True) + RMS_EPS) * qg
    kvn = kvc * lax.rsqrt(jnp.mean(kvc * kvc, -1, keepdims=True) + RMS_EPS) * kvg
    qn_t = qn.T.astype(BF16)
    kvn_t = kvn.T.astype(BF16)
    kr_t = kr.T

    ang = inv * pos
    cos_h = jnp.cos(ang)
    sin_h = jnp.sin(ang)
    cos_r = jnp.concatenate([cos_h, cos_h], axis=0)
    sin_r = jnp.concatenate([sin_h, sin_h], axis=0)

    plain_t = jnp.dot(wqp_ref[...], qn_t, preferred_element_type=F32)
    rot_t = jnp.dot(wqr_ref[...], qn_t, preferred_element_type=F32)
    scale = MLA_QK ** -0.5 * math.log2(math.e)
    zpad = jnp.zeros((HEAD_PAD - MLA_QK, tm), BF16)
    for h in range(MLA_HEADS):
        base = h * HEAD_PAD
        nope = (plain_t[base:base + MLA_NOPE] * scale).astype(BF16)
        pe = ((plain_t[base + MLA_NOPE:base + MLA_QK] * cos_r
               + rot_t[h * MLA_ROPE:(h + 1) * MLA_ROPE] * sin_r) * scale).astype(BF16)
        q_t = jnp.concatenate([nope, pe, zpad], axis=0)
        for kt in range(tm // KEY_TILE):
            qt_out[0, kt, base:base + HEAD_PAD, :] = q_t[:, kt * KEY_TILE:(kt + 1) * KEY_TILE]

    k1 = kr_t[:half]
    k2 = kr_t[half:MLA_ROPE]
    kpe_t = jnp.concatenate([jnp.zeros((MLA_NOPE, tm), F32),
                             k1 * cos_h - k2 * sin_h, k1 * sin_h + k2 * cos_h,
                             jnp.zeros((HEAD_PAD - MLA_QK, tm), F32)], axis=0)
    kpe = kpe_t.T
    ka = jnp.dot(kvn.astype(BF16), wk_ref[...], preferred_element_type=F32)
    for h in range(MLA_HEADS):
        sl = slice(h * HEAD_PAD, (h + 1) * HEAD_PAD)
        k_out[0, :, sl] = (ka[:, sl] + kpe).astype(BF16)

    vt = jnp.dot(wvt_ref[...], kvn_t, preferred_element_type=F32)
    vrow = lax.broadcasted_iota(jnp.int32, (MLA_HEADS * V_ROWS, 1), 0) % V_ROWS
    vt = (vt + jnp.where(vrow == MLA_V, 1.0, 0.0)).astype(BF16)
    for kt in range(tm // KEY_TILE):
        vt_out[0, kt] = vt[:, kt * KEY_TILE:(kt + 1) * KEY_TILE]


def _mla_weights(w_uq, w_ukv):
    half = MLA_ROPE // 2
    wq = w_uq.reshape(MLA_Q_RANK, MLA_HEADS, MLA_QK)
    zq = jnp.zeros((MLA_Q_RANK, MLA_HEADS, HEAD_PAD - MLA_QK), F32)
    wq_plain_t = jnp.concatenate([wq, zq], -1).reshape(MLA_Q_RANK, -1).T.astype(BF16)
    t1 = wq[:, :, MLA_NOPE:MLA_NOPE + half]
    t2 = wq[:, :, MLA_NOPE + half:]
    wq_rot_t = jnp.concatenate([-t2, t1], -1).reshape(MLA_Q_RANK, -1).T.astype(BF16)

    wkv = w_ukv.reshape(MLA_KV_RANK, MLA_HEADS, MLA_NOPE + MLA_V)
    zk = jnp.zeros((MLA_KV_RANK, MLA_HEADS, HEAD_PAD - MLA_NOPE), F32)
    wk = jnp.concatenate([wkv[:, :, :MLA_NOPE], zk], -1).reshape(MLA_KV_RANK, -1).astype(BF16)
    zv = jnp.zeros((MLA_KV_RANK, MLA_HEADS, V_ROWS - MLA_V), F32)
    wvt = jnp.concatenate([wkv[:, :, MLA_NOPE:], zv], -1).reshape(MLA_KV_RANK, -1).T.astype(BF16)
    return wq_plain_t, wq_rot_t, wk, wvt


ATTN_SUBTILES = 2
ATTN_PAIRS_PER_ITER = 2


def _attn_kernel(qt_ref, k_ref, vt_ref, o_ref, s_buf):
    ts = KEY_TILE
    nsub = ATTN_SUBTILES
    n_qt = o_ref.shape[1] // (nsub * ts)
    key_chunk = lax.broadcasted_iota(jnp.int32, (ts, ts), 0) // ATTN_CHUNK
    qry_chunk = lax.broadcasted_iota(jnp.int32, (ts, ts), 1) // ATTN_CHUNK
    diag_mask = key_chunk <= qry_chunk
    lanes = [slice(hh * HEAD_PAD, (hh + 1) * HEAD_PAD) for hh in range(2)]
    chains = [(hh, sb) for hh in range(2) for sb in range(nsub)]

    def scores(i, j, slot, first_sub):
        if isinstance(j, int):
            start = j * ts
        else:
            start = pl.multiple_of(j * ts, ts)
        kj = [k_ref[0, pl.ds(start, ts), lanes[hh]] for hh in range(2)]
        for n, ch in enumerate(chains):
            if ch[1] >= first_sub:
                q_t = qt_ref[0, i * nsub + ch[1], lanes[ch[0]], :]
                s_buf[slot, n] = jnp.dot(kj[ch[0]], q_t, preferred_element_type=F32)

    def consume(j, slot, carry, masked_sub, first_sub):
        vj = [vt_ref[0, j, hh * V_ROWS:(hh + 1) * V_ROWS, :] for hh in range(2)]
        active = [(n, ch) for n, ch in enumerate(chains) if ch[1] >= first_sub]
        s = {ch: s_buf[slot, n] for n, ch in active}
        for _, ch in active:
            if ch[1] == masked_sub:
                s[ch] = jnp.where(diag_mask, s[ch], NEG_BIG)
        m_new = {ch: jnp.maximum(carry[ch][0], jnp.max(s[ch], axis=0, keepdims=True)) for _, ch in active}
        alpha = {ch: jnp.exp2(carry[ch][0] - m_new[ch]) for _, ch in active}
        out = dict(carry)
        for _, ch in active:
            p = jnp.exp2(s[ch] - m_new[ch]).astype(BF16)
            pv = jnp.dot(vj[ch[0]], p, preferred_element_type=F32)
            out[ch] = (m_new[ch], carry[ch][1] * alpha[ch] + pv)
        return out

    init = tuple(v for _ in chains
                 for v in (jnp.full((1, ts), NEG_BIG, F32), jnp.zeros((V_ROWS, ts), F32)))
    scores(0, 0, 0, 0)
    for i in range(n_qt):
        sa, sb_ = (0, 1) if i % 2 == 0 else (2, 3)
        next_sa = 2 if i % 2 == 0 else 0

        def pair_step(t, carry, i=i, sa=sa, sb_=sb_):
            scores(i, 2 * t + 1, sb_, 0)
            carry = consume(2 * t, sa, carry, -1, 0)
            scores(i, 2 * t + 2, sa, 0)
            return consume(2 * t + 1, sb_, carry, -1, 0)

        def loop_body(t, flat):
            carry = {ch: (flat[2 * n], flat[2 * n + 1]) for n, ch in enumerate(chains)}
            for u in range(ATTN_PAIRS_PER_ITER):
                carry = pair_step(ATTN_PAIRS_PER_ITER * t + u, carry)
            return tuple(v for ch in chains for v in carry[ch])

        n_iter, n_rest = divmod(i, ATTN_PAIRS_PER_ITER)
        flat = lax.fori_loop(0, n_iter, loop_body, init)
        carry = {ch: (flat[2 * n], flat[2 * n + 1]) for n, ch in enumerate(chains)}
        for u in range(n_rest):
            carry = pair_step(ATTN_PAIRS_PER_ITER * n_iter + u, carry)
        scores(i, 2 * i + 1, sb_, 1)
        if i + 1 < n_qt:
            scores(i + 1, 0, next_sa, 0)
        carry = consume(2 * i, sa, carry, 0, 0)
        carry = consume(2 * i + 1, sb_, carry, 1, 1)

        for sb in range(nsub):
            normed = []
            for hh in range(2):
                acc = carry[(hh, sb)][1]
                normed.append(acc[:MLA_V] / acc[MLA_V:MLA_V + 1])
            out_t = jnp.concatenate(normed, axis=0)
            row0 = (i * nsub + sb) * ts
            o_ref[0, row0:row0 + ts, :] = out_t.T.astype(o_ref.dtype)


def _attention(qt, k, vt):
    b, s, _ = k.shape
    pairs = MLA_HEADS // 2
    nsub = ATTN_SUBTILES
    return pl.pallas_call(
        _attn_kernel,
        grid=(b, pairs),
        in_specs=[pl.BlockSpec((1, s // KEY_TILE, 2 * HEAD_PAD, KEY_TILE), lambda bi, g: (bi, 0, g, 0)),
                  pl.BlockSpec((1, s, 2 * HEAD_PAD), lambda bi, g: (bi, 0, g)),
                  pl.BlockSpec((1, s // KEY_TILE, 2 * V_ROWS, KEY_TILE), lambda bi, g: (bi, 0, g, 0))],
        out_specs=pl.BlockSpec((1, s, 2 * MLA_V), lambda bi, g: (bi, 0, g)),
        out_shape=jax.ShapeDtypeStruct((b, s, MLA_WIDTH), BF16),
        scratch_shapes=[pltpu.VMEM((4, 2 * nsub, KEY_TILE, KEY_TILE), F32)],
        compiler_params=_params("parallel", "parallel"),
    )(qt, k, vt)


def _lane_head(width):
    return lax.broadcasted_iota(jnp.int32, (1, width), 1) // RWKV_HEAD


def _bd_stack(x):
    head = _lane_head(x.shape[1])
    xb = x.astype(BF16)
    return jnp.concatenate([jnp.where(head == h, xb, jnp.zeros_like(xb)) for h in range(PACK_HEADS)], axis=0)


def _pdot(a, b):
    return jnp.dot(a.astype(BF16), _bd_stack(b), preferred_element_type=F32)


def _pdot2(a, b1, b2):
    rhs = jnp.concatenate([_bd_stack(b1), _bd_stack(b2)], axis=1)
    out = jnp.dot(a.astype(BF16), rhs, preferred_element_type=F32)
    return out[:, :PACK_LANES], out[:, PACK_LANES:]


def _pdot_nt(a, b):
    return lax.dot_general(a.astype(BF16), _bd_stack(b), (((1,), (1,)), ((), ())),
                           preferred_element_type=F32)


def _split3(x):
    x1 = x.astype(BF16)
    r1 = x - x1.astype(F32)
    x2 = r1.astype(BF16)
    x3 = (r1 - x2.astype(F32)).astype(BF16)
    return x1, x2, x3


def _wkv_kernel(rw_ref, prev_ref, mu_ref, wlora_ref, w0_ref, a0_ref, kk_ref, ka_ref, rk_ref,
                ones_ref, tri_ref, y_ref, bonus_ref, s_ref, rhat_s, y0_s, p_s, q_s, *, tm, nt, n_tiles):
    g = pl.program_id(0)
    c = WKV_CHUNK
    w = RWKV_WIDTH
    nch = tm // c
    packs = range(w // PACK_LANES)
    lanes_of = lambda gk: slice(gk * PACK_LANES, (gk + 1) * PACK_LANES)

    @pl.when(g == 0)
    def _():
        for ref in (s_ref, rhat_s, y0_s, p_s, q_s):
            ref[...] = jnp.zeros_like(ref)

    scan_starts_sequence = (jnp.maximum(g - 1, 0) % nt) == 0
    scan = {"state": {gk: jnp.where(scan_starts_sequence, 0.0, s_ref[:, lanes_of(gk)]) for gk in packs},
            "chunk": 0}

    def advance_scan():
        cc = scan["chunk"]
        if cc >= nch:
            return
        state = scan["state"]
        rs = slice(cc * c, (cc + 1) * c)
        for gk in packs:
            ls = lanes_of(gk)
            y_ref[0, rs, ls] = (_pdot_nt(rhat_s[rs, ls], state[gk]) + y0_s[rs, ls]).astype(y_ref.dtype)
        state = {gk: jnp.dot(state[gk].astype(BF16), p_s[cc * PACK_LANES:(cc + 1) * PACK_LANES, lanes_of(gk)],
                             preferred_element_type=F32) + q_s[rs, lanes_of(gk)] for gk in packs}
        scan["state"] = state
        scan["chunk"] = cc + 1
        if cc + 1 == nch:
            for gk in packs:
                s_ref[:, lanes_of(gk)] = state[gk]

    i = jnp.minimum(g, n_tiles - 1) % nt
    u_raw = rw_ref[0].astype(F32)
    prev_row = prev_ref[0, 7:8, :].astype(F32) * (i > 0).astype(F32)
    row = lax.broadcasted_iota(jnp.int32, (tm, 1), 0)
    u_prev = jnp.where(row == 0, prev_row, pltpu.roll(u_raw, 1, axis=0))
    u = u_raw + (u_prev - u_raw) * mu_ref[...]
    r = u[:, :w]
    k = u[:, w:2 * w]
    v = u[:, 2 * w:3 * w]
    lora_in = u[:, 3 * w:]
    lane = lax.broadcasted_iota(jnp.int32, (1, DECAY_LORA + ICLR_LORA), 1)
    lora_in = jnp.where(lane < DECAY_LORA, jnp.tanh(lora_in), lora_in)
    lora = jnp.dot(lora_in.astype(BF16), wlora_ref[...], preferred_element_type=F32)
    c_lw = -0.5 * math.exp(-0.5)
    lw = c_lw + c_lw * jnp.tanh(w0_ref[...] + lora[:, :w])
    a = 0.5 + 0.5 * jnp.tanh(a0_ref[...] + lora[:, w:])

    ones_bd = ones_ref[...]
    headsum = lambda t: jnp.dot(t.astype(BF16), ones_bd, preferred_element_type=F32)
    kk = k * kk_ref[...]
    kk = kk * lax.rsqrt(jnp.maximum(headsum(kk * kk), 1e-24))
    k = k * (1.0 + (a - 1.0) * ka_ref[...])
    bonus_ref[0] = (headsum(r * k * rk_ref[...]) * v).astype(bonus_ref.dtype)
    a_vec = -kk
    b_vec = kk * a

    tri = tri_ref[...]
    cum = sum(jnp.dot(tri, part, preferred_element_type=F32) for part in _split3(lw))

    t_idx = lax.broadcasted_iota(jnp.int32, (c, PACK_LANES), 0)
    s_idx = lax.broadcasted_iota(jnp.int32, (c, PACK_LANES), 1) % RWKV_HEAD
    strict = s_idx < t_idx
    incl = s_idx <= t_idx
    eye = (s_idx == t_idx).astype(F32)
    prow = lax.broadcasted_iota(jnp.int32, (PACK_LANES, PACK_LANES), 0)
    pcol = lax.broadcasted_iota(jnp.int32, (PACK_LANES, PACK_LANES), 1)
    same_head = (prow // RWKV_HEAD) == (pcol // RWKV_HEAD)
    on_diag = prow == pcol
    head = _lane_head(PACK_LANES)

    cum_last = jnp.concatenate(
        [jnp.broadcast_to(cum[(cc + 1) * c - 1:(cc + 1) * c, :], (c, w)) for cc in range(nch)], axis=0)
    e_pos = jnp.exp(cum)
    e_neg = jnp.exp(-cum)
    e_last = jnp.exp(cum_last - cum)
    at_all = a_vec * jnp.exp(cum - lw)
    bt_all = b_vec * e_neg
    kt_all = k * e_neg
    rt_all = r * e_pos
    bl_all = b_vec * e_last
    kl_all = k * e_last
    w_end_all = jnp.exp(cum_last)

    probs = [(cc, gk) for cc in range(nch) for gk in packs]
    sl = {pr: (slice(pr[0] * c, (pr[0] + 1) * c), slice(pr[1] * PACK_LANES, (pr[1] + 1) * PACK_LANES))
          for pr in probs}
    at = {pr: at_all[sl[pr]] for pr in probs}
    rt = {pr: rt_all[sl[pr]] for pr in probs}
    vv = {pr: v[sl[pr]] for pr in probs}
    m_bk = {pr: lax.dot_general(
        jnp.concatenate([at[pr], rt[pr]], axis=0).astype(BF16),
        jnp.concatenate([_bd_stack(bt_all[sl[pr]]), _bd_stack(kt_all[sl[pr]])], axis=0),
        (((1,), (1,)), ((), ())), preferred_element_type=F32) for pr in probs}
    advance_scan()
    m_b = {pr: m_bk[pr][:, :PACK_LANES] for pr in probs}
    m_k = {pr: m_bk[pr][:, PACK_LANES:] for pr in probs}
    l_ab = {pr: jnp.where(strict, m_b[pr][:c], 0.0) for pr in probs}
    t_inv = {pr: eye + l_ab[pr] for pr in probs}
    x = {pr: _pdot(l_ab[pr], l_ab[pr]) for pr in probs}
    levels = int(math.log2(c)) - 1
    for lvl in range(levels):
        advance_scan()
        if lvl < levels - 1:
            tx = {pr: _pdot2(x[pr], t_inv[pr], x[pr]) for pr in probs}
            t_inv = {pr: t_inv[pr] + tx[pr][0] for pr in probs}
            x = {pr: tx[pr][1] for pr in probs}
        else:
            t_inv = {pr: t_inv[pr] + _pdot(x[pr], t_inv[pr]) for pr in probs}
    akv = {pr: _pdot(jnp.where(strict, m_k[pr][:c], 0.0), vv[pr]) for pr in probs}
    au = {pr: _pdot2(t_inv[pr], at[pr], akv[pr]) for pr in probs}
    a_hat = {pr: au[pr][0] for pr in probs}
    u0 = {pr: au[pr][1] for pr in probs}
    while scan["chunk"] < nch:
        advance_scan()
    for pr in probs:
        rs, ls = sl[pr]
        m_rb = jnp.where(incl, m_b[pr][c:], 0.0)
        m_rk = jnp.where(incl, m_k[pr][c:], 0.0)
        r_add, y_add = _pdot2(m_rb, a_hat[pr], u0[pr])
        rhat_s[rs, ls] = (rt[pr] + r_add).astype(rhat_s.dtype)
        y0_s[rs, ls] = y_add + _pdot(m_rk, vv[pr])
    for pr in probs:
        rs, ls = sl[pr]
        bl = bl_all[sl[pr]]
        p_full = jnp.dot(a_hat[pr].T.astype(BF16), bl.astype(BF16), preferred_element_type=F32)
        w_end = w_end_all[rs.start:rs.start + 1, ls]
        p_bd = jnp.where(same_head, p_full, 0.0) + jnp.where(on_diag, w_end, 0.0)
        p_s[pr[0] * PACK_LANES:(pr[0] + 1) * PACK_LANES, ls] = p_bd.astype(p_s.dtype)
        uv_t = jnp.concatenate([u0[pr], vv[pr]], axis=0).T
        bk = jnp.concatenate([bl, kl_all[sl[pr]]], axis=0).astype(BF16)
        f = jnp.dot(uv_t.astype(BF16), bk, preferred_element_type=F32)
        q_s[rs, ls] = sum(jnp.where(head == h, f[h * RWKV_HEAD:(h + 1) * RWKV_HEAD], 0.0)
                          for h in range(PACK_HEADS))


def _wkv(rw, mu, w0, w_decay_up, a0, w_iclr_up, k_k, k_a, r_k, tm):
    b, s, width = rw.shape
    w = RWKV_WIDTH
    nt = s // tm
    n_tiles = b * nt
    wlora = (0.5 * jnp.concatenate(
        [jnp.concatenate([w_decay_up, jnp.zeros((DECAY_LORA, w), F32)], 1),
         jnp.concatenate([jnp.zeros((ICLR_LORA, w), F32), w_iclr_up], 1)], 0)).astype(BF16)
    w0 = 0.5 * w0
    a0 = 0.5 * a0
    hid = np.arange(w) // RWKV_HEAD
    ones_bd = jnp.asarray((hid[:, None] == hid[None, :]).astype(np.float32), BF16)
    tid = np.arange(tm)
    tri = jnp.asarray(((tid[:, None] >= tid[None, :]) &
                       (tid[:, None] // WKV_CHUNK == tid[None, :] // WKV_CHUNK)).astype(np.float32), BF16)
    row = lambda t: t.reshape(1, -1)
    prep_tile = lambda g: jnp.minimum(g, n_tiles - 1)
    scan_tile = lambda g: jnp.maximum(g - 1, 0)
    rw_spec = pl.BlockSpec((1, tm, width), lambda g: (prep_tile(g) // nt, prep_tile(g) % nt, 0))
    prev = pl.BlockSpec((1, 8, width),
                        lambda g: (prep_tile(g) // nt, jnp.maximum((prep_tile(g) % nt) * (tm // 8) - 1, 0), 0))
    y_spec = pl.BlockSpec((1, tm, w), lambda g: (scan_tile(g) // nt, scan_tile(g) % nt, 0))
    bonus_spec = pl.BlockSpec((1, tm, w), lambda g: (prep_tile(g) // nt, prep_tile(g) % nt, 0))
    n_chunks = tm // WKV_CHUNK
    return pl.pallas_call(
        functools.partial(_wkv_kernel, tm=tm, nt=nt, n_tiles=n_tiles),
        grid=(n_tiles + 1,),
        in_specs=[rw_spec, prev, _const_spec((1, width)), _const_spec(wlora.shape)]
                 + [_const_spec((1, w))] * 5 + [_const_spec(ones_bd.shape), _const_spec(tri.shape)],
        out_specs=[y_spec, bonus_spec],
        out_shape=[jax.ShapeDtypeStruct((b, s, w), BF16),
                   jax.ShapeDtypeStruct((b, s, w), BF16)],
        scratch_shapes=[pltpu.VMEM((RWKV_HEAD, w), F32),
                        pltpu.VMEM((tm, w), BF16),
                        pltpu.VMEM((tm, w), F32),
                        pltpu.VMEM((n_chunks * PACK_LANES, w), BF16),
                        pltpu.VMEM((tm, w), F32)],
        compiler_params=_params("arbitrary"),
    )(rw, rw, row(mu), wlora, row(w0), row(a0), row(k_k), row(k_a), row(r_k), ones_bd, tri)


def _epilogue_kernel(x_ref, gate_ref, attn_ref, ga_ref, y_ref, bonus_ref, gb_ref, ma_ref, mb_ref,
                     ones_ref, gng_ref, gnb_ref, wpa_ref, wpb_ref, wout_ref, pg_ref, pb_ref, o_ref):
    silu_half = lambda th: th + th * jnp.tanh(th)
    ones_bd = ones_ref[...]
    headmean = lambda t: jnp.dot(t.astype(BF16), ones_bd, preferred_element_type=F32) * (1.0 / RWKV_HEAD)

    y = y_ref[0].astype(F32)
    yc = y - headmean(y)
    yn = yc * lax.rsqrt(headmean(yc * yc) + GN_EPS)
    yb = yn * gng_ref[...] + gnb_ref[...] + bonus_ref[0].astype(F32)
    yb = yb * silu_half(gb_ref[0].astype(F32))
    ya = attn_ref[0].astype(F32) * silu_half(ga_ref[0].astype(F32))
    ya_p = jnp.dot(ya.astype(BF16), wpa_ref[...], preferred_element_type=F32)
    yb_p = jnp.dot(yb.astype(BF16), wpb_ref[...], preferred_element_type=F32)
    merged2 = ((1.0 + jnp.tanh(ma_ref[0].astype(F32))) * ya_p
               + (1.0 + jnp.tanh(mb_ref[0].astype(F32))) * yb_p)
    sub2 = jnp.dot(merged2.astype(BF16), wout_ref[...], preferred_element_type=F32)
    z = ALPHA * x_ref[0] + (0.5 * (1.0 + gate_ref[0])) * sub2
    zc = z - jnp.mean(z, -1, keepdims=True)
    zn = zc * lax.rsqrt(jnp.mean(zc * zc, -1, keepdims=True) + LN_EPS)
    o_ref[0] = (zn * pg_ref[...] + pb_ref[...]).astype(o_ref.dtype)


def _epilogue(x, gate, attn, ga, y, bonus, gb, ma, mb, gn_g, gn_b, w_proj_a, w_proj_b, w_out,
              post_g, post_b, tm):
    b, s, d = x.shape
    w = RWKV_WIDTH
    hid = np.arange(w) // RWKV_HEAD
    ones_bd = jnp.asarray((hid[:, None] == hid[None, :]).astype(np.float32), BF16)
    row = lambda t: t.reshape(1, -1)
    tok = lambda wd: pl.BlockSpec((1, tm, wd), lambda bi, i: (bi, i, 0))
    vec = pl.BlockSpec((1, 1, d), lambda bi, i: (bi, 0, 0))
    return pl.pallas_call(
        _epilogue_kernel,
        grid=(b, s // tm),
        in_specs=[tok(d), vec, tok(w), tok(w), tok(w), tok(w), tok(w), tok(d), tok(d),
                  _const_spec((w, w)), _const_spec((1, w)), _const_spec((1, w)),
                  _const_spec((w, d)), _const_spec((w, d)), _const_spec((d, d)),
                  _const_spec((1, d)), _const_spec((1, d))],
        out_specs=tok(d),
        out_shape=jax.ShapeDtypeStruct((b, s, d), x.dtype),
        compiler_params=_params("parallel", "parallel"),
    )(x, gate, attn, ga, y, bonus, gb, ma, mb, ones_bd, row(gn_g), row(gn_b),
      w_proj_a.astype(BF16), w_proj_b.astype(BF16), w_out.astype(BF16), row(post_g), row(post_b))


def _layer(x, c, pos, w_ada, b_ada, w_in, q_norm_g, w_uq, kv_norm_g, w_ukv, mu_rwkv, w0, w_decay_up,
           a0, w_iclr_up, k_k, k_a, r_k, gn_g, gn_b, w_proj_a, w_proj_b, w_out, post_g, post_b):
    b, s, d = x.shape
    tm = min(512, s)
    ada = _adaln(c, w_ada, b_ada)
    shift, scale, gate = (ada[:, j * d:(j + 1) * d].reshape(b, 1, d) for j in range(3))
    qt, k, vt, ga, rw, gb, ma, mb = _ln_proj(x, scale, shift, w_in.astype(BF16), pos, q_norm_g, kv_norm_g,
                                             w_uq, w_ukv, tm)
    attn = _attention(qt, k, vt)
    y, bonus = _wkv(rw, mu_rwkv, w0, w_decay_up, a0, w_iclr_up, k_k, k_a, r_k.reshape(-1), min(256, s))
    return _epilogue(x, gate, attn, ga, y, bonus, gb, ma, mb, gn_g, gn_b, w_proj_a, w_proj_b, w_out,
                     post_g, post_b, tm)


def kernel(x, c, positions, w_ada, b_ada, w_in, q_norm_g, w_uq, kv_norm_g, w_ukv, mu_rwkv, w0,
           w_decay_up, a0, w_iclr_up, k_k, k_a, r_k, gn_g, gn_b, w_proj_a, w_proj_b, w_out, post_g,
           post_b):
    pos = positions.astype(F32)[:, None, :]
    for l in range(w_ada.shape[0]):
        x = _layer(x, c, pos, w_ada[l], b_ada[l], w_in[l], q_norm_g[l], w_uq[l], kv_norm_g[l],
                   w_ukv[l], mu_rwkv[l], w0[l], w_decay_up[l], a0[l], w_iclr_up[l], k_k[l], k_a[l],
                   r_k[l], gn_g[l], gn_b[l], w_proj_a[l], w_proj_b[l], w_out[l], post_g[l], post_b[l])
    return x
```

```python
import functools
import math

import jax
import jax.numpy as jnp
import numpy as np
from jax import lax
from jax.experimental import pallas as pl
from jax.experimental.pallas import tpu as pltpu

F32 = jnp.float32
BF16 = jnp.bfloat16

D_MODEL = 1024
LN_EPS = 1e-5
RMS_EPS = 1e-6
GN_EPS = 64e-5

MLA_HEADS = 8
MLA_NOPE = 64
MLA_ROPE = 32
MLA_V = 64
MLA_QK = MLA_NOPE + MLA_ROPE
MLA_Q_RANK = 256
MLA_KV_RANK = 128
MLA_WIDTH = MLA_HEADS * MLA_V
ROPE_THETA = 10000.0
ATTN_CHUNK = 64

RWKV_HEADS = 8
RWKV_HEAD = 64
RWKV_WIDTH = RWKV_HEADS * RWKV_HEAD
DECAY_LORA = 64
ICLR_LORA = 64
RWKV_SHIFT_WIDTH = 3 * RWKV_WIDTH + DECAY_LORA + ICLR_LORA
WKV_CHUNK = 64

DEPTH = 1
ALPHA = (2.0 * DEPTH) ** 0.25

LANES = 128
HEAD_PAD = 128
V_ROWS = 80
KEY_TILE = 256
PACK_HEADS = 2
PACK_LANES = PACK_HEADS * RWKV_HEAD

QKR_WIDTH = 512
VMEM_LIMIT = 56 * 1024 * 1024

NEG_BIG = -1e30


def _const_spec(shape):
    n = len(shape)
    return pl.BlockSpec(shape, lambda *_: (0,) * n)


def _params(*sem):
    return pltpu.CompilerParams(dimension_semantics=sem, vmem_limit_bytes=VMEM_LIMIT)


def _adaln_kernel(c_ref, w_ref, b_ref, o_ref):
    c = c_ref[...]
    sc = c * jax.nn.sigmoid(c)
    o_ref[...] = jnp.dot(sc, w_ref[...], preferred_element_type=F32) + b_ref[...]


def _adaln(c, w_ada, b_ada):
    b = c.shape[0]
    return pl.pallas_call(
        _adaln_kernel,
        out_shape=jax.ShapeDtypeStruct((b, 3 * D_MODEL), F32),
        compiler_params=pltpu.CompilerParams(vmem_limit_bytes=VMEM_LIMIT),
    )(c, w_ada, b_ada.reshape(1, -1))


_PROJ_WIDTHS = (MLA_WIDTH, RWKV_SHIFT_WIDTH, RWKV_WIDTH, D_MODEL, D_MODEL)
_PROJ_SCALES = (0.5, 1.0, 0.5, 0.5, 0.5)


def _ln_proj_kernel(x_ref, scale_ref, shift_ref, w_in_ref, pos_ref, inv_ref, qg_ref, kvg_ref, wqp_ref,
                    wqr_ref, wk_ref, wvt_ref, qt_out, k_out, vt_out, *refs):
    out_refs, w_ref = refs[:-1], refs[-1]

    @pl.when(jnp.logical_and(pl.program_id(0) == 0, pl.program_id(1) == 0))
    def _():
        used = MLA_Q_RANK + MLA_KV_RANK + MLA_ROPE
        shift = QKR_WIDTH - used
        lane = lax.broadcasted_iota(jnp.int32, (1, LANES), 1)
        rows = 256
        for r in range(0, D_MODEL, rows):
            rs = slice(r, r + rows)
            aligned = (used // LANES) * LANES
            w_ref[rs, :aligned] = w_in_ref[rs, :aligned]
            edge = w_in_ref[rs, aligned:aligned + LANES]
            w_ref[rs, aligned:QKR_WIDTH] = jnp.where(lane < used - aligned, edge, jnp.zeros_like(edge))
            for c0 in range(QKR_WIDTH, w_ref.shape[1], LANES):
                w_ref[rs, c0:c0 + LANES] = w_in_ref[rs, c0 - shift:c0 - shift + LANES]

    x = x_ref[0]
    xc = x - jnp.mean(x, -1, keepdims=True)
    h = xc * lax.rsqrt(jnp.mean(xc * xc, -1, keepdims=True) + LN_EPS)
    hb = (h * (1.0 + scale_ref[0]) + shift_ref[0]).astype(BF16)
    latent = jnp.dot(hb, w_ref[:, :QKR_WIDTH], preferred_element_type=F32)
    off = QKR_WIDTH
    for n, (o_ref, width, scale) in enumerate(zip(out_refs, _PROJ_WIDTHS, _PROJ_SCALES)):
        acc = jnp.dot(hb, w_ref[:, off:off + width], preferred_element_type=F32)
        if scale != 1.0:
            acc = acc * scale
        o_ref[0] = acc.astype(o_ref.dtype)
        off += width
        if n == 1:
            _mla_prep(latent, pos_ref[0], inv_ref[...], qg_ref[...], kvg_ref[...], wqp_ref, wqr_ref,
                      wk_ref, wvt_ref, qt_out, k_out, vt_out)


def _ln_proj(x, scale, shift, w_in, pos, q_norm_g, kv_norm_g, w_uq, w_ukv, tm):
    b, s, d = x.shape
    wqp, wqr, wk, wvt = _mla_weights(w_uq, w_ukv)
    half = MLA_ROPE // 2
    inv = (ROPE_THETA ** (-jnp.arange(0, MLA_ROPE, 2, dtype=F32) / MLA_ROPE)).reshape(half, 1)
    width = MLA_HEADS * HEAD_PAD
    vrows = MLA_HEADS * V_ROWS
    tok = lambda w: pl.BlockSpec((1, tm, w), lambda bi, i: (bi, i, 0))
    vec = pl.BlockSpec((1, 1, d), lambda bi, i: (bi, 0, 0))
    tiles = lambda rows: pl.BlockSpec((1, tm // KEY_TILE, rows, KEY_TILE), lambda bi, i: (bi, i, 0, 0))
    return pl.pallas_call(
        _ln_proj_kernel,
        grid=(b, s // tm),
        in_specs=[tok(d), vec, vec,
                  pl.BlockSpec(w_in.shape, lambda bi, i: (0, 0), pipeline_mode=pl.Buffered(1)),
                  pl.BlockSpec((1, 1, tm), lambda bi, i: (bi, 0, i)),
                  _const_spec((half, 1)), _const_spec((1, MLA_Q_RANK)), _const_spec((1, MLA_KV_RANK)),
                  _const_spec(wqp.shape), _const_spec(wqr.shape), _const_spec(wk.shape),
                  _const_spec(wvt.shape)],
        out_specs=[tiles(width), tok(width), tiles(vrows)] + [tok(w) for w in _PROJ_WIDTHS],
        out_shape=[jax.ShapeDtypeStruct((b, s // KEY_TILE, width, KEY_TILE), BF16),
                   jax.ShapeDtypeStruct((b, s, width), BF16),
                   jax.ShapeDtypeStruct((b, s // KEY_TILE, vrows, KEY_TILE), BF16)]
                  + [jax.ShapeDtypeStruct((b, s, w), BF16) for w in _PROJ_WIDTHS],
        scratch_shapes=[pltpu.VMEM((d, QKR_WIDTH + sum(_PROJ_WIDTHS)), BF16)],
        compiler_params=_params("arbitrary", "arbitrary"),
    )(x, scale, shift, w_in, pos, inv, q_norm_g.reshape(1, -1), kv_norm_g.reshape(1, -1),
      wqp, wqr, wk, wvt)


def _mla_prep(t, pos, inv, qg, kvg, wqp_ref, wqr_ref, wk_ref, wvt_ref, qt_out, k_out, vt_out):
    tm = t.shape[0]
    half = MLA_ROPE // 2
    qc = t[:, :MLA_Q_RANK]
    kvc = t[:, MLA_Q_RANK:MLA_Q_RANK + MLA_KV_RANK]
    kr = t[:, MLA_Q_RANK + MLA_KV_RANK:]
    qn = qc * lax.rsqrt(jnp.mean(qc * qc, -1, keepdims=True) + RMS_EPS) * qg
    kvn = kvc * lax.rsqrt(jnp.mean(kvc * kvc, -1, keepdims=True) + RMS_EPS) * kvg
    qn_t = qn.T.astype(BF16)
    kvn_t = kvn.T.astype(BF16)
    kr_t = kr.T

    ang = inv * pos
    cos_h = jnp.cos(ang)
    sin_h = jnp.sin(ang)
    cos_r = jnp.concatenate([cos_h, cos_h], axis=0)
    sin_r = jnp.concatenate([sin_h, sin_h], axis=0)

    plain_t = jnp.dot(wqp_ref[...], qn_t, preferred_element_type=F32)
    rot_t = jnp.dot(wqr_ref[...], qn_t, preferred_element_type=F32)
    scale = MLA_QK ** -0.5 * math.log2(math.e)
    zpad = jnp.zeros((HEAD_PAD - MLA_QK, tm), BF16)
    for h in range(MLA_HEADS):
        base = h * HEAD_PAD
        nope = (plain_t[base:base + MLA_NOPE] * scale).astype(BF16)
        pe = ((plain_t[base + MLA_NOPE:base + MLA_QK] * cos_r
               + rot_t[h * MLA_ROPE:(h + 1) * MLA_ROPE] * sin_r) * scale).astype(BF16)
        q_t = jnp.concatenate([nope, pe, zpad], axis=0)
        for kt in range(tm // KEY_TILE):
            qt_out[0, kt, base:base + HEAD_PAD, :] = q_t[:, kt * KEY_TILE:(kt + 1) * KEY_TILE]

    k1 = kr_t[:half]
    k2 = kr_t[half:MLA_ROPE]
    kpe_t = jnp.concatenate([jnp.zeros((MLA_NOPE, tm), F32),
                             k1 * cos_h - k2 * sin_h, k1 * sin_h + k2 * cos_h,
                             jnp.zeros((HEAD_PAD - MLA_QK, tm), F32)], axis=0)
    kpe = kpe_t.T
    ka = jnp.dot(kvn.astype(BF16), wk_ref[...], preferred_element_type=F32)
    for h in range(MLA_HEADS):
        sl = slice(h * HEAD_PAD, (h + 1) * HEAD_PAD)
        k_out[0, :, sl] = (ka[:, sl] + kpe).astype(BF16)

    vt = jnp.dot(wvt_ref[...], kvn_t, preferred_element_type=F32)
    vrow = lax.broadcasted_iota(jnp.int32, (MLA_HEADS * V_ROWS, 1), 0) % V_ROWS
    vt = (vt + jnp.where(vrow == MLA_V, 1.0, 0.0)).astype(BF16)
    for kt in range(tm // KEY_TILE):
        vt_out[0, kt] = vt[:, kt * KEY_TILE:(kt + 1) * KEY_TILE]


def _mla_weights(w_uq, w_ukv):
    half = MLA_ROPE // 2
    wq = w_uq.reshape(MLA_Q_RANK, MLA_HEADS, MLA_QK)
    zq = jnp.zeros((MLA_Q_RANK, MLA_HEADS, HEAD_PAD - MLA_QK), F32)
    wq_plain_t = jnp.concatenate([wq, zq], -1).reshape(MLA_Q_RANK, -1).T.astype(BF16)
    t1 = wq[:, :, MLA_NOPE:MLA_NOPE + half]
    t2 = wq[:, :, MLA_NOPE + half:]
    wq_rot_t = jnp.concatenate([-t2, t1], -1).reshape(MLA_Q_RANK, -1).T.astype(BF16)

    wkv = w_ukv.reshape(MLA_KV_RANK, MLA_HEADS, MLA_NOPE + MLA_V)
    zk = jnp.zeros((MLA_KV_RANK, MLA_HEADS, HEAD_PAD - MLA_NOPE), F32)
    wk = jnp.concatenate([wkv[:, :, :MLA_NOPE], zk], -1).reshape(MLA_KV_RANK, -1).astype(BF16)
    zv = jnp.zeros((MLA_KV_RANK, MLA_HEADS, V_ROWS - MLA_V), F32)
    wvt = jnp.concatenate([wkv[:, :, MLA_NOPE:], zv], -1).reshape(MLA_KV_RANK, -1).T.astype(BF16)
    return wq_plain_t, wq_rot_t, wk, wvt


ATTN_SUBTILES = 2
ATTN_PAIRS_PER_ITER = 2


def _attn_kernel(qt_ref, k_ref, vt_ref, o_ref, s_buf):
    ts = KEY_TILE
    nsub = ATTN_SUBTILES
    n_qt = o_ref.shape[1] // (nsub * ts)
    key_chunk = lax.broadcasted_iota(jnp.int32, (ts, ts), 0) // ATTN_CHUNK
    qry_chunk = lax.broadcasted_iota(jnp.int32, (ts, ts), 1) // ATTN_CHUNK
    diag_mask = key_chunk <= qry_chunk
    lanes = [slice(hh * HEAD_PAD, (hh + 1) * HEAD_PAD) for hh in range(2)]
    chains = [(hh, sb) for hh in range(2) for sb in range(nsub)]

    def scores(i, j, slot, first_sub, masked_sub):
        if isinstance(j, int):
            start = j * ts
        else:
            start = pl.multiple_of(j * ts, ts)
        kj = [k_ref[0, pl.ds(start, ts), lanes[hh]] for hh in range(2)]
        tile_max = {}
        for n, ch in enumerate(chains):
            if ch[1] >= first_sub:
                q_t = qt_ref[0, i * nsub + ch[1], lanes[ch[0]], :]
                s = jnp.dot(kj[ch[0]], q_t, preferred_element_type=F32)
                if ch[1] == masked_sub:
                    s = jnp.where(diag_mask, s, NEG_BIG)
                s_buf[slot, n] = s
                tile_max[ch] = jnp.max(s, axis=0, keepdims=True)
        return tile_max

    def consume(j, slot, carry, tile_max, first_sub):
        vj = [vt_ref[0, j, hh * V_ROWS:(hh + 1) * V_ROWS, :] for hh in range(2)]
        active = [(n, ch) for n, ch in enumerate(chains) if ch[1] >= first_sub]
        m_new = {ch: jnp.maximum(carry[ch][0], tile_max[ch]) for _, ch in active}
        alpha = {ch: jnp.exp2(carry[ch][0] - m_new[ch]) for _, ch in active}
        out = dict(carry)
        for n, ch in active:
            p = jnp.exp2(s_buf[slot, n] - m_new[ch]).astype(BF16)
            pv = jnp.dot(vj[ch[0]], p, preferred_element_type=F32)
            out[ch] = (m_new[ch], carry[ch][1] * alpha[ch] + pv)
        return out

    head_max = scores(0, 0, 0, 0, 0)
    for i in range(n_qt):
        sa, sb_ = (0, 1) if i % 2 == 0 else (2, 3)
        next_sa = 2 if i % 2 == 0 else 0

        def pair_step(t, carry, max_a, on_diag, i=i, sa=sa, sb_=sb_):
            max_b = scores(i, 2 * t + 1, sb_, 0, -1)
            carry = consume(2 * t, sa, carry, max_a, 0)
            max_a = scores(i, 2 * t + 2, sa, 0, 0 if on_diag else -1)
            return consume(2 * t + 1, sb_, carry, max_b, 0), max_a

        def loop_body(t, flat):
            carry = {ch: (flat[3 * n], flat[3 * n + 1]) for n, ch in enumerate(chains)}
            max_a = {ch: flat[3 * n + 2] for n, ch in enumerate(chains)}
            for u in range(ATTN_PAIRS_PER_ITER):
                carry, max_a = pair_step(ATTN_PAIRS_PER_ITER * t + u, carry, max_a, False)
            return tuple(v for ch in chains for v in (*carry[ch], max_a[ch]))

        n_iter, n_rest = divmod(max(i - 1, 0), ATTN_PAIRS_PER_ITER)
        init = tuple(v for ch in chains
                     for v in (jnp.full((1, ts), NEG_BIG, F32), jnp.zeros((V_ROWS, ts), F32), head_max[ch]))
        flat = lax.fori_loop(0, n_iter, loop_body, init)
        carry = {ch: (flat[3 * n], flat[3 * n + 1]) for n, ch in enumerate(chains)}
        max_a = {ch: flat[3 * n + 2] for n, ch in enumerate(chains)}
        for u in range(n_rest):
            carry, max_a = pair_step(ATTN_PAIRS_PER_ITER * n_iter + u, carry, max_a, False)
        if i >= 1:
            carry, max_a = pair_step(i - 1, carry, max_a, True)
        max_b = scores(i, 2 * i + 1, sb_, 1, 1)
        if i + 1 < n_qt:
            head_max = scores(i + 1, 0, next_sa, 0, -1)
        carry = consume(2 * i, sa, carry, max_a, 0)
        carry = consume(2 * i + 1, sb_, carry, max_b, 1)

        for sb in range(nsub):
            normed = []
            for hh in range(2):
                acc = carry[(hh, sb)][1]
                normed.append(acc[:MLA_V] / acc[MLA_V:MLA_V + 1])
            out_t = jnp.concatenate(normed, axis=0)
            row0 = (i * nsub + sb) * ts
            o_ref[0, row0:row0 + ts, :] = out_t.T.astype(o_ref.dtype)


def _attention(qt, k, vt):
    b, s, _ = k.shape
    pairs = MLA_HEADS // 2
    nsub = ATTN_SUBTILES
    return pl.pallas_call(
        _attn_kernel,
        grid=(b, pairs),
        in_specs=[pl.BlockSpec((1, s // KEY_TILE, 2 * HEAD_PAD, KEY_TILE), lambda bi, g: (bi, 0, g, 0)),
                  pl.BlockSpec((1, s, 2 * HEAD_PAD), lambda bi, g: (bi, 0, g)),
                  pl.BlockSpec((1, s // KEY_TILE, 2 * V_ROWS, KEY_TILE), lambda bi, g: (bi, 0, g, 0))],
        out_specs=pl.BlockSpec((1, s, 2 * MLA_V), lambda bi, g: (bi, 0, g)),
        out_shape=jax.ShapeDtypeStruct((b, s, MLA_WIDTH), BF16),
        scratch_shapes=[pltpu.VMEM((4, 2 * nsub, KEY_TILE, KEY_TILE), F32)],
        compiler_params=_params("parallel", "parallel"),
    )(qt, k, vt)


def _lane_head(width):
    return lax.broadcasted_iota(jnp.int32, (1, width), 1) // RWKV_HEAD


def _bd_stack(x):
    head = _lane_head(x.shape[1])
    xb = x.astype(BF16)
    return jnp.concatenate([jnp.where(head == h, xb, jnp.zeros_like(xb)) for h in range(PACK_HEADS)], axis=0)


def _pdot(a, b):
    return jnp.dot(a.astype(BF16), _bd_stack(b), preferred_element_type=F32)


def _pdot2(a, b1, b2):
    rhs = jnp.concatenate([_bd_stack(b1), _bd_stack(b2)], axis=1)
    out = jnp.dot(a.astype(BF16), rhs, preferred_element_type=F32)
    return out[:, :PACK_LANES], out[:, PACK_LANES:]


def _pdot_nt(a, b):
    return lax.dot_general(a.astype(BF16), _bd_stack(b), (((1,), (1,)), ((), ())),
                           preferred_element_type=F32)


def _split3(x):
    x1 = x.astype(BF16)
    r1 = x - x1.astype(F32)
    x2 = r1.astype(BF16)
    x3 = (r1 - x2.astype(F32)).astype(BF16)
    return x1, x2, x3


def _wkv_kernel(rw_ref, prev_ref, mu_ref, wlora_ref, w0_ref, a0_ref, kk_ref, ka_ref, rk_ref,
                ones_ref, tri_ref, y_ref, bonus_ref, s_ref, rhat_s, y0_s, p_s, q_s, *, tm, nt, n_tiles):
    g = pl.program_id(0)
    c = WKV_CHUNK
    w = RWKV_WIDTH
    nch = tm // c
    packs = range(w // PACK_LANES)
    lanes_of = lambda gk: slice(gk * PACK_LANES, (gk + 1) * PACK_LANES)

    @pl.when(g == 0)
    def _():
        for ref in (s_ref, rhat_s, y0_s, p_s, q_s):
            ref[...] = jnp.zeros_like(ref)

    scan_starts_sequence = (jnp.maximum(g - 1, 0) % nt) == 0
    scan = {"state": {gk: jnp.where(scan_starts_sequence, 0.0, s_ref[:, lanes_of(gk)]) for gk in packs},
            "chunk": 0}

    def advance_scan():
        cc = scan["chunk"]
        if cc >= nch:
            return
        state = scan["state"]
        rs = slice(cc * c, (cc + 1) * c)
        for gk in packs:
            ls = lanes_of(gk)
            y_ref[0, rs, ls] = (_pdot_nt(rhat_s[rs, ls], state[gk]) + y0_s[rs, ls]).astype(y_ref.dtype)
        state = {gk: jnp.dot(state[gk].astype(BF16), p_s[cc * PACK_LANES:(cc + 1) * PACK_LANES, lanes_of(gk)],
                             preferred_element_type=F32) + q_s[rs, lanes_of(gk)] for gk in packs}
        scan["state"] = state
        scan["chunk"] = cc + 1
        if cc + 1 == nch:
            for gk in packs:
                s_ref[:, lanes_of(gk)] = state[gk]

    i = jnp.minimum(g, n_tiles - 1) % nt
    u_raw = rw_ref[0].astype(F32)
    prev_row = prev_ref[0, 7:8, :].astype(F32) * (i > 0).astype(F32)
    row = lax.broadcasted_iota(jnp.int32, (tm, 1), 0)
    u_prev = jnp.where(row == 0, prev_row, pltpu.roll(u_raw, 1, axis=0))
    u = u_raw + (u_prev - u_raw) * mu_ref[...]
    r = u[:, :w]
    k = u[:, w:2 * w]
    v = u[:, 2 * w:3 * w]
    lora_in = u[:, 3 * w:]
    lane = lax.broadcasted_iota(jnp.int32, (1, DECAY_LORA + ICLR_LORA), 1)
    lora_in = jnp.where(lane < DECAY_LORA, jnp.tanh(lora_in), lora_in)
    lora = jnp.dot(lora_in.astype(BF16), wlora_ref[...], preferred_element_type=F32)
    c_lw = -0.5 * math.exp(-0.5)
    lw = c_lw + c_lw * jnp.tanh(w0_ref[...] + lora[:, :w])
    a = 0.5 + 0.5 * jnp.tanh(a0_ref[...] + lora[:, w:])

    ones_bd = ones_ref[...]
    headsum = lambda t: jnp.dot(t.astype(BF16), ones_bd, preferred_element_type=F32)
    kk = k * kk_ref[...]
    kk = kk * lax.rsqrt(jnp.maximum(headsum(kk * kk), 1e-24))
    k = k * (1.0 + (a - 1.0) * ka_ref[...])
    bonus_ref[0] = (headsum(r * k * rk_ref[...]) * v).astype(bonus_ref.dtype)
    a_vec = -kk
    b_vec = kk * a

    tri = tri_ref[...]
    cum = sum(jnp.dot(tri, part, preferred_element_type=F32) for part in _split3(lw))

    t_idx = lax.broadcasted_iota(jnp.int32, (c, PACK_LANES), 0)
    s_idx = lax.broadcasted_iota(jnp.int32, (c, PACK_LANES), 1) % RWKV_HEAD
    strict = s_idx < t_idx
    incl = s_idx <= t_idx
    eye = (s_idx == t_idx).astype(F32)
    prow = lax.broadcasted_iota(jnp.int32, (PACK_LANES, PACK_LANES), 0)
    pcol = lax.broadcasted_iota(jnp.int32, (PACK_LANES, PACK_LANES), 1)
    same_head = (prow // RWKV_HEAD) == (pcol // RWKV_HEAD)
    on_diag = prow == pcol
    head = _lane_head(PACK_LANES)

    cum_last = jnp.concatenate(
        [jnp.broadcast_to(cum[(cc + 1) * c - 1:(cc + 1) * c, :], (c, w)) for cc in range(nch)], axis=0)
    e_pos = jnp.exp(cum)
    e_neg = jnp.exp(-cum)
    e_last = jnp.exp(cum_last - cum)
    at_all = a_vec * jnp.exp(cum - lw)
    bt_all = b_vec * e_neg
    kt_all = k * e_neg
    rt_all = r * e_pos
    bl_all = b_vec * e_last
    kl_all = k * e_last
    w_end_all = jnp.exp(cum_last)

    probs = [(cc, gk) for cc in range(nch) for gk in packs]
    sl = {pr: (slice(pr[0] * c, (pr[0] + 1) * c), slice(pr[1] * PACK_LANES, (pr[1] + 1) * PACK_LANES))
          for pr in probs}
    at = {pr: at_all[sl[pr]] for pr in probs}
    rt = {pr: rt_all[sl[pr]] for pr in probs}
    vv = {pr: v[sl[pr]] for pr in probs}
    m_bk = {pr: lax.dot_general(
        jnp.concatenate([at[pr], rt[pr]], axis=0).astype(BF16),
        jnp.concatenate([_bd_stack(bt_all[sl[pr]]), _bd_stack(kt_all[sl[pr]])], axis=0),
        (((1,), (1,)), ((), ())), preferred_element_type=F32) for pr in probs}
    advance_scan()
    m_b = {pr: m_bk[pr][:, :PACK_LANES] for pr in probs}
    m_k = {pr: m_bk[pr][:, PACK_LANES:] for pr in probs}
    l_ab = {pr: jnp.where(strict, m_b[pr][:c], 0.0) for pr in probs}
    t_inv = {pr: eye + l_ab[pr] for pr in probs}
    x = {pr: _pdot(l_ab[pr], l_ab[pr]) for pr in probs}
    levels = int(math.log2(c)) - 1
    for lvl in range(levels):
        advance_scan()
        if lvl < levels - 1:
            tx = {pr: _pdot2(x[pr], t_inv[pr], x[pr]) for pr in probs}
            t_inv = {pr: t_inv[pr] + tx[pr][0] for pr in probs}
            x = {pr: tx[pr][1] for pr in probs}
        else:
            t_inv = {pr: t_inv[pr] + _pdot(x[pr], t_inv[pr]) for pr in probs}
    akv = {pr: _pdot(jnp.where(strict, m_k[pr][:c], 0.0), vv[pr]) for pr in probs}
    au = {pr: _pdot2(t_inv[pr], at[pr], akv[pr]) for pr in probs}
    a_hat = {pr: au[pr][0] for pr in probs}
    u0 = {pr: au[pr][1] for pr in probs}
    while scan["chunk"] < nch:
        advance_scan()
    for pr in probs:
        rs, ls = sl[pr]
        m_rb = jnp.where(incl, m_b[pr][c:], 0.0)
        m_rk = jnp.where(incl, m_k[pr][c:], 0.0)
        r_add, y_add = _pdot2(m_rb, a_hat[pr], u0[pr])
        rhat_s[rs, ls] = (rt[pr] + r_add).astype(rhat_s.dtype)
        y0_s[rs, ls] = y_add + _pdot(m_rk, vv[pr])
    for pr in probs:
        rs, ls = sl[pr]
        bl = bl_all[sl[pr]]
        p_full = jnp.dot(a_hat[pr].T.astype(BF16), bl.astype(BF16), preferred_element_type=F32)
        w_end = w_end_all[rs.start:rs.start + 1, ls]
        p_bd = jnp.where(same_head, p_full, 0.0) + jnp.where(on_diag, w_end, 0.0)
        p_s[pr[0] * PACK_LANES:(pr[0] + 1) * PACK_LANES, ls] = p_bd.astype(p_s.dtype)
        uv_t = jnp.concatenate([u0[pr], vv[pr]], axis=0).T
        bk = jnp.concatenate([bl, kl_all[sl[pr]]], axis=0).astype(BF16)
        f = jnp.dot(uv_t.astype(BF16), bk, preferred_element_type=F32)
        q_s[rs, ls] = sum(jnp.where(head == h, f[h * RWKV_HEAD:(h + 1) * RWKV_HEAD], 0.0)
                          for h in range(PACK_HEADS))


def _wkv(rw, mu, w0, w_decay_up, a0, w_iclr_up, k_k, k_a, r_k, tm):
    b, s, width = rw.shape
    w = RWKV_WIDTH
    nt = s // tm
    n_tiles = b * nt
    wlora = (0.5 * jnp.concatenate(
        [jnp.concatenate([w_decay_up, jnp.zeros((DECAY_LORA, w), F32)], 1),
         jnp.concatenate([jnp.zeros((ICLR_LORA, w), F32), w_iclr_up], 1)], 0)).astype(BF16)
    w0 = 0.5 * w0
    a0 = 0.5 * a0
    hid = np.arange(w) // RWKV_HEAD
    ones_bd = jnp.asarray((hid[:, None] == hid[None, :]).astype(np.float32), BF16)
    tid = np.arange(tm)
    tri = jnp.asarray(((tid[:, None] >= tid[None, :]) &
                       (tid[:, None] // WKV_CHUNK == tid[None, :] // WKV_CHUNK)).astype(np.float32), BF16)
    row = lambda t: t.reshape(1, -1)
    prep_tile = lambda g: jnp.minimum(g, n_tiles - 1)
    scan_tile = lambda g: jnp.maximum(g - 1, 0)
    rw_spec = pl.BlockSpec((1, tm, width), lambda g: (prep_tile(g) // nt, prep_tile(g) % nt, 0))
    prev = pl.BlockSpec((1, 8, width),
                        lambda g: (prep_tile(g) // nt, jnp.maximum((prep_tile(g) % nt) * (tm // 8) - 1, 0), 0))
    y_spec = pl.BlockSpec((1, tm, w), lambda g: (scan_tile(g) // nt, scan_tile(g) % nt, 0))
    bonus_spec = pl.BlockSpec((1, tm, w), lambda g: (prep_tile(g) // nt, prep_tile(g) % nt, 0))
    n_chunks = tm // WKV_CHUNK
    return pl.pallas_call(
        functools.partial(_wkv_kernel, tm=tm, nt=nt, n_tiles=n_tiles),
        grid=(n_tiles + 1,),
        in_specs=[rw_spec, prev, _const_spec((1, width)), _const_spec(wlora.shape)]
                 + [_const_spec((1, w))] * 5 + [_const_spec(ones_bd.shape), _const_spec(tri.shape)],
        out_specs=[y_spec, bonus_spec],
        out_shape=[jax.ShapeDtypeStruct((b, s, w), BF16),
                   jax.ShapeDtypeStruct((b, s, w), BF16)],
        scratch_shapes=[pltpu.VMEM((RWKV_HEAD, w), F32),
                        pltpu.VMEM((tm, w), BF16),
                        pltpu.VMEM((tm, w), F32),
                        pltpu.VMEM((n_chunks * PACK_LANES, w), BF16),
                        pltpu.VMEM((tm, w), F32)],
        compiler_params=_params("arbitrary"),
    )(rw, rw, row(mu), wlora, row(w0), row(a0), row(k_k), row(k_a), row(r_k), ones_bd, tri)


def _epilogue_kernel(x_ref, gate_ref, attn_ref, ga_ref, y_ref, bonus_ref, gb_ref, ma_ref, mb_ref,
                     ones_ref, gng_ref, gnb_ref, wpa_ref, wpb_ref, wout_ref, pg_ref, pb_ref, o_ref):
    silu_half = lambda th: th + th * jnp.tanh(th)
    ones_bd = ones_ref[...]
    headmean = lambda t: jnp.dot(t.astype(BF16), ones_bd, preferred_element_type=F32) * (1.0 / RWKV_HEAD)

    y = y_ref[0].astype(F32)
    yc = y - headmean(y)
    yn = yc * lax.rsqrt(headmean(yc * yc) + GN_EPS)
    yb = yn * gng_ref[...] + gnb_ref[...] + bonus_ref[0].astype(F32)
    yb = yb * silu_half(gb_ref[0].astype(F32))
    ya = attn_ref[0].astype(F32) * silu_half(ga_ref[0].astype(F32))
    ya_p = jnp.dot(ya.astype(BF16), wpa_ref[...], preferred_element_type=F32)
    yb_p = jnp.dot(yb.astype(BF16), wpb_ref[...], preferred_element_type=F32)
    merged2 = ((1.0 + jnp.tanh(ma_ref[0].astype(F32))) * ya_p
               + (1.0 + jnp.tanh(mb_ref[0].astype(F32))) * yb_p)
    sub2 = jnp.dot(merged2.astype(BF16), wout_ref[...], preferred_element_type=F32)
    z = ALPHA * x_ref[0] + (0.5 * (1.0 + gate_ref[0])) * sub2
    zc = z - jnp.mean(z, -1, keepdims=True)
    zn = zc * lax.rsqrt(jnp.mean(zc * zc, -1, keepdims=True) + LN_EPS)
    o_ref[0] = (zn * pg_ref[...] + pb_ref[...]).astype(o_ref.dtype)


def _epilogue(x, gate, attn, ga, y, bonus, gb, ma, mb, gn_g, gn_b, w_proj_a, w_proj_b, w_out,
              post_g, post_b, tm):
    b, s, d = x.shape
    w = RWKV_WIDTH
    hid = np.arange(w) // RWKV_HEAD
    ones_bd = jnp.asarray((hid[:, None] == hid[None, :]).astype(np.float32), BF16)
    row = lambda t: t.reshape(1, -1)
    tok = lambda wd: pl.BlockSpec((1, tm, wd), lambda bi, i: (bi, i, 0))
    vec = pl.BlockSpec((1, 1, d), lambda bi, i: (bi, 0, 0))
    return pl.pallas_call(
        _epilogue_kernel,
        grid=(b, s // tm),
        in_specs=[tok(d), vec, tok(w), tok(w), tok(w), tok(w), tok(w), tok(d), tok(d),
                  _const_spec((w, w)), _const_spec((1, w)), _const_spec((1, w)),
                  _const_spec((w, d)), _const_spec((w, d)), _const_spec((d, d)),
                  _const_spec((1, d)), _const_spec((1, d))],
        out_specs=tok(d),
        out_shape=jax.ShapeDtypeStruct((b, s, d), x.dtype),
        compiler_params=_params("parallel", "parallel"),
    )(x, gate, attn, ga, y, bonus, gb, ma, mb, ones_bd, row(gn_g), row(gn_b),
      w_proj_a.astype(BF16), w_proj_b.astype(BF16), w_out.astype(BF16), row(post_g), row(post_b))


def _layer(x, c, pos, w_ada, b_ada, w_in, q_norm_g, w_uq, kv_norm_g, w_ukv, mu_rwkv, w0, w_decay_up,
           a0, w_iclr_up, k_k, k_a, r_k, gn_g, gn_b, w_proj_a, w_proj_b, w_out, post_g, post_b):
    b, s, d = x.shape
    tm = min(512, s)
    ada = _adaln(c, w_ada, b_ada)
    shift, scale, gate = (ada[:, j * d:(j + 1) * d].reshape(b, 1, d) for j in range(3))
    qt, k, vt, ga, rw, gb, ma, mb = _ln_proj(x, scale, shift, w_in.astype(BF16), pos, q_norm_g, kv_norm_g,
                                             w_uq, w_ukv, tm)
    attn = _attention(qt, k, vt)
    y, bonus = _wkv(rw, mu_rwkv, w0, w_decay_up, a0, w_iclr_up, k_k, k_a, r_k.reshape(-1), min(256, s))
    return _epilogue(x, gate, attn, ga, y, bonus, gb, ma, mb, gn_g, gn_b, w_proj_a, w_proj_b, w_out,
                     post_g, post_b, tm)


def kernel(x, c, positions, w_ada, b_ada, w_in, q_norm_g, w_uq, kv_norm_g, w_ukv, mu_rwkv, w0,
           w_decay_up, a0, w_iclr_up, k_k, k_a, r_k, gn_g, gn_b, w_proj_a, w_proj_b, w_out, post_g,
           post_b):
    pos = positions.astype(F32)[:, None, :]
    for l in range(w_ada.shape[0]):
        x = _layer(x, c, pos, w_ada[l], b_ada[l], w_in[l], q_norm_g[l], w_uq[l], kv_norm_g[l],
                   w_ukv[l], mu_rwkv[l], w0[l], w_decay_up[l], a0[l], w_iclr_up[l], k_k[l], k_a[l],
                   r_k[l], gn_g[l], gn_b[l], w_proj_a[l], w_proj_b[l], w_out[l], post_g[l], post_b[l])
    return x
```

```python
import functools
import math

import jax
import jax.numpy as jnp
import numpy as np
from jax import lax
from jax.experimental import pallas as pl
from jax.experimental.pallas import tpu as pltpu

F32 = jnp.float32
BF16 = jnp.bfloat16

D_MODEL = 1024
LN_EPS = 1e-5
RMS_EPS = 1e-6
GN_EPS = 64e-5

MLA_HEADS = 8
MLA_NOPE = 64
MLA_ROPE = 32
MLA_V = 64
MLA_QK = MLA_NOPE + MLA_ROPE
MLA_Q_RANK = 256
MLA_KV_RANK = 128
MLA_WIDTH = MLA_HEADS * MLA_V
ROPE_THETA = 10000.0
ATTN_CHUNK = 64

RWKV_HEADS = 8
RWKV_HEAD = 64
RWKV_WIDTH = RWKV_HEADS * RWKV_HEAD
DECAY_LORA = 64
ICLR_LORA = 64
RWKV_SHIFT_WIDTH = 3 * RWKV_WIDTH + DECAY_LORA + ICLR_LORA
WKV_CHUNK = 64

DEPTH = 1
ALPHA = (2.0 * DEPTH) ** 0.25

LANES = 128
HEAD_PAD = 128
V_ROWS = 80
KEY_TILE = 256
PACK_HEADS = 2
PACK_LANES = PACK_HEADS * RWKV_HEAD

QKR_WIDTH = 512
VMEM_LIMIT = 56 * 1024 * 1024

NEG_BIG = -1e30


def _const_spec(shape):
    n = len(shape)
    return pl.BlockSpec(shape, lambda *_: (0,) * n)


def _params(*sem):
    return pltpu.CompilerParams(dimension_semantics=sem, vmem_limit_bytes=VMEM_LIMIT)


def _adaln_kernel(c_ref, w_ref, b_ref, o_ref):
    c = c_ref[...]
    sc = c * jax.nn.sigmoid(c)
    o_ref[...] = jnp.dot(sc, w_ref[...], preferred_element_type=F32) + b_ref[...]


def _adaln(c, w_ada, b_ada):
    b = c.shape[0]
    return pl.pallas_call(
        _adaln_kernel,
        out_shape=jax.ShapeDtypeStruct((b, 3 * D_MODEL), F32),
        compiler_params=pltpu.CompilerParams(vmem_limit_bytes=VMEM_LIMIT),
    )(c, w_ada, b_ada.reshape(1, -1))


_PROJ_WIDTHS = (MLA_WIDTH, RWKV_SHIFT_WIDTH, RWKV_WIDTH, D_MODEL, D_MODEL)
_PROJ_SCALES = (0.5, 1.0, 0.5, 0.5, 0.5)


def _ln_proj_kernel(x_ref, scale_ref, shift_ref, w_in_ref, pos_ref, inv_ref, qg_ref, kvg_ref, wqp_ref,
                    wqr_ref, wk_ref, wvt_ref, qt_out, k_out, vt_out, *refs):
    out_refs, w_ref = refs[:-1], refs[-1]

    @pl.when(jnp.logical_and(pl.program_id(0) == 0, pl.program_id(1) == 0))
    def _():
        used = MLA_Q_RANK + MLA_KV_RANK + MLA_ROPE
        shift = QKR_WIDTH - used
        lane = lax.broadcasted_iota(jnp.int32, (1, LANES), 1)
        rows = 256
        for r in range(0, D_MODEL, rows):
            rs = slice(r, r + rows)
            aligned = (used // LANES) * LANES
            w_ref[rs, :aligned] = w_in_ref[rs, :aligned]
            edge = w_in_ref[rs, aligned:aligned + LANES]
            w_ref[rs, aligned:QKR_WIDTH] = jnp.where(lane < used - aligned, edge, jnp.zeros_like(edge))
            for c0 in range(QKR_WIDTH, w_ref.shape[1], LANES):
                w_ref[rs, c0:c0 + LANES] = w_in_ref[rs, c0 - shift:c0 - shift + LANES]

    x = x_ref[0]
    xc = x - jnp.mean(x, -1, keepdims=True)
    h = xc * lax.rsqrt(jnp.mean(xc * xc, -1, keepdims=True) + LN_EPS)
    hb = (h * (1.0 + scale_ref[0]) + shift_ref[0]).astype(BF16)
    latent = jnp.dot(hb, w_ref[:, :QKR_WIDTH], preferred_element_type=F32)
    off = QKR_WIDTH
    for n, (o_ref, width, scale) in enumerate(zip(out_refs, _PROJ_WIDTHS, _PROJ_SCALES)):
        acc = jnp.dot(hb, w_ref[:, off:off + width], preferred_element_type=F32)
        if scale != 1.0:
            acc = acc * scale
        o_ref[0] = acc.astype(o_ref.dtype)
        off += width
        if n == 1:
            _mla_prep(latent, pos_ref[0], inv_ref[...], qg_ref[...], kvg_ref[...], wqp_ref, wqr_ref,
                      wk_ref, wvt_ref, qt_out, k_out, vt_out)


def _ln_proj(x, scale, shift, w_in, pos, q_norm_g, kv_norm_g, w_uq, w_ukv, tm):
    b, s, d = x.shape
    wqp, wqr, wk, wvt = _mla_weights(w_uq, w_ukv)
    half = MLA_ROPE // 2
    inv = (ROPE_THETA ** (-jnp.arange(0, MLA_ROPE, 2, dtype=F32) / MLA_ROPE)).reshape(half, 1)
    width = MLA_HEADS * HEAD_PAD
    vrows = MLA_HEADS * V_ROWS
    tok = lambda w: pl.BlockSpec((1, tm, w), lambda bi, i: (bi, i, 0))
    vec = pl.BlockSpec((1, 1, d), lambda bi, i: (bi, 0, 0))
    tiles = lambda rows: pl.BlockSpec((1, tm // KEY_TILE, rows, KEY_TILE), lambda bi, i: (bi, i, 0, 0))
    return pl.pallas_call(
        _ln_proj_kernel,
        grid=(b, s // tm),
        in_specs=[tok(d), vec, vec,
                  pl.BlockSpec(w_in.shape, lambda bi, i: (0, 0), pipeline_mode=pl.Buffered(1)),
                  pl.BlockSpec((1, 1, tm), lambda bi, i: (bi, 0, i)),
                  _const_spec((half, 1)), _const_spec((1, MLA_Q_RANK)), _const_spec((1, MLA_KV_RANK)),
                  _const_spec(wqp.shape), _const_spec(wqr.shape), _const_spec(wk.shape),
                  _const_spec(wvt.shape)],
        out_specs=[tiles(width), tok(width), tiles(vrows)] + [tok(w) for w in _PROJ_WIDTHS],
        out_shape=[jax.ShapeDtypeStruct((b, s // KEY_TILE, width, KEY_TILE), BF16),
                   jax.ShapeDtypeStruct((b, s, width), BF16),
                   jax.ShapeDtypeStruct((b, s // KEY_TILE, vrows, KEY_TILE), BF16)]
                  + [jax.ShapeDtypeStruct((b, s, w), BF16) for w in _PROJ_WIDTHS],
        scratch_shapes=[pltpu.VMEM((d, QKR_WIDTH + sum(_PROJ_WIDTHS)), BF16)],
        compiler_params=_params("arbitrary", "arbitrary"),
    )(x, scale, shift, w_in, pos, inv, q_norm_g.reshape(1, -1), kv_norm_g.reshape(1, -1),
      wqp, wqr, wk, wvt)


def _mla_prep(t, pos, inv, qg, kvg, wqp_ref, wqr_ref, wk_ref, wvt_ref, qt_out, k_out, vt_out):
    tm = t.shape[0]
    half = MLA_ROPE // 2
    qc = t[:, :MLA_Q_RANK]
    kvc = t[:, MLA_Q_RANK:MLA_Q_RANK + MLA_KV_RANK]
    kr = t[:, MLA_Q_RANK + MLA_KV_RANK:]
    qn = qc * lax.rsqrt(jnp.mean(qc * qc, -1, keepdims=True) + RMS_EPS) * qg
    kvn = kvc * lax.rsqrt(jnp.mean(kvc * kvc, -1, keepdims=True) + RMS_EPS) * kvg
    qn_t = qn.T.astype(BF16)
    kvn_t = kvn.T.astype(BF16)
    kr_t = kr.T

    ang = inv * pos
    cos_h = jnp.cos(ang)
    sin_h = jnp.sin(ang)
    cos_r = jnp.concatenate([cos_h, cos_h], axis=0)
    sin_r = jnp.concatenate([sin_h, sin_h], axis=0)

    plain_t = jnp.dot(wqp_ref[...], qn_t, preferred_element_type=F32)
    rot_t = jnp.dot(wqr_ref[...], qn_t, preferred_element_type=F32)
    scale = MLA_QK ** -0.5 * math.log2(math.e)
    zpad = jnp.zeros((HEAD_PAD - MLA_QK, tm), BF16)
    for h in range(MLA_HEADS):
        base = h * HEAD_PAD
        nope = (plain_t[base:base + MLA_NOPE] * scale).astype(BF16)
        pe = ((plain_t[base + MLA_NOPE:base + MLA_QK] * cos_r
               + rot_t[h * MLA_ROPE:(h + 1) * MLA_ROPE] * sin_r) * scale).astype(BF16)
        q_t = jnp.concatenate([nope, pe, zpad], axis=0)
        for kt in range(tm // KEY_TILE):
            qt_out[0, kt, base:base + HEAD_PAD, :] = q_t[:, kt * KEY_TILE:(kt + 1) * KEY_TILE]

    k1 = kr_t[:half]
    k2 = kr_t[half:MLA_ROPE]
    kpe_t = jnp.concatenate([jnp.zeros((MLA_NOPE, tm), F32),
                             k1 * cos_h - k2 * sin_h, k1 * sin_h + k2 * cos_h,
                             jnp.zeros((HEAD_PAD - MLA_QK, tm), F32)], axis=0)
    kpe = kpe_t.T
    ka = jnp.dot(kvn.astype(BF16), wk_ref[...], preferred_element_type=F32)
    for h in range(MLA_HEADS):
        sl = slice(h * HEAD_PAD, (h + 1) * HEAD_PAD)
        k_out[0, :, sl] = (ka[:, sl] + kpe).astype(BF16)

    vt = jnp.dot(wvt_ref[...], kvn_t, preferred_element_type=F32)
    vrow = lax.broadcasted_iota(jnp.int32, (MLA_HEADS * V_ROWS, 1), 0) % V_ROWS
    vt = (vt + jnp.where(vrow == MLA_V, 1.0, 0.0)).astype(BF16)
    for kt in range(tm // KEY_TILE):
        vt_out[0, kt] = vt[:, kt * KEY_TILE:(kt + 1) * KEY_TILE]


def _mla_weights(w_uq, w_ukv):
    half = MLA_ROPE // 2
    wq = w_uq.reshape(MLA_Q_RANK, MLA_HEADS, MLA_QK)
    zq = jnp.zeros((MLA_Q_RANK, MLA_HEADS, HEAD_PAD - MLA_QK), F32)
    wq_plain_t = jnp.concatenate([wq, zq], -1).reshape(MLA_Q_RANK, -1).T.astype(BF16)
    t1 = wq[:, :, MLA_NOPE:MLA_NOPE + half]
    t2 = wq[:, :, MLA_NOPE + half:]
    wq_rot_t = jnp.concatenate([-t2, t1], -1).reshape(MLA_Q_RANK, -1).T.astype(BF16)

    wkv = w_ukv.reshape(MLA_KV_RANK, MLA_HEADS, MLA_NOPE + MLA_V)
    zk = jnp.zeros((MLA_KV_RANK, MLA_HEADS, HEAD_PAD - MLA_NOPE), F32)
    wk = jnp.concatenate([wkv[:, :, :MLA_NOPE], zk], -1).reshape(MLA_KV_RANK, -1).astype(BF16)
    zv = jnp.zeros((MLA_KV_RANK, MLA_HEADS, V_ROWS - MLA_V), F32)
    wvt = jnp.concatenate([wkv[:, :, MLA_NOPE:], zv], -1).reshape(MLA_KV_RANK, -1).T.astype(BF16)
    return wq_plain_t, wq_rot_t, wk, wvt


ATTN_SUBTILES = 2
ATTN_PAIRS_PER_ITER = 2


def _attn_kernel(qt_ref, k_ref, vt_ref, o_ref, s_buf):
    ts = KEY_TILE
    nsub = ATTN_SUBTILES
    n_qt = o_ref.shape[1] // (nsub * ts)
    key_chunk = lax.broadcasted_iota(jnp.int32, (ts, ts), 0) // ATTN_CHUNK
    qry_chunk = lax.broadcasted_iota(jnp.int32, (ts, ts), 1) // ATTN_CHUNK
    diag_mask = key_chunk <= qry_chunk
    lanes = [slice(hh * HEAD_PAD, (hh + 1) * HEAD_PAD) for hh in range(2)]
    chains = [(hh, sb) for hh in range(2) for sb in range(nsub)]

    def scores(i, j, slot, first_sub, masked_sub):
        if isinstance(j, int):
            start = j * ts
        else:
            start = pl.multiple_of(j * ts, ts)
        kj = [k_ref[0, pl.ds(start, ts), lanes[hh]] for hh in range(2)]
        tile_max = {}
        for n, ch in enumerate(chains):
            if ch[1] >= first_sub:
                q_t = qt_ref[0, i * nsub + ch[1], lanes[ch[0]], :]
                s = jnp.dot(kj[ch[0]], q_t, preferred_element_type=F32)
                if ch[1] == masked_sub:
                    s = jnp.where(diag_mask, s, NEG_BIG)
                s_buf[slot, n] = s
                tile_max[ch] = jnp.max(s, axis=0, keepdims=True)
        return tile_max

    def consume(j, slot, carry, tile_max, first_sub):
        vj = [vt_ref[0, j, hh * V_ROWS:(hh + 1) * V_ROWS, :] for hh in range(2)]
        active = [(n, ch) for n, ch in enumerate(chains) if ch[1] >= first_sub]
        m_new = {ch: jnp.maximum(carry[ch][0], tile_max[ch]) for _, ch in active}
        alpha = {ch: jnp.exp2(carry[ch][0] - m_new[ch]) for _, ch in active}
        out = dict(carry)
        for n, ch in active:
            p = jnp.exp2(s_buf[slot, n] - m_new[ch]).astype(BF16)
            pv = jnp.dot(vj[ch[0]], p, preferred_element_type=F32)
            out[ch] = (m_new[ch], carry[ch][1] * alpha[ch] + pv)
        return out

    head_max = scores(0, 0, 0, 0, 0)
    for i in range(n_qt):
        sa, sb_ = (0, 1) if i % 2 == 0 else (2, 3)
        next_sa = 2 if i % 2 == 0 else 0

        def pair_step(t, carry, max_a, on_diag, i=i, sa=sa, sb_=sb_):
            max_b = scores(i, 2 * t + 1, sb_, 0, -1)
            carry = consume(2 * t, sa, carry, max_a, 0)
            max_a = scores(i, 2 * t + 2, sa, 0, 0 if on_diag else -1)
            return consume(2 * t + 1, sb_, carry, max_b, 0), max_a

        def loop_body(t, flat):
            carry = {ch: (flat[3 * n], flat[3 * n + 1]) for n, ch in enumerate(chains)}
            max_a = {ch: flat[3 * n + 2] for n, ch in enumerate(chains)}
            for u in range(ATTN_PAIRS_PER_ITER):
                carry, max_a = pair_step(ATTN_PAIRS_PER_ITER * t + u, carry, max_a, False)
            return tuple(v for ch in chains for v in (*carry[ch], max_a[ch]))

        n_iter, n_rest = divmod(max(i - 1, 0), ATTN_PAIRS_PER_ITER)
        init = tuple(v for ch in chains
                     for v in (jnp.full((1, ts), NEG_BIG, F32), jnp.zeros((V_ROWS, ts), F32), head_max[ch]))
        flat = lax.fori_loop(0, n_iter, loop_body, init)
        carry = {ch: (flat[3 * n], flat[3 * n + 1]) for n, ch in enumerate(chains)}
        max_a = {ch: flat[3 * n + 2] for n, ch in enumerate(chains)}
        for u in range(n_rest):
            carry, max_a = pair_step(ATTN_PAIRS_PER_ITER * n_iter + u, carry, max_a, False)
        if i >= 1:
            carry, max_a = pair_step(i - 1, carry, max_a, True)
        max_b = scores(i, 2 * i + 1, sb_, 1, 1)
        if i + 1 < n_qt:
            head_max = scores(i + 1, 0, next_sa, 0, -1)
        carry = consume(2 * i, sa, carry, max_a, 0)
        carry = consume(2 * i + 1, sb_, carry, max_b, 1)

        for sb in range(nsub):
            normed = []
            for hh in range(2):
                acc = carry[(hh, sb)][1]
                normed.append(acc[:MLA_V] / acc[MLA_V:MLA_V + 1])
            out_t = jnp.concatenate(normed, axis=0)
            row0 = (i * nsub + sb) * ts
            o_ref[0, row0:row0 + ts, :] = out_t.T.astype(o_ref.dtype)


def _attention(qt, k, vt):
    b, s, _ = k.shape
    pairs = MLA_HEADS // 2
    nsub = ATTN_SUBTILES
    return pl.pallas_call(
        _attn_kernel,
        grid=(b, pairs),
        in_specs=[pl.BlockSpec((1, s // KEY_TILE, 2 * HEAD_PAD, KEY_TILE), lambda bi, g: (bi, 0, g, 0)),
                  pl.BlockSpec((1, s, 2 * HEAD_PAD), lambda bi, g: (bi, 0, g)),
                  pl.BlockSpec((1, s // KEY_TILE, 2 * V_ROWS, KEY_TILE), lambda bi, g: (bi, 0, g, 0))],
        out_specs=pl.BlockSpec((1, s, 2 * MLA_V), lambda bi, g: (bi, 0, g)),
        out_shape=jax.ShapeDtypeStruct((b, s, MLA_WIDTH), BF16),
        scratch_shapes=[pltpu.VMEM((4, 2 * nsub, KEY_TILE, KEY_TILE), F32)],
        compiler_params=_params("parallel", "parallel"),
    )(qt, k, vt)


def _lane_head(width):
    return lax.broadcasted_iota(jnp.int32, (1, width), 1) // RWKV_HEAD


def _bd_stack(x):
    head = _lane_head(x.shape[1])
    xb = x.astype(BF16)
    return jnp.concatenate([jnp.where(head == h, xb, jnp.zeros_like(xb)) for h in range(PACK_HEADS)], axis=0)


def _pdot(a, b):
    return jnp.dot(a.astype(BF16), _bd_stack(b), preferred_element_type=F32)


def _pdot2(a, b1, b2):
    rhs = jnp.concatenate([_bd_stack(b1), _bd_stack(b2)], axis=1)
    out = jnp.dot(a.astype(BF16), rhs, preferred_element_type=F32)
    return out[:, :PACK_LANES], out[:, PACK_LANES:]


def _pdot_nt(a, b):
    return lax.dot_general(a.astype(BF16), _bd_stack(b), (((1,), (1,)), ((), ())),
                           preferred_element_type=F32)


def _split2(x):
    x1 = x.astype(BF16)
    x2 = (x - x1.astype(F32)).astype(BF16)
    return x1, x2


MXU_TILE = 256


def _head_ones():
    hid = np.arange(MXU_TILE) // RWKV_HEAD
    return jnp.asarray((hid[:, None] == hid[None, :]).astype(np.float32), BF16)


def _head_sums(t, ones_tile):
    tb = t.astype(BF16)
    return jnp.concatenate(
        [jnp.dot(tb[:, o:o + MXU_TILE], ones_tile, preferred_element_type=F32)
         for o in range(0, t.shape[1], MXU_TILE)], axis=1)


def _wkv_kernel(rw_ref, prev_ref, mu_ref, wlora_ref, w0_ref, a0_ref, kk_ref, ka_ref, rk_ref,
                ones_ref, tri_ref, y_ref, bonus_ref, s_ref, rhat_s, y0_s, p_s, q_s, *, tm, nt, n_tiles):
    g = pl.program_id(0)
    c = WKV_CHUNK
    w = RWKV_WIDTH
    nch = tm // c
    packs = range(w // PACK_LANES)
    lanes_of = lambda gk: slice(gk * PACK_LANES, (gk + 1) * PACK_LANES)

    @pl.when(g == 0)
    def _():
        for ref in (s_ref, rhat_s, y0_s, p_s, q_s):
            ref[...] = jnp.zeros_like(ref)

    scan_starts_sequence = (jnp.maximum(g - 1, 0) % nt) == 0
    scan = {"state": {gk: jnp.where(scan_starts_sequence, 0.0, s_ref[:, lanes_of(gk)]) for gk in packs},
            "chunk": 0}

    def advance_scan():
        cc = scan["chunk"]
        if cc >= nch:
            return
        state = scan["state"]
        rs = slice(cc * c, (cc + 1) * c)
        for gk in packs:
            ls = lanes_of(gk)
            y_ref[0, rs, ls] = (_pdot_nt(rhat_s[rs, ls], state[gk]) + y0_s[rs, ls]).astype(y_ref.dtype)
        state = {gk: jnp.dot(state[gk].astype(BF16), p_s[cc * PACK_LANES:(cc + 1) * PACK_LANES, lanes_of(gk)],
                             preferred_element_type=F32) + q_s[rs, lanes_of(gk)] for gk in packs}
        scan["state"] = state
        scan["chunk"] = cc + 1
        if cc + 1 == nch:
            for gk in packs:
                s_ref[:, lanes_of(gk)] = state[gk]

    i = jnp.minimum(g, n_tiles - 1) % nt
    u_raw = rw_ref[0].astype(F32)
    prev_row = prev_ref[0, 7:8, :].astype(F32) * (i > 0).astype(F32)
    row = lax.broadcasted_iota(jnp.int32, (tm, 1), 0)
    u_prev = jnp.where(row == 0, prev_row, pltpu.roll(u_raw, 1, axis=0))
    u = u_raw + (u_prev - u_raw) * mu_ref[...]
    r = u[:, :w]
    k = u[:, w:2 * w]
    v = u[:, 2 * w:3 * w]
    lora_in = u[:, 3 * w:]
    lane = lax.broadcasted_iota(jnp.int32, (1, DECAY_LORA + ICLR_LORA), 1)
    lora_in = jnp.where(lane < DECAY_LORA, jnp.tanh(lora_in), lora_in)
    lora = jnp.dot(lora_in.astype(BF16), wlora_ref[...], preferred_element_type=F32)
    c_lw = -0.5 * math.exp(-0.5) * math.log2(math.e)
    lw = c_lw + c_lw * jnp.tanh(w0_ref[...] + lora[:, :w])
    a = 0.5 + 0.5 * jnp.tanh(a0_ref[...] + lora[:, w:])

    ones_tile = ones_ref[...]
    headsum = lambda t: _head_sums(t, ones_tile)
    kk = k * kk_ref[...]
    kk = kk * lax.rsqrt(jnp.maximum(headsum(kk * kk), 1e-24))
    k = k * (1.0 + (a - 1.0) * ka_ref[...])
    bonus_ref[0] = (headsum(r * k * rk_ref[...]) * v).astype(bonus_ref.dtype)
    a_vec = -kk
    b_vec = kk * a

    tri = tri_ref[...]
    cum = sum(jnp.dot(tri, part, preferred_element_type=F32) for part in _split2(lw))

    t_idx = lax.broadcasted_iota(jnp.int32, (c, PACK_LANES), 0)
    s_idx = lax.broadcasted_iota(jnp.int32, (c, PACK_LANES), 1) % RWKV_HEAD
    strict = s_idx < t_idx
    incl = s_idx <= t_idx
    eye = (s_idx == t_idx).astype(F32)
    prow = lax.broadcasted_iota(jnp.int32, (PACK_LANES, PACK_LANES), 0)
    pcol = lax.broadcasted_iota(jnp.int32, (PACK_LANES, PACK_LANES), 1)
    same_head = (prow // RWKV_HEAD) == (pcol // RWKV_HEAD)
    on_diag = prow == pcol
    head = _lane_head(PACK_LANES)

    cum_last = jnp.concatenate(
        [jnp.broadcast_to(cum[(cc + 1) * c - 1:(cc + 1) * c, :], (c, w)) for cc in range(nch)], axis=0)
    e_pos = jnp.exp2(cum)
    e_neg = jnp.exp2(-cum)
    e_last = jnp.exp2(cum_last - cum)
    at_all = a_vec * jnp.exp2(cum - lw)
    bt_all = b_vec * e_neg
    kt_all = k * e_neg
    rt_all = r * e_pos
    bl_all = b_vec * e_last
    kl_all = k * e_last
    w_end_all = jnp.exp2(cum_last)

    probs = [(cc, gk) for cc in range(nch) for gk in packs]
    sl = {pr: (slice(pr[0] * c, (pr[0] + 1) * c), slice(pr[1] * PACK_LANES, (pr[1] + 1) * PACK_LANES))
          for pr in probs}
    at = {pr: at_all[sl[pr]] for pr in probs}
    rt = {pr: rt_all[sl[pr]] for pr in probs}
    vv = {pr: v[sl[pr]] for pr in probs}
    m_bk = {pr: lax.dot_general(
        jnp.concatenate([at[pr], rt[pr]], axis=0).astype(BF16),
        jnp.concatenate([_bd_stack(bt_all[sl[pr]]), _bd_stack(kt_all[sl[pr]])], axis=0),
        (((1,), (1,)), ((), ())), preferred_element_type=F32) for pr in probs}
    advance_scan()
    m_b = {pr: m_bk[pr][:, :PACK_LANES] for pr in probs}
    m_k = {pr: m_bk[pr][:, PACK_LANES:] for pr in probs}
    l_ab = {pr: jnp.where(strict, m_b[pr][:c], 0.0) for pr in probs}
    t_inv = {pr: eye + l_ab[pr] for pr in probs}
    x = {pr: _pdot(l_ab[pr], l_ab[pr]) for pr in probs}
    levels = int(math.log2(c)) - 1
    for lvl in range(levels):
        advance_scan()
        if lvl < levels - 1:
            tx = {pr: _pdot2(x[pr], t_inv[pr], x[pr]) for pr in probs}
            t_inv = {pr: t_inv[pr] + tx[pr][0] for pr in probs}
            x = {pr: tx[pr][1] for pr in probs}
        else:
            t_inv = {pr: t_inv[pr] + _pdot(x[pr], t_inv[pr]) for pr in probs}
    akv = {pr: _pdot(jnp.where(strict, m_k[pr][:c], 0.0), vv[pr]) for pr in probs}
    au = {pr: _pdot2(t_inv[pr], at[pr], akv[pr]) for pr in probs}
    a_hat = {pr: au[pr][0] for pr in probs}
    u0 = {pr: au[pr][1] for pr in probs}
    while scan["chunk"] < nch:
        advance_scan()
    for pr in probs:
        rs, ls = sl[pr]
        m_rb = jnp.where(incl, m_b[pr][c:], 0.0)
        m_rk = jnp.where(incl, m_k[pr][c:], 0.0)
        r_add, y_add = _pdot2(m_rb, a_hat[pr], u0[pr])
        rhat_s[rs, ls] = (rt[pr] + r_add).astype(rhat_s.dtype)
        y0_s[rs, ls] = y_add + _pdot(m_rk, vv[pr])
    for pr in probs:
        rs, ls = sl[pr]
        bl = bl_all[sl[pr]]
        p_full = jnp.dot(a_hat[pr].T.astype(BF16), bl.astype(BF16), preferred_element_type=F32)
        w_end = w_end_all[rs.start:rs.start + 1, ls]
        p_bd = jnp.where(same_head, p_full, 0.0) + jnp.where(on_diag, w_end, 0.0)
        p_s[pr[0] * PACK_LANES:(pr[0] + 1) * PACK_LANES, ls] = p_bd.astype(p_s.dtype)
        uv_t = jnp.concatenate([u0[pr], vv[pr]], axis=0).T
        bk = jnp.concatenate([bl, kl_all[sl[pr]]], axis=0).astype(BF16)
        f = jnp.dot(uv_t.astype(BF16), bk, preferred_element_type=F32)
        q_s[rs, ls] = sum(jnp.where(head == h, f[h * RWKV_HEAD:(h + 1) * RWKV_HEAD], 0.0)
                          for h in range(PACK_HEADS))


def _wkv(rw, mu, w0, w_decay_up, a0, w_iclr_up, k_k, k_a, r_k, tm):
    b, s, width = rw.shape
    w = RWKV_WIDTH
    nt = s // tm
    n_tiles = b * nt
    wlora = (0.5 * jnp.concatenate(
        [jnp.concatenate([w_decay_up, jnp.zeros((DECAY_LORA, w), F32)], 1),
         jnp.concatenate([jnp.zeros((ICLR_LORA, w), F32), w_iclr_up], 1)], 0)).astype(BF16)
    w0 = 0.5 * w0
    a0 = 0.5 * a0
    ones_bd = _head_ones()
    tid = np.arange(tm)
    tri = jnp.asarray(((tid[:, None] >= tid[None, :]) &
                       (tid[:, None] // WKV_CHUNK == tid[None, :] // WKV_CHUNK)).astype(np.float32), BF16)
    row = lambda t: t.reshape(1, -1)
    prep_tile = lambda g: jnp.minimum(g, n_tiles - 1)
    scan_tile = lambda g: jnp.maximum(g - 1, 0)
    rw_spec = pl.BlockSpec((1, tm, width), lambda g: (prep_tile(g) // nt, prep_tile(g) % nt, 0))
    prev = pl.BlockSpec((1, 8, width),
                        lambda g: (prep_tile(g) // nt, jnp.maximum((prep_tile(g) % nt) * (tm // 8) - 1, 0), 0))
    y_spec = pl.BlockSpec((1, tm, w), lambda g: (scan_tile(g) // nt, scan_tile(g) % nt, 0))
    bonus_spec = pl.BlockSpec((1, tm, w), lambda g: (prep_tile(g) // nt, prep_tile(g) % nt, 0))
    n_chunks = tm // WKV_CHUNK
    return pl.pallas_call(
        functools.partial(_wkv_kernel, tm=tm, nt=nt, n_tiles=n_tiles),
        grid=(n_tiles + 1,),
        in_specs=[rw_spec, prev, _const_spec((1, width)), _const_spec(wlora.shape)]
                 + [_const_spec((1, w))] * 5 + [_const_spec(ones_bd.shape), _const_spec(tri.shape)],
        out_specs=[y_spec, bonus_spec],
        out_shape=[jax.ShapeDtypeStruct((b, s, w), BF16),
                   jax.ShapeDtypeStruct((b, s, w), BF16)],
        scratch_shapes=[pltpu.VMEM((RWKV_HEAD, w), F32),
                        pltpu.VMEM((tm, w), BF16),
                        pltpu.VMEM((tm, w), F32),
                        pltpu.VMEM((n_chunks * PACK_LANES, w), BF16),
                        pltpu.VMEM((tm, w), F32)],
        compiler_params=_params("arbitrary"),
    )(rw, rw, row(mu), wlora, row(w0), row(a0), row(k_k), row(k_a), row(r_k), ones_bd, tri)


def _epilogue_kernel(x_ref, gate_ref, attn_ref, ga_ref, y_ref, bonus_ref, gb_ref, ma_ref, mb_ref,
                     ones_ref, gng_ref, gnb_ref, wpa_ref, wpb_ref, wout_ref, pg_ref, pb_ref, o_ref):
    silu_half = lambda th: th + th * jnp.tanh(th)
    ones_tile = ones_ref[...]
    headmean = lambda t: _head_sums(t, ones_tile) * (1.0 / RWKV_HEAD)

    y = y_ref[0].astype(F32)
    yc = y - headmean(y)
    yn = yc * lax.rsqrt(headmean(yc * yc) + GN_EPS)
    yb = yn * gng_ref[...] + gnb_ref[...] + bonus_ref[0].astype(F32)
    yb = yb * silu_half(gb_ref[0].astype(F32))
    ya = attn_ref[0].astype(F32) * silu_half(ga_ref[0].astype(F32))
    ya_p = jnp.dot(ya.astype(BF16), wpa_ref[...], preferred_element_type=F32)
    yb_p = jnp.dot(yb.astype(BF16), wpb_ref[...], preferred_element_type=F32)
    merged2 = ((1.0 + jnp.tanh(ma_ref[0].astype(F32))) * ya_p
               + (1.0 + jnp.tanh(mb_ref[0].astype(F32))) * yb_p)
    sub2 = jnp.dot(merged2.astype(BF16), wout_ref[...], preferred_element_type=F32)
    z = ALPHA * x_ref[0] + (0.5 * (1.0 + gate_ref[0])) * sub2
    zc = z - jnp.mean(z, -1, keepdims=True)
    zn = zc * lax.rsqrt(jnp.mean(zc * zc, -1, keepdims=True) + LN_EPS)
    o_ref[0] = (zn * pg_ref[...] + pb_ref[...]).astype(o_ref.dtype)


def _epilogue(x, gate, attn, ga, y, bonus, gb, ma, mb, gn_g, gn_b, w_proj_a, w_proj_b, w_out,
              post_g, post_b, tm):
    b, s, d = x.shape
    w = RWKV_WIDTH
    ones_bd = _head_ones()
    row = lambda t: t.reshape(1, -1)
    tok = lambda wd: pl.BlockSpec((1, tm, wd), lambda bi, i: (bi, i, 0))
    vec = pl.BlockSpec((1, 1, d), lambda bi, i: (bi, 0, 0))
    return pl.pallas_call(
        _epilogue_kernel,
        grid=(b, s // tm),
        in_specs=[tok(d), vec, tok(w), tok(w), tok(w), tok(w), tok(w), tok(d), tok(d),
                  _const_spec(ones_bd.shape), _const_spec((1, w)), _const_spec((1, w)),
                  _const_spec((w, d)), _const_spec((w, d)), _const_spec((d, d)),
                  _const_spec((1, d)), _const_spec((1, d))],
        out_specs=tok(d),
        out_shape=jax.ShapeDtypeStruct((b, s, d), x.dtype),
        compiler_params=_params("parallel", "parallel"),
    )(x, gate, attn, ga, y, bonus, gb, ma, mb, ones_bd, row(gn_g), row(gn_b),
      w_proj_a.astype(BF16), w_proj_b.astype(BF16), w_out.astype(BF16), row(post_g), row(post_b))


def _layer(x, c, pos, w_ada, b_ada, w_in, q_norm_g, w_uq, kv_norm_g, w_ukv, mu_rwkv, w0, w_decay_up,
           a0, w_iclr_up, k_k, k_a, r_k, gn_g, gn_b, w_proj_a, w_proj_b, w_out, post_g, post_b):
    b, s, d = x.shape
    tm = min(512, s)
    ada = _adaln(c, w_ada, b_ada)
    shift, scale, gate = (ada[:, j * d:(j + 1) * d].reshape(b, 1, d) for j in range(3))
    qt, k, vt, ga, rw, gb, ma, mb = _ln_proj(x, scale, shift, w_in.astype(BF16), pos, q_norm_g, kv_norm_g,
                                             w_uq, w_ukv, tm)
    attn = _attention(qt, k, vt)
    y, bonus = _wkv(rw, mu_rwkv, w0, w_decay_up, a0, w_iclr_up, k_k, k_a, r_k.reshape(-1), min(256, s))
    return _epilogue(x, gate, attn, ga, y, bonus, gb, ma, mb, gn_g, gn_b, w_proj_a, w_proj_b, w_out,
                     post_g, post_b, tm)


def kernel(x, c, positions, w_ada, b_ada, w_in, q_norm_g, w_uq, kv_norm_g, w_ukv, mu_rwkv, w0,
           w_decay_up, a0, w_iclr_up, k_k, k_a, r_k, gn_g, gn_b, w_proj_a, w_proj_b, w_out, post_g,
           post_b):
    pos = positions.astype(F32)[:, None, :]
    for l in range(w_ada.shape[0]):
        x = _layer(x, c, pos, w_ada[l], b_ada[l], w_in[l], q_norm_g[l], w_uq[l], kv_norm_g[l],
                   w_ukv[l], mu_rwkv[l], w0[l], w_decay_up[l], a0[l], w_iclr_up[l], k_k[l], k_a[l],
                   r_k[l], gn_g[l], gn_b[l], w_proj_a[l], w_proj_b[l], w_out[l], post_g[l], post_b[l])
    return x
```

```python
import functools
import math

import jax
import jax.numpy as jnp
import numpy as np
from jax import lax
from jax.experimental import pallas as pl
from jax.experimental.pallas import tpu as pltpu

F32 = jnp.float32
BF16 = jnp.bfloat16

D_MODEL = 1024
LN_EPS = 1e-5
RMS_EPS = 1e-6
GN_EPS = 64e-5

MLA_HEADS = 8
MLA_NOPE = 64
MLA_ROPE = 32
MLA_V = 64
MLA_QK = MLA_NOPE + MLA_ROPE
MLA_Q_RANK = 256
MLA_KV_RANK = 128
MLA_WIDTH = MLA_HEADS * MLA_V
ROPE_THETA = 10000.0
ATTN_CHUNK = 64

RWKV_HEADS = 8
RWKV_HEAD = 64
RWKV_WIDTH = RWKV_HEADS * RWKV_HEAD
DECAY_LORA = 64
ICLR_LORA = 64
RWKV_SHIFT_WIDTH = 3 * RWKV_WIDTH + DECAY_LORA + ICLR_LORA
WKV_CHUNK = 64

DEPTH = 1
ALPHA = (2.0 * DEPTH) ** 0.25

LANES = 128
HEAD_PAD = 128
V_ROWS = 80
KEY_TILE = 256
PACK_HEADS = 2
PACK_LANES = PACK_HEADS * RWKV_HEAD

QKR_WIDTH = 512
VMEM_LIMIT = 56 * 1024 * 1024

NEG_BIG = -1e30


def _const_spec(shape):
    n = len(shape)
    return pl.BlockSpec(shape, lambda *_: (0,) * n)


def _params(*sem):
    return pltpu.CompilerParams(dimension_semantics=sem, vmem_limit_bytes=VMEM_LIMIT)


def _adaln_kernel(c_ref, w_ref, b_ref, o_ref):
    c = c_ref[...]
    sc = c * jax.nn.sigmoid(c)
    o_ref[...] = jnp.dot(sc, w_ref[...], preferred_element_type=F32) + b_ref[...]


def _adaln(c, w_ada, b_ada):
    b = c.shape[0]
    return pl.pallas_call(
        _adaln_kernel,
        out_shape=jax.ShapeDtypeStruct((b, 3 * D_MODEL), F32),
        compiler_params=pltpu.CompilerParams(vmem_limit_bytes=VMEM_LIMIT),
    )(c, w_ada, b_ada.reshape(1, -1))


_PROJ_WIDTHS = (MLA_WIDTH, RWKV_SHIFT_WIDTH, RWKV_WIDTH, D_MODEL, D_MODEL)
_PROJ_SCALES = (0.5, 1.0, 0.5, 0.5, 0.5)
_RWKV_SLICE = 1


def _ln_proj_kernel(x_ref, scale_ref, shift_ref, w_in_ref, pos_ref, inv_ref, qg_ref, kvg_ref, wqp_ref,
                    wqr_ref, wk_ref, wvt_ref, mu_ref, qt_out, k_out, vt_out, *refs):
    out_refs, w_ref, last_row = refs[:-2], refs[-2], refs[-1]

    @pl.when(jnp.logical_and(pl.program_id(0) == 0, pl.program_id(1) == 0))
    def _():
        last_row[...] = jnp.zeros_like(last_row)
        used = MLA_Q_RANK + MLA_KV_RANK + MLA_ROPE
        shift = QKR_WIDTH - used
        lane = lax.broadcasted_iota(jnp.int32, (1, LANES), 1)
        rows = 256
        for r in range(0, D_MODEL, rows):
            rs = slice(r, r + rows)
            aligned = (used // LANES) * LANES
            w_ref[rs, :aligned] = w_in_ref[rs, :aligned]
            edge = w_in_ref[rs, aligned:aligned + LANES]
            w_ref[rs, aligned:QKR_WIDTH] = jnp.where(lane < used - aligned, edge, jnp.zeros_like(edge))
            for c0 in range(QKR_WIDTH, w_ref.shape[1], LANES):
                w_ref[rs, c0:c0 + LANES] = w_in_ref[rs, c0 - shift:c0 - shift + LANES]

    x = x_ref[0]
    xc = x - jnp.mean(x, -1, keepdims=True)
    h = xc * lax.rsqrt(jnp.mean(xc * xc, -1, keepdims=True) + LN_EPS)
    hb = (h * (1.0 + scale_ref[0]) + shift_ref[0]).astype(BF16)
    latent = jnp.dot(hb, w_ref[:, :QKR_WIDTH], preferred_element_type=F32)
    off = QKR_WIDTH
    for n, (o_ref, width, scale) in enumerate(zip(out_refs, _PROJ_WIDTHS, _PROJ_SCALES)):
        acc = jnp.dot(hb, w_ref[:, off:off + width], preferred_element_type=F32)
        if scale != 1.0:
            acc = acc * scale
        if n == _RWKV_SLICE:
            tm = acc.shape[0]
            before = jnp.where(pl.program_id(1) > 0, last_row[0:1, :], 0.0)
            row = lax.broadcasted_iota(jnp.int32, (tm, 1), 0)
            prev = jnp.where(row == 0, before, pltpu.roll(acc, 1, axis=0))
            last_row[0:1, :] = acc[tm - 1:tm, :]
            acc = acc + (prev - acc) * mu_ref[...]
        o_ref[0] = acc.astype(o_ref.dtype)
        off += width
        if n == 1:
            _mla_prep(latent, pos_ref[0], inv_ref[...], qg_ref[...], kvg_ref[...], wqp_ref, wqr_ref,
                      wk_ref, wvt_ref, qt_out, k_out, vt_out)


def _ln_proj(x, scale, shift, w_in, pos, q_norm_g, kv_norm_g, w_uq, w_ukv, mu, tm):
    b, s, d = x.shape
    wqp, wqr, wk, wvt = _mla_weights(w_uq, w_ukv)
    half = MLA_ROPE // 2
    inv = (ROPE_THETA ** (-jnp.arange(0, MLA_ROPE, 2, dtype=F32) / MLA_ROPE)).reshape(half, 1)
    width = MLA_HEADS * HEAD_PAD
    vrows = MLA_HEADS * V_ROWS
    tok = lambda w: pl.BlockSpec((1, tm, w), lambda bi, i: (bi, i, 0))
    vec = pl.BlockSpec((1, 1, d), lambda bi, i: (bi, 0, 0))
    tiles = lambda rows: pl.BlockSpec((1, tm // KEY_TILE, rows, KEY_TILE), lambda bi, i: (bi, i, 0, 0))
    return pl.pallas_call(
        _ln_proj_kernel,
        grid=(b, s // tm),
        in_specs=[tok(d), vec, vec,
                  pl.BlockSpec(w_in.shape, lambda bi, i: (0, 0), pipeline_mode=pl.Buffered(1)),
                  pl.BlockSpec((1, 1, tm), lambda bi, i: (bi, 0, i)),
                  _const_spec((half, 1)), _const_spec((1, MLA_Q_RANK)), _const_spec((1, MLA_KV_RANK)),
                  _const_spec(wqp.shape), _const_spec(wqr.shape), _const_spec(wk.shape),
                  _const_spec(wvt.shape), _const_spec((1, RWKV_SHIFT_WIDTH))],
        out_specs=[tiles(width), tok(width), tiles(vrows)] + [tok(w) for w in _PROJ_WIDTHS],
        out_shape=[jax.ShapeDtypeStruct((b, s // KEY_TILE, width, KEY_TILE), BF16),
                   jax.ShapeDtypeStruct((b, s, width), BF16),
                   jax.ShapeDtypeStruct((b, s // KEY_TILE, vrows, KEY_TILE), BF16)]
                  + [jax.ShapeDtypeStruct((b, s, w), BF16) for w in _PROJ_WIDTHS],
        scratch_shapes=[pltpu.VMEM((d, QKR_WIDTH + sum(_PROJ_WIDTHS)), BF16),
                        pltpu.VMEM((8, RWKV_SHIFT_WIDTH), F32)],
        compiler_params=_params("arbitrary", "arbitrary"),
    )(x, scale, shift, w_in, pos, inv, q_norm_g.reshape(1, -1), kv_norm_g.reshape(1, -1),
      wqp, wqr, wk, wvt, mu.reshape(1, -1))


def _mla_prep(t, pos, inv, qg, kvg, wqp_ref, wqr_ref, wk_ref, wvt_ref, qt_out, k_out, vt_out):
    tm = t.shape[0]
    half = MLA_ROPE // 2
    qc = t[:, :MLA_Q_RANK]
    kvc = t[:, MLA_Q_RANK:MLA_Q_RANK + MLA_KV_RANK]
    kr = t[:, MLA_Q_RANK + MLA_KV_RANK:]
    qn = qc * lax.rsqrt(jnp.mean(qc * qc, -1, keepdims=True) + RMS_EPS) * qg
    kvn = kvc * lax.rsqrt(jnp.mean(kvc * kvc, -1, keepdims=True) + RMS_EPS) * kvg
    qn_t = qn.T.astype(BF16)
    kvn_t = kvn.T.astype(BF16)
    kr_t = kr.T

    ang = inv * pos
    cos_h = jnp.cos(ang)
    sin_h = jnp.sin(ang)
    cos_r = jnp.concatenate([cos_h, cos_h], axis=0)
    sin_r = jnp.concatenate([sin_h, sin_h], axis=0)

    plain_t = jnp.dot(wqp_ref[...], qn_t, preferred_element_type=F32)
    rot_t = jnp.dot(wqr_ref[...], qn_t, preferred_element_type=F32)
    scale = MLA_QK ** -0.5 * math.log2(math.e)
    zpad = jnp.zeros((HEAD_PAD - MLA_QK, tm), BF16)
    for h in range(MLA_HEADS):
        base = h * HEAD_PAD
        nope = (plain_t[base:base + MLA_NOPE] * scale).astype(BF16)
        pe = ((plain_t[base + MLA_NOPE:base + MLA_QK] * cos_r
               + rot_t[h * MLA_ROPE:(h + 1) * MLA_ROPE] * sin_r) * scale).astype(BF16)
        q_t = jnp.concatenate([nope, pe, zpad], axis=0)
        for kt in range(tm // KEY_TILE):
            qt_out[0, kt, base:base + HEAD_PAD, :] = q_t[:, kt * KEY_TILE:(kt + 1) * KEY_TILE]

    k1 = kr_t[:half]
    k2 = kr_t[half:MLA_ROPE]
    kpe_t = jnp.concatenate([jnp.zeros((MLA_NOPE, tm), F32),
                             k1 * cos_h - k2 * sin_h, k1 * sin_h + k2 * cos_h,
                             jnp.zeros((HEAD_PAD - MLA_QK, tm), F32)], axis=0)
    kpe = kpe_t.T
    ka = jnp.dot(kvn.astype(BF16), wk_ref[...], preferred_element_type=F32)
    for h in range(MLA_HEADS):
        sl = slice(h * HEAD_PAD, (h + 1) * HEAD_PAD)
        k_out[0, :, sl] = (ka[:, sl] + kpe).astype(BF16)

    vt = jnp.dot(wvt_ref[...], kvn_t, preferred_element_type=F32)
    vrow = lax.broadcasted_iota(jnp.int32, (MLA_HEADS * V_ROWS, 1), 0) % V_ROWS
    vt = (vt + jnp.where(vrow == MLA_V, 1.0, 0.0)).astype(BF16)
    for kt in range(tm // KEY_TILE):
        vt_out[0, kt] = vt[:, kt * KEY_TILE:(kt + 1) * KEY_TILE]


def _mla_weights(w_uq, w_ukv):
    half = MLA_ROPE // 2
    wq = w_uq.reshape(MLA_Q_RANK, MLA_HEADS, MLA_QK)
    zq = jnp.zeros((MLA_Q_RANK, MLA_HEADS, HEAD_PAD - MLA_QK), F32)
    wq_plain_t = jnp.concatenate([wq, zq], -1).reshape(MLA_Q_RANK, -1).T.astype(BF16)
    t1 = wq[:, :, MLA_NOPE:MLA_NOPE + half]
    t2 = wq[:, :, MLA_NOPE + half:]
    wq_rot_t = jnp.concatenate([-t2, t1], -1).reshape(MLA_Q_RANK, -1).T.astype(BF16)

    wkv = w_ukv.reshape(MLA_KV_RANK, MLA_HEADS, MLA_NOPE + MLA_V)
    zk = jnp.zeros((MLA_KV_RANK, MLA_HEADS, HEAD_PAD - MLA_NOPE), F32)
    wk = jnp.concatenate([wkv[:, :, :MLA_NOPE], zk], -1).reshape(MLA_KV_RANK, -1).astype(BF16)
    zv = jnp.zeros((MLA_KV_RANK, MLA_HEADS, V_ROWS - MLA_V), F32)
    wvt = jnp.concatenate([wkv[:, :, MLA_NOPE:], zv], -1).reshape(MLA_KV_RANK, -1).T.astype(BF16)
    return wq_plain_t, wq_rot_t, wk, wvt


ATTN_SUBTILES = 2
ATTN_PAIRS_PER_ITER = 2


def _attn_kernel(qt_ref, k_ref, vt_ref, o_ref, s_buf):
    ts = KEY_TILE
    nsub = ATTN_SUBTILES
    n_qt = o_ref.shape[1] // (nsub * ts)
    key_chunk = lax.broadcasted_iota(jnp.int32, (ts, ts), 0) // ATTN_CHUNK
    qry_chunk = lax.broadcasted_iota(jnp.int32, (ts, ts), 1) // ATTN_CHUNK
    diag_mask = key_chunk <= qry_chunk
    lanes = [slice(hh * HEAD_PAD, (hh + 1) * HEAD_PAD) for hh in range(2)]
    chains = [(hh, sb) for hh in range(2) for sb in range(nsub)]

    def scores(i, j, slot, first_sub, masked_sub):
        if isinstance(j, int):
            start = j * ts
        else:
            start = pl.multiple_of(j * ts, ts)
        kj = [k_ref[0, pl.ds(start, ts), lanes[hh]] for hh in range(2)]
        tile_max = {}
        for n, ch in enumerate(chains):
            if ch[1] >= first_sub:
                q_t = qt_ref[0, i * nsub + ch[1], lanes[ch[0]], :]
                s = jnp.dot(kj[ch[0]], q_t, preferred_element_type=F32)
                if ch[1] == masked_sub:
                    s = jnp.where(diag_mask, s, NEG_BIG)
                s_buf[slot, n] = s
                tile_max[ch] = jnp.max(s, axis=0, keepdims=True)
        return tile_max

    def consume(j, slot, carry, tile_max, first_sub):
        vj = [vt_ref[0, j, hh * V_ROWS:(hh + 1) * V_ROWS, :] for hh in range(2)]
        active = [(n, ch) for n, ch in enumerate(chains) if ch[1] >= first_sub]
        m_new = {ch: jnp.maximum(carry[ch][0], tile_max[ch]) for _, ch in active}
        alpha = {ch: jnp.exp2(carry[ch][0] - m_new[ch]) for _, ch in active}
        out = dict(carry)
        for n, ch in active:
            p = jnp.exp2(s_buf[slot, n] - m_new[ch]).astype(BF16)
            pv = jnp.dot(vj[ch[0]], p, preferred_element_type=F32)
            out[ch] = (m_new[ch], carry[ch][1] * alpha[ch] + pv)
        return out

    head_max = scores(0, 0, 0, 0, 0)
    for i in range(n_qt):
        sa, sb_ = (0, 1) if i % 2 == 0 else (2, 3)
        next_sa = 2 if i % 2 == 0 else 0

        def pair_step(t, carry, max_a, on_diag, i=i, sa=sa, sb_=sb_):
            max_b = scores(i, 2 * t + 1, sb_, 0, -1)
            carry = consume(2 * t, sa, carry, max_a, 0)
            max_a = scores(i, 2 * t + 2, sa, 0, 0 if on_diag else -1)
            return consume(2 * t + 1, sb_, carry, max_b, 0), max_a

        def loop_body(t, flat):
            carry = {ch: (flat[3 * n], flat[3 * n + 1]) for n, ch in enumerate(chains)}
            max_a = {ch: flat[3 * n + 2] for n, ch in enumerate(chains)}
            for u in range(ATTN_PAIRS_PER_ITER):
                carry, max_a = pair_step(ATTN_PAIRS_PER_ITER * t + u, carry, max_a, False)
            return tuple(v for ch in chains for v in (*carry[ch], max_a[ch]))

        n_iter, n_rest = divmod(max(i - 1, 0), ATTN_PAIRS_PER_ITER)
        init = tuple(v for ch in chains
                     for v in (jnp.full((1, ts), NEG_BIG, F32), jnp.zeros((V_ROWS, ts), F32), head_max[ch]))
        flat = lax.fori_loop(0, n_iter, loop_body, init)
        carry = {ch: (flat[3 * n], flat[3 * n + 1]) for n, ch in enumerate(chains)}
        max_a = {ch: flat[3 * n + 2] for n, ch in enumerate(chains)}
        for u in range(n_rest):
            carry, max_a = pair_step(ATTN_PAIRS_PER_ITER * n_iter + u, carry, max_a, False)
        if i >= 1:
            carry, max_a = pair_step(i - 1, carry, max_a, True)
        max_b = scores(i, 2 * i + 1, sb_, 1, 1)
        if i + 1 < n_qt:
            head_max = scores(i + 1, 0, next_sa, 0, -1)
        carry = consume(2 * i, sa, carry, max_a, 0)
        carry = consume(2 * i + 1, sb_, carry, max_b, 1)

        for sb in range(nsub):
            normed = []
            for hh in range(2):
                acc = carry[(hh, sb)][1]
                normed.append(acc[:MLA_V] / acc[MLA_V:MLA_V + 1])
            out_t = jnp.concatenate(normed, axis=0)
            row0 = (i * nsub + sb) * ts
            o_ref[0, row0:row0 + ts, :] = out_t.T.astype(o_ref.dtype)


def _attention(qt, k, vt):
    b, s, _ = k.shape
    pairs = MLA_HEADS // 2
    nsub = ATTN_SUBTILES
    return pl.pallas_call(
        _attn_kernel,
        grid=(b, pairs),
        in_specs=[pl.BlockSpec((1, s // KEY_TILE, 2 * HEAD_PAD, KEY_TILE), lambda bi, g: (bi, 0, g, 0)),
                  pl.BlockSpec((1, s, 2 * HEAD_PAD), lambda bi, g: (bi, 0, g)),
                  pl.BlockSpec((1, s // KEY_TILE, 2 * V_ROWS, KEY_TILE), lambda bi, g: (bi, 0, g, 0))],
        out_specs=pl.BlockSpec((1, s, 2 * MLA_V), lambda bi, g: (bi, 0, g)),
        out_shape=jax.ShapeDtypeStruct((b, s, MLA_WIDTH), BF16),
        scratch_shapes=[pltpu.VMEM((4, 2 * nsub, KEY_TILE, KEY_TILE), F32)],
        compiler_params=_params("parallel", "parallel"),
    )(qt, k, vt)


def _lane_head(width):
    return lax.broadcasted_iota(jnp.int32, (1, width), 1) // RWKV_HEAD


def _bd_stack(x):
    head = _lane_head(x.shape[1])
    xb = x.astype(BF16)
    return jnp.concatenate([jnp.where(head == h, xb, jnp.zeros_like(xb)) for h in range(PACK_HEADS)], axis=0)


def _pdot(a, b):
    return jnp.dot(a.astype(BF16), _bd_stack(b), preferred_element_type=F32)


def _pdot2(a, b1, b2):
    rhs = jnp.concatenate([_bd_stack(b1), _bd_stack(b2)], axis=1)
    out = jnp.dot(a.astype(BF16), rhs, preferred_element_type=F32)
    return out[:, :PACK_LANES], out[:, PACK_LANES:]


def _pdot_nt(a, b):
    return lax.dot_general(a.astype(BF16), _bd_stack(b), (((1,), (1,)), ((), ())),
                           preferred_element_type=F32)


def _split2(x):
    x1 = x.astype(BF16)
    x2 = (x - x1.astype(F32)).astype(BF16)
    return x1, x2


MXU_TILE = 256


def _head_ones():
    hid = np.arange(MXU_TILE) // RWKV_HEAD
    return jnp.asarray((hid[:, None] == hid[None, :]).astype(np.float32), BF16)


def _head_sums(t, ones_tile):
    tb = t.astype(BF16)
    return jnp.concatenate(
        [jnp.dot(tb[:, o:o + MXU_TILE], ones_tile, preferred_element_type=F32)
         for o in range(0, t.shape[1], MXU_TILE)], axis=1)


def _wkv_kernel(rw_ref, wlora_ref, w0_ref, a0_ref, kk_ref, ka_ref, rk_ref,
                ones_ref, tri_ref, y_ref, bonus_ref, s_ref, rhat_s, y0_s, p_s, q_s, *, tm, nt):
    g = pl.program_id(0)
    c = WKV_CHUNK
    w = RWKV_WIDTH
    nch = tm // c
    packs = range(w // PACK_LANES)
    lanes_of = lambda gk: slice(gk * PACK_LANES, (gk + 1) * PACK_LANES)

    @pl.when(g == 0)
    def _():
        for ref in (s_ref, rhat_s, y0_s, p_s, q_s):
            ref[...] = jnp.zeros_like(ref)

    scan_starts_sequence = (jnp.maximum(g - 1, 0) % nt) == 0
    scan = {"state": {gk: jnp.where(scan_starts_sequence, 0.0, s_ref[:, lanes_of(gk)]) for gk in packs},
            "chunk": 0}

    def advance_scan():
        cc = scan["chunk"]
        if cc >= nch:
            return
        state = scan["state"]
        rs = slice(cc * c, (cc + 1) * c)
        for gk in packs:
            ls = lanes_of(gk)
            y_ref[0, rs, ls] = (_pdot_nt(rhat_s[rs, ls], state[gk]) + y0_s[rs, ls]).astype(y_ref.dtype)
        state = {gk: jnp.dot(state[gk].astype(BF16), p_s[cc * PACK_LANES:(cc + 1) * PACK_LANES, lanes_of(gk)],
                             preferred_element_type=F32) + q_s[rs, lanes_of(gk)] for gk in packs}
        scan["state"] = state
        scan["chunk"] = cc + 1
        if cc + 1 == nch:
            for gk in packs:
                s_ref[:, lanes_of(gk)] = state[gk]

    u = rw_ref[0].astype(F32)
    r = u[:, :w]
    k = u[:, w:2 * w]
    v = u[:, 2 * w:3 * w]
    lora_in = u[:, 3 * w:]
    lane = lax.broadcasted_iota(jnp.int32, (1, DECAY_LORA + ICLR_LORA), 1)
    lora_in = jnp.where(lane < DECAY_LORA, jnp.tanh(lora_in), lora_in)
    lora = jnp.dot(lora_in.astype(BF16), wlora_ref[...], preferred_element_type=F32)
    c_lw = -0.5 * math.exp(-0.5) * math.log2(math.e)
    lw = c_lw + c_lw * jnp.tanh(w0_ref[...] + lora[:, :w])
    a = 0.5 + 0.5 * jnp.tanh(a0_ref[...] + lora[:, w:])

    ones_tile = ones_ref[...]
    headsum = lambda t: _head_sums(t, ones_tile)
    kk = k * kk_ref[...]
    kk = kk * lax.rsqrt(jnp.maximum(headsum(kk * kk), 1e-24))
    k = k * (1.0 + (a - 1.0) * ka_ref[...])
    bonus_ref[0] = (headsum(r * k * rk_ref[...]) * v).astype(bonus_ref.dtype)
    a_vec = -kk
    b_vec = kk * a

    tri = tri_ref[...]
    cum = sum(jnp.dot(tri, part, preferred_element_type=F32) for part in _split2(lw))

    t_idx = lax.broadcasted_iota(jnp.int32, (c, PACK_LANES), 0)
    s_idx = lax.broadcasted_iota(jnp.int32, (c, PACK_LANES), 1) % RWKV_HEAD
    strict = s_idx < t_idx
    incl = s_idx <= t_idx
    eye = (s_idx == t_idx).astype(F32)
    prow = lax.broadcasted_iota(jnp.int32, (PACK_LANES, PACK_LANES), 0)
    pcol = lax.broadcasted_iota(jnp.int32, (PACK_LANES, PACK_LANES), 1)
    same_head = (prow // RWKV_HEAD) == (pcol // RWKV_HEAD)
    on_diag = prow == pcol
    head = _lane_head(PACK_LANES)

    cum_last = jnp.concatenate(
        [jnp.broadcast_to(cum[(cc + 1) * c - 1:(cc + 1) * c, :], (c, w)) for cc in range(nch)], axis=0)
    e_pos = jnp.exp2(cum)
    e_neg = jnp.exp2(-cum)
    e_last = jnp.exp2(cum_last - cum)
    at_all = a_vec * jnp.exp2(cum - lw)
    bt_all = b_vec * e_neg
    kt_all = k * e_neg
    rt_all = r * e_pos
    bl_all = b_vec * e_last
    kl_all = k * e_last
    w_end_all = jnp.exp2(cum_last)

    probs = [(cc, gk) for cc in range(nch) for gk in packs]
    sl = {pr: (slice(pr[0] * c, (pr[0] + 1) * c), slice(pr[1] * PACK_LANES, (pr[1] + 1) * PACK_LANES))
          for pr in probs}
    at = {pr: at_all[sl[pr]] for pr in probs}
    rt = {pr: rt_all[sl[pr]] for pr in probs}
    vv = {pr: v[sl[pr]] for pr in probs}
    m_bk = {pr: lax.dot_general(
        jnp.concatenate([at[pr], rt[pr]], axis=0).astype(BF16),
        jnp.concatenate([_bd_stack(bt_all[sl[pr]]), _bd_stack(kt_all[sl[pr]])], axis=0),
        (((1,), (1,)), ((), ())), preferred_element_type=F32) for pr in probs}
    advance_scan()
    m_b = {pr: m_bk[pr][:, :PACK_LANES] for pr in probs}
    m_k = {pr: m_bk[pr][:, PACK_LANES:] for pr in probs}
    l_ab = {pr: jnp.where(strict, m_b[pr][:c], 0.0) for pr in probs}
    t_inv = {pr: eye + l_ab[pr] for pr in probs}
    x = {pr: _pdot(l_ab[pr], l_ab[pr]) for pr in probs}
    levels = int(math.log2(c)) - 1
    for lvl in range(levels):
        advance_scan()
        if lvl < levels - 1:
            tx = {pr: _pdot2(x[pr], t_inv[pr], x[pr]) for pr in probs}
            t_inv = {pr: t_inv[pr] + tx[pr][0] for pr in probs}
            x = {pr: tx[pr][1] for pr in probs}
        else:
            t_inv = {pr: t_inv[pr] + _pdot(x[pr], t_inv[pr]) for pr in probs}
    akv = {pr: _pdot(jnp.where(strict, m_k[pr][:c], 0.0), vv[pr]) for pr in probs}
    au = {pr: _pdot2(t_inv[pr], at[pr], akv[pr]) for pr in probs}
    a_hat = {pr: au[pr][0] for pr in probs}
    u0 = {pr: au[pr][1] for pr in probs}
    while scan["chunk"] < nch:
        advance_scan()
    for pr in probs:
        rs, ls = sl[pr]
        m_rb = jnp.where(incl, m_b[pr][c:], 0.0)
        m_rk = jnp.where(incl, m_k[pr][c:], 0.0)
        r_add, y_add = _pdot2(m_rb, a_hat[pr], u0[pr])
        rhat_s[rs, ls] = (rt[pr] + r_add).astype(rhat_s.dtype)
        y0_s[rs, ls] = y_add + _pdot(m_rk, vv[pr])
    for pr in probs:
        rs, ls = sl[pr]
        bl = bl_all[sl[pr]]
        p_full = jnp.dot(a_hat[pr].T.astype(BF16), bl.astype(BF16), preferred_element_type=F32)
        w_end = w_end_all[rs.start:rs.start + 1, ls]
        p_bd = jnp.where(same_head, p_full, 0.0) + jnp.where(on_diag, w_end, 0.0)
        p_s[pr[0] * PACK_LANES:(pr[0] + 1) * PACK_LANES, ls] = p_bd.astype(p_s.dtype)
        uv_t = jnp.concatenate([u0[pr], vv[pr]], axis=0).T
        bk = jnp.concatenate([bl, kl_all[sl[pr]]], axis=0).astype(BF16)
        f = jnp.dot(uv_t.astype(BF16), bk, preferred_element_type=F32)
        q_s[rs, ls] = sum(jnp.where(head == h, f[h * RWKV_HEAD:(h + 1) * RWKV_HEAD], 0.0)
                          for h in range(PACK_HEADS))


def _wkv(rw, w0, w_decay_up, a0, w_iclr_up, k_k, k_a, r_k, tm):
    b, s, width = rw.shape
    w = RWKV_WIDTH
    nt = s // tm
    n_tiles = b * nt
    wlora = (0.5 * jnp.concatenate(
        [jnp.concatenate([w_decay_up, jnp.zeros((DECAY_LORA, w), F32)], 1),
         jnp.concatenate([jnp.zeros((ICLR_LORA, w), F32), w_iclr_up], 1)], 0)).astype(BF16)
    w0 = 0.5 * w0
    a0 = 0.5 * a0
    ones_bd = _head_ones()
    tid = np.arange(tm)
    tri = jnp.asarray(((tid[:, None] >= tid[None, :]) &
                       (tid[:, None] // WKV_CHUNK == tid[None, :] // WKV_CHUNK)).astype(np.float32), BF16)
    row = lambda t: t.reshape(1, -1)
    prep_tile = lambda g: jnp.minimum(g, n_tiles - 1)
    scan_tile = lambda g: jnp.maximum(g - 1, 0)
    rw_spec = pl.BlockSpec((1, tm, width), lambda g: (prep_tile(g) // nt, prep_tile(g) % nt, 0))
    y_spec = pl.BlockSpec((1, tm, w), lambda g: (scan_tile(g) // nt, scan_tile(g) % nt, 0))
    bonus_spec = pl.BlockSpec((1, tm, w), lambda g: (prep_tile(g) // nt, prep_tile(g) % nt, 0))
    n_chunks = tm // WKV_CHUNK
    return pl.pallas_call(
        functools.partial(_wkv_kernel, tm=tm, nt=nt),
        grid=(n_tiles + 1,),
        in_specs=[rw_spec, _const_spec(wlora.shape)]
                 + [_const_spec((1, w))] * 5 + [_const_spec(ones_bd.shape), _const_spec(tri.shape)],
        out_specs=[y_spec, bonus_spec],
        out_shape=[jax.ShapeDtypeStruct((b, s, w), BF16),
                   jax.ShapeDtypeStruct((b, s, w), BF16)],
        scratch_shapes=[pltpu.VMEM((RWKV_HEAD, w), F32),
                        pltpu.VMEM((tm, w), BF16),
                        pltpu.VMEM((tm, w), F32),
                        pltpu.VMEM((n_chunks * PACK_LANES, w), BF16),
                        pltpu.VMEM((tm, w), F32)],
        compiler_params=_params("arbitrary"),
    )(rw, wlora, row(w0), row(a0), row(k_k), row(k_a), row(r_k), ones_bd, tri)


def _epilogue_kernel(x_ref, gate_ref, attn_ref, ga_ref, y_ref, bonus_ref, gb_ref, ma_ref, mb_ref,
                     ones_ref, gng_ref, gnb_ref, wpa_ref, wpb_ref, wout_ref, pg_ref, pb_ref, o_ref):
    silu_half = lambda th: th + th * jnp.tanh(th)
    ones_tile = ones_ref[...]
    headmean = lambda t: _head_sums(t, ones_tile) * (1.0 / RWKV_HEAD)

    y = y_ref[0].astype(F32)
    yc = y - headmean(y)
    yn = yc * lax.rsqrt(headmean(yc * yc) + GN_EPS)
    yb = yn * gng_ref[...] + gnb_ref[...] + bonus_ref[0].astype(F32)
    yb = yb * silu_half(gb_ref[0].astype(F32))
    ya = attn_ref[0].astype(F32) * silu_half(ga_ref[0].astype(F32))
    ya_p = jnp.dot(ya.astype(BF16), wpa_ref[...], preferred_element_type=F32)
    yb_p = jnp.dot(yb.astype(BF16), wpb_ref[...], preferred_element_type=F32)
    merged2 = ((1.0 + jnp.tanh(ma_ref[0].astype(F32))) * ya_p
               + (1.0 + jnp.tanh(mb_ref[0].astype(F32))) * yb_p)
    sub2 = jnp.dot(merged2.astype(BF16), wout_ref[...], preferred_element_type=F32)
    z = ALPHA * x_ref[0] + (0.5 * (1.0 + gate_ref[0])) * sub2
    zc = z - jnp.mean(z, -1, keepdims=True)
    zn = zc * lax.rsqrt(jnp.mean(zc * zc, -1, keepdims=True) + LN_EPS)
    o_ref[0] = (zn * pg_ref[...] + pb_ref[...]).astype(o_ref.dtype)


def _epilogue(x, gate, attn, ga, y, bonus, gb, ma, mb, gn_g, gn_b, w_proj_a, w_proj_b, w_out,
              post_g, post_b, tm):
    b, s, d = x.shape
    w = RWKV_WIDTH
    ones_bd = _head_ones()
    row = lambda t: t.reshape(1, -1)
    tok = lambda wd: pl.BlockSpec((1, tm, wd), lambda bi, i: (bi, i, 0))
    vec = pl.BlockSpec((1, 1, d), lambda bi, i: (bi, 0, 0))
    return pl.pallas_call(
        _epilogue_kernel,
        grid=(b, s // tm),
        in_specs=[tok(d), vec, tok(w), tok(w), tok(w), tok(w), tok(w), tok(d), tok(d),
                  _const_spec(ones_bd.shape), _const_spec((1, w)), _const_spec((1, w)),
                  _const_spec((w, d)), _const_spec((w, d)), _const_spec((d, d)),
                  _const_spec((1, d)), _const_spec((1, d))],
        out_specs=tok(d),
        out_shape=jax.ShapeDtypeStruct((b, s, d), x.dtype),
        compiler_params=_params("parallel", "parallel"),
    )(x, gate, attn, ga, y, bonus, gb, ma, mb, ones_bd, row(gn_g), row(gn_b),
      w_proj_a.astype(BF16), w_proj_b.astype(BF16), w_out.astype(BF16), row(post_g), row(post_b))


def _layer(x, c, pos, w_ada, b_ada, w_in, q_norm_g, w_uq, kv_norm_g, w_ukv, mu_rwkv, w0, w_decay_up,
           a0, w_iclr_up, k_k, k_a, r_k, gn_g, gn_b, w_proj_a, w_proj_b, w_out, post_g, post_b):
    b, s, d = x.shape
    tm = min(512, s)
    ada = _adaln(c, w_ada, b_ada)
    shift, scale, gate = (ada[:, j * d:(j + 1) * d].reshape(b, 1, d) for j in range(3))
    qt, k, vt, ga, rw, gb, ma, mb = _ln_proj(x, scale, shift, w_in.astype(BF16), pos, q_norm_g, kv_norm_g,
                                             w_uq, w_ukv, mu_rwkv, tm)
    attn = _attention(qt, k, vt)
    y, bonus = _wkv(rw, w0, w_decay_up, a0, w_iclr_up, k_k, k_a, r_k.reshape(-1), min(256, s))
    return _epilogue(x, gate, attn, ga, y, bonus, gb, ma, mb, gn_g, gn_b, w_proj_a, w_proj_b, w_out,
                     post_g, post_b, tm)


def kernel(x, c, positions, w_ada, b_ada, w_in, q_norm_g, w_uq, kv_norm_g, w_ukv, mu_rwkv, w0,
           w_decay_up, a0, w_iclr_up, k_k, k_a, r_k, gn_g, gn_b, w_proj_a, w_proj_b, w_out, post_g,
           post_b):
    pos = positions.astype(F32)[:, None, :]
    for l in range(w_ada.shape[0]):
        x = _layer(x, c, pos, w_ada[l], b_ada[l], w_in[l], q_norm_g[l], w_uq[l], kv_norm_g[l],
                   w_ukv[l], mu_rwkv[l], w0[l], w_decay_up[l], a0[l], w_iclr_up[l], k_k[l], k_a[l],
                   r_k[l], gn_g[l], gn_b[l], w_proj_a[l], w_proj_b[l], w_out[l], post_g[l], post_b[l])
    return x
```

```python
import functools
import math

import jax
import jax.numpy as jnp
import numpy as np
from jax import lax
from jax.experimental import pallas as pl
from jax.experimental.pallas import tpu as pltpu

F32 = jnp.float32
BF16 = jnp.bfloat16

D_MODEL = 1024
LN_EPS = 1e-5
RMS_EPS = 1e-6
GN_EPS = 64e-5

MLA_HEADS = 8
MLA_NOPE = 64
MLA_ROPE = 32
MLA_V = 64
MLA_QK = MLA_NOPE + MLA_ROPE
MLA_Q_RANK = 256
MLA_KV_RANK = 128
MLA_WIDTH = MLA_HEADS * MLA_V
ROPE_THETA = 10000.0
ATTN_CHUNK = 64

RWKV_HEADS = 8
RWKV_HEAD = 64
RWKV_WIDTH = RWKV_HEADS * RWKV_HEAD
DECAY_LORA = 64
ICLR_LORA = 64
RWKV_SHIFT_WIDTH = 3 * RWKV_WIDTH + DECAY_LORA + ICLR_LORA
WKV_CHUNK = 64

DEPTH = 1
ALPHA = (2.0 * DEPTH) ** 0.25

LANES = 128
HEAD_PAD = 128
V_ROWS = 80
KEY_TILE = 256
PACK_HEADS = 2
PACK_LANES = PACK_HEADS * RWKV_HEAD

QKR_WIDTH = 512
VMEM_LIMIT = 56 * 1024 * 1024

NEG_BIG = -1e30


def _const_spec(shape):
    n = len(shape)
    return pl.BlockSpec(shape, lambda *_: (0,) * n)


def _params(*sem):
    return pltpu.CompilerParams(dimension_semantics=sem, vmem_limit_bytes=VMEM_LIMIT)


def _adaln_kernel(c_ref, w_ref, b_ref, o_ref):
    c = c_ref[...]
    sc = c * jax.nn.sigmoid(c)
    o_ref[...] = jnp.dot(sc, w_ref[...], preferred_element_type=F32) + b_ref[...]


def _adaln(c, w_ada, b_ada):
    b = c.shape[0]
    return pl.pallas_call(
        _adaln_kernel,
        out_shape=jax.ShapeDtypeStruct((b, 3 * D_MODEL), F32),
        compiler_params=pltpu.CompilerParams(vmem_limit_bytes=VMEM_LIMIT),
    )(c, w_ada, b_ada.reshape(1, -1))


_PROJ_WIDTHS = (MLA_WIDTH, RWKV_SHIFT_WIDTH, RWKV_WIDTH, D_MODEL, D_MODEL)
_PROJ_SCALES = (0.5, 1.0, 0.5, 0.5, 0.5)
_RWKV_SLICE = 1


def _ln_proj_kernel(x_ref, scale_ref, shift_ref, w_in_ref, pos_ref, inv_ref, qg_ref, kvg_ref, wqp_ref,
                    wqr_ref, wk_ref, wvt_ref, mu_ref, qt_out, k_out, vt_out, *refs):
    out_refs, w_ref, last_row = refs[:-2], refs[-2], refs[-1]

    @pl.when(jnp.logical_and(pl.program_id(0) == 0, pl.program_id(1) == 0))
    def _():
        last_row[...] = jnp.zeros_like(last_row)
        used = MLA_Q_RANK + MLA_KV_RANK + MLA_ROPE
        shift = QKR_WIDTH - used
        lane = lax.broadcasted_iota(jnp.int32, (1, LANES), 1)
        rows = 256
        for r in range(0, D_MODEL, rows):
            rs = slice(r, r + rows)
            aligned = (used // LANES) * LANES
            w_ref[rs, :aligned] = w_in_ref[rs, :aligned]
            edge = w_in_ref[rs, aligned:aligned + LANES]
            w_ref[rs, aligned:QKR_WIDTH] = jnp.where(lane < used - aligned, edge, jnp.zeros_like(edge))
            for c0 in range(QKR_WIDTH, w_ref.shape[1], LANES):
                w_ref[rs, c0:c0 + LANES] = w_in_ref[rs, c0 - shift:c0 - shift + LANES]

    x = x_ref[0]
    xc = x - jnp.mean(x, -1, keepdims=True)
    h = xc * lax.rsqrt(jnp.mean(xc * xc, -1, keepdims=True) + LN_EPS)
    hb = (h * (1.0 + scale_ref[0]) + shift_ref[0]).astype(BF16)
    latent = jnp.dot(hb, w_ref[:, :QKR_WIDTH], preferred_element_type=F32)
    off = QKR_WIDTH
    for n, (o_ref, width, scale) in enumerate(zip(out_refs, _PROJ_WIDTHS, _PROJ_SCALES)):
        acc = jnp.dot(hb, w_ref[:, off:off + width], preferred_element_type=F32)
        if scale != 1.0:
            acc = acc * scale
        if n == _RWKV_SLICE:
            tm = acc.shape[0]
            before = jnp.where(pl.program_id(1) > 0, last_row[0:1, :], 0.0)
            row = lax.broadcasted_iota(jnp.int32, (tm, 1), 0)
            prev = jnp.where(row == 0, before, pltpu.roll(acc, 1, axis=0))
            last_row[0:1, :] = acc[tm - 1:tm, :]
            acc = acc + (prev - acc) * mu_ref[...]
        o_ref[0] = acc.astype(o_ref.dtype)
        off += width
        if n == 1:
            _mla_prep(latent, pos_ref[0], inv_ref[...], qg_ref[...], kvg_ref[...], wqp_ref, wqr_ref,
                      wk_ref, wvt_ref, qt_out, k_out, vt_out)


def _ln_proj(x, scale, shift, w_in, pos, q_norm_g, kv_norm_g, w_uq, w_ukv, mu, tm):
    b, s, d = x.shape
    wqp, wqr, wk, wvt = _mla_weights(w_uq, w_ukv)
    half = MLA_ROPE // 2
    inv = (ROPE_THETA ** (-jnp.arange(0, MLA_ROPE, 2, dtype=F32) / MLA_ROPE)).reshape(half, 1)
    width = MLA_HEADS * HEAD_PAD
    vrows = MLA_HEADS * V_ROWS
    tok = lambda w: pl.BlockSpec((1, tm, w), lambda bi, i: (bi, i, 0))
    vec = pl.BlockSpec((1, 1, d), lambda bi, i: (bi, 0, 0))
    tiles = lambda rows: pl.BlockSpec((1, tm // KEY_TILE, rows, KEY_TILE), lambda bi, i: (bi, i, 0, 0))
    return pl.pallas_call(
        _ln_proj_kernel,
        grid=(b, s // tm),
        in_specs=[tok(d), vec, vec,
                  pl.BlockSpec(w_in.shape, lambda bi, i: (0, 0), pipeline_mode=pl.Buffered(1)),
                  pl.BlockSpec((1, 1, tm), lambda bi, i: (bi, 0, i)),
                  _const_spec((half, 1)), _const_spec((1, MLA_Q_RANK)), _const_spec((1, MLA_KV_RANK)),
                  _const_spec(wqp.shape), _const_spec(wqr.shape), _const_spec(wk.shape),
                  _const_spec(wvt.shape), _const_spec((1, RWKV_SHIFT_WIDTH))],
        out_specs=[tiles(width), tok(width), tiles(vrows)] + [tok(w) for w in _PROJ_WIDTHS],
        out_shape=[jax.ShapeDtypeStruct((b, s // KEY_TILE, width, KEY_TILE), BF16),
                   jax.ShapeDtypeStruct((b, s, width), BF16),
                   jax.ShapeDtypeStruct((b, s // KEY_TILE, vrows, KEY_TILE), BF16)]
                  + [jax.ShapeDtypeStruct((b, s, w), BF16) for w in _PROJ_WIDTHS],
        scratch_shapes=[pltpu.VMEM((d, QKR_WIDTH + sum(_PROJ_WIDTHS)), BF16),
                        pltpu.VMEM((8, RWKV_SHIFT_WIDTH), F32)],
        compiler_params=_params("arbitrary", "arbitrary"),
    )(x, scale, shift, w_in, pos, inv, q_norm_g.reshape(1, -1), kv_norm_g.reshape(1, -1),
      wqp, wqr, wk, wvt, mu.reshape(1, -1))


def _mla_prep(t, pos, inv, qg, kvg, wqp_ref, wqr_ref, wk_ref, wvt_ref, qt_out, k_out, vt_out):
    tm = t.shape[0]
    half = MLA_ROPE // 2
    qc = t[:, :MLA_Q_RANK]
    kvc = t[:, MLA_Q_RANK:MLA_Q_RANK + MLA_KV_RANK]
    kr = t[:, MLA_Q_RANK + MLA_KV_RANK:]
    qn = qc * lax.rsqrt(jnp.mean(qc * qc, -1, keepdims=True) + RMS_EPS) * qg
    kvn = kvc * lax.rsqrt(jnp.mean(kvc * kvc, -1, keepdims=True) + RMS_EPS) * kvg
    qn_t = qn.T.astype(BF16)
    kvn_t = kvn.T.astype(BF16)
    kr_t = kr.T

    ang = inv * pos
    cos_h = jnp.cos(ang)
    sin_h = jnp.sin(ang)
    cos_r = jnp.concatenate([cos_h, cos_h], axis=0)
    sin_r = jnp.concatenate([sin_h, sin_h], axis=0)

    plain_t = jnp.dot(wqp_ref[...], qn_t, preferred_element_type=F32)
    rot_t = jnp.dot(wqr_ref[...], qn_t, preferred_element_type=F32)
    scale = MLA_QK ** -0.5 * math.log2(math.e)
    zpad = jnp.zeros((HEAD_PAD - MLA_QK, tm), BF16)
    for h in range(MLA_HEADS):
        base = h * HEAD_PAD
        nope = (plain_t[base:base + MLA_NOPE] * scale).astype(BF16)
        pe = ((plain_t[base + MLA_NOPE:base + MLA_QK] * cos_r
               + rot_t[h * MLA_ROPE:(h + 1) * MLA_ROPE] * sin_r) * scale).astype(BF16)
        q_t = jnp.concatenate([nope, pe, zpad], axis=0)
        for kt in range(tm // KEY_TILE):
            qt_out[0, kt, base:base + HEAD_PAD, :] = q_t[:, kt * KEY_TILE:(kt + 1) * KEY_TILE]

    k1 = kr_t[:half]
    k2 = kr_t[half:MLA_ROPE]
    kpe_t = jnp.concatenate([jnp.zeros((MLA_NOPE, tm), F32),
                             k1 * cos_h - k2 * sin_h, k1 * sin_h + k2 * cos_h,
                             jnp.zeros((HEAD_PAD - MLA_QK, tm), F32)], axis=0)
    kpe = kpe_t.T
    ka = jnp.dot(kvn.astype(BF16), wk_ref[...], preferred_element_type=F32)
    for h in range(MLA_HEADS):
        sl = slice(h * HEAD_PAD, (h + 1) * HEAD_PAD)
        k_out[0, :, sl] = (ka[:, sl] + kpe).astype(BF16)

    vt = jnp.dot(wvt_ref[...], kvn_t, preferred_element_type=F32)
    vrow = lax.broadcasted_iota(jnp.int32, (MLA_HEADS * V_ROWS, 1), 0) % V_ROWS
    vt = (vt + jnp.where(vrow == MLA_V, 1.0, 0.0)).astype(BF16)
    for kt in range(tm // KEY_TILE):
        vt_out[0, kt] = vt[:, kt * KEY_TILE:(kt + 1) * KEY_TILE]


def _mla_weights(w_uq, w_ukv):
    half = MLA_ROPE // 2
    wq = w_uq.reshape(MLA_Q_RANK, MLA_HEADS, MLA_QK)
    zq = jnp.zeros((MLA_Q_RANK, MLA_HEADS, HEAD_PAD - MLA_QK), F32)
    wq_plain_t = jnp.concatenate([wq, zq], -1).reshape(MLA_Q_RANK, -1).T.astype(BF16)
    t1 = wq[:, :, MLA_NOPE:MLA_NOPE + half]
    t2 = wq[:, :, MLA_NOPE + half:]
    wq_rot_t = jnp.concatenate([-t2, t1], -1).reshape(MLA_Q_RANK, -1).T.astype(BF16)

    wkv = w_ukv.reshape(MLA_KV_RANK, MLA_HEADS, MLA_NOPE + MLA_V)
    zk = jnp.zeros((MLA_KV_RANK, MLA_HEADS, HEAD_PAD - MLA_NOPE), F32)
    wk = jnp.concatenate([wkv[:, :, :MLA_NOPE], zk], -1).reshape(MLA_KV_RANK, -1).astype(BF16)
    zv = jnp.zeros((MLA_KV_RANK, MLA_HEADS, V_ROWS - MLA_V), F32)
    wvt = jnp.concatenate([wkv[:, :, MLA_NOPE:], zv], -1).reshape(MLA_KV_RANK, -1).T.astype(BF16)
    return wq_plain_t, wq_rot_t, wk, wvt


ATTN_SUBTILES = 2
ATTN_PAIRS_PER_ITER = 2


def _attn_kernel(qt_ref, k_ref, vt_ref, o_ref, s_buf):
    ts = KEY_TILE
    nsub = ATTN_SUBTILES
    n_qt = o_ref.shape[1] // (nsub * ts)
    key_chunk = lax.broadcasted_iota(jnp.int32, (ts, ts), 0) // ATTN_CHUNK
    qry_chunk = lax.broadcasted_iota(jnp.int32, (ts, ts), 1) // ATTN_CHUNK
    diag_mask = key_chunk <= qry_chunk
    lanes = [slice(hh * HEAD_PAD, (hh + 1) * HEAD_PAD) for hh in range(2)]
    chains = [(hh, sb) for hh in range(2) for sb in range(nsub)]

    def scores(i, j, slot, first_sub, masked_sub):
        if isinstance(j, int):
            start = j * ts
        else:
            start = pl.multiple_of(j * ts, ts)
        kj = [k_ref[0, pl.ds(start, ts), lanes[hh]] for hh in range(2)]
        tile_max = {}
        for n, ch in enumerate(chains):
            if ch[1] >= first_sub:
                q_t = qt_ref[0, i * nsub + ch[1], lanes[ch[0]], :]
                s = jnp.dot(kj[ch[0]], q_t, preferred_element_type=F32)
                if ch[1] == masked_sub:
                    s = jnp.where(diag_mask, s, NEG_BIG)
                s_buf[slot, n] = s
                tile_max[ch] = jnp.max(s, axis=0, keepdims=True)
        return tile_max

    def consume(j, slot, carry, tile_max, first_sub):
        vj = [vt_ref[0, j, hh * V_ROWS:(hh + 1) * V_ROWS, :] for hh in range(2)]
        active = [(n, ch) for n, ch in enumerate(chains) if ch[1] >= first_sub]
        m_new = {ch: jnp.maximum(carry[ch][0], tile_max[ch]) for _, ch in active}
        alpha = {ch: jnp.exp2(carry[ch][0] - m_new[ch]) for _, ch in active}
        out = dict(carry)
        for n, ch in active:
            p = jnp.exp2(s_buf[slot, n] - m_new[ch]).astype(BF16)
            pv = jnp.dot(vj[ch[0]], p, preferred_element_type=F32)
            out[ch] = (m_new[ch], carry[ch][1] * alpha[ch] + pv)
        return out

    head_max = scores(0, 0, 0, 0, 0)
    for i in range(n_qt):
        sa, sb_ = (0, 1) if i % 2 == 0 else (2, 3)
        next_sa = 2 if i % 2 == 0 else 0

        def pair_step(t, carry, max_a, on_diag, i=i, sa=sa, sb_=sb_):
            max_b = scores(i, 2 * t + 1, sb_, 0, -1)
            carry = consume(2 * t, sa, carry, max_a, 0)
            max_a = scores(i, 2 * t + 2, sa, 0, 0 if on_diag else -1)
            return consume(2 * t + 1, sb_, carry, max_b, 0), max_a

        def loop_body(t, flat):
            carry = {ch: (flat[3 * n], flat[3 * n + 1]) for n, ch in enumerate(chains)}
            max_a = {ch: flat[3 * n + 2] for n, ch in enumerate(chains)}
            for u in range(ATTN_PAIRS_PER_ITER):
                carry, max_a = pair_step(ATTN_PAIRS_PER_ITER * t + u, carry, max_a, False)
            return tuple(v for ch in chains for v in (*carry[ch], max_a[ch]))

        n_iter, n_rest = divmod(max(i - 1, 0), ATTN_PAIRS_PER_ITER)
        init = tuple(v for ch in chains
                     for v in (jnp.full((1, ts), NEG_BIG, F32), jnp.zeros((V_ROWS, ts), F32), head_max[ch]))
        flat = lax.fori_loop(0, n_iter, loop_body, init)
        carry = {ch: (flat[3 * n], flat[3 * n + 1]) for n, ch in enumerate(chains)}
        max_a = {ch: flat[3 * n + 2] for n, ch in enumerate(chains)}
        for u in range(n_rest):
            carry, max_a = pair_step(ATTN_PAIRS_PER_ITER * n_iter + u, carry, max_a, False)
        if i >= 1:
            carry, max_a = pair_step(i - 1, carry, max_a, True)
        max_b = scores(i, 2 * i + 1, sb_, 1, 1)
        if i + 1 < n_qt:
            head_max = scores(i + 1, 0, next_sa, 0, -1)
        carry = consume(2 * i, sa, carry, max_a, 0)
        carry = consume(2 * i + 1, sb_, carry, max_b, 1)

        for sb in range(nsub):
            normed = []
            for hh in range(2):
                acc = carry[(hh, sb)][1]
                normed.append(acc[:MLA_V] / acc[MLA_V:MLA_V + 1])
            out_t = jnp.concatenate(normed, axis=0)
            row0 = (i * nsub + sb) * ts
            o_ref[0, row0:row0 + ts, :] = out_t.T.astype(o_ref.dtype)


def _attention(qt, k, vt):
    b, s, _ = k.shape
    pairs = MLA_HEADS // 2
    nsub = ATTN_SUBTILES
    return pl.pallas_call(
        _attn_kernel,
        grid=(b, pairs),
        in_specs=[pl.BlockSpec((1, s // KEY_TILE, 2 * HEAD_PAD, KEY_TILE), lambda bi, g: (bi, 0, g, 0)),
                  pl.BlockSpec((1, s, 2 * HEAD_PAD), lambda bi, g: (bi, 0, g)),
                  pl.BlockSpec((1, s // KEY_TILE, 2 * V_ROWS, KEY_TILE), lambda bi, g: (bi, 0, g, 0))],
        out_specs=pl.BlockSpec((1, s, 2 * MLA_V), lambda bi, g: (bi, 0, g)),
        out_shape=jax.ShapeDtypeStruct((b, s, MLA_WIDTH), BF16),
        scratch_shapes=[pltpu.VMEM((4, 2 * nsub, KEY_TILE, KEY_TILE), F32)],
        compiler_params=_params("parallel", "parallel"),
    )(qt, k, vt)


def _lane_head(width):
    return lax.broadcasted_iota(jnp.int32, (1, width), 1) // RWKV_HEAD


def _bd_stack(x):
    head = _lane_head(x.shape[1])
    xb = x.astype(BF16)
    return jnp.concatenate([jnp.where(head == h, xb, jnp.zeros_like(xb)) for h in range(PACK_HEADS)], axis=0)


def _pdot(a, b):
    return jnp.dot(a.astype(BF16), _bd_stack(b), preferred_element_type=F32)


def _pdot2(a, b1, b2):
    rhs = jnp.concatenate([_bd_stack(b1), _bd_stack(b2)], axis=1)
    out = jnp.dot(a.astype(BF16), rhs, preferred_element_type=F32)
    return out[:, :PACK_LANES], out[:, PACK_LANES:]


def _pdot_nt(a, b):
    return lax.dot_general(a.astype(BF16), _bd_stack(b), (((1,), (1,)), ((), ())),
                           preferred_element_type=F32)


def _split2(x):
    x1 = x.astype(BF16)
    x2 = (x - x1.astype(F32)).astype(BF16)
    return x1, x2


MXU_TILE = 256


def _head_ones():
    hid = np.arange(MXU_TILE) // RWKV_HEAD
    return jnp.asarray((hid[:, None] == hid[None, :]).astype(np.float32), BF16)


def _head_sums(t, ones_tile):
    tb = t.astype(BF16)
    return jnp.concatenate(
        [jnp.dot(tb[:, o:o + MXU_TILE], ones_tile, preferred_element_type=F32)
         for o in range(0, t.shape[1], MXU_TILE)], axis=1)


def _wkv_kernel(rw_ref, wlora_ref, w0_ref, a0_ref, kk_ref, ka_ref, rk_ref,
                ones_ref, tri_ref, y_ref, bonus_ref, s_ref, rhat_s, y0_s, p_s, q_s, *, tm, nt):
    g = pl.program_id(0)
    c = WKV_CHUNK
    w = RWKV_WIDTH
    nch = tm // c
    packs = range(w // PACK_LANES)
    lanes_of = lambda gk: slice(gk * PACK_LANES, (gk + 1) * PACK_LANES)

    @pl.when(g == 0)
    def _():
        for ref in (s_ref, rhat_s, y0_s, p_s, q_s):
            ref[...] = jnp.zeros_like(ref)

    scan_starts_sequence = (jnp.maximum(g - 1, 0) % nt) == 0
    scan = {"state": {gk: jnp.where(scan_starts_sequence, 0.0, s_ref[:, lanes_of(gk)]) for gk in packs},
            "chunk": 0}

    def advance_scan():
        cc = scan["chunk"]
        if cc >= nch:
            return
        state = scan["state"]
        rs = slice(cc * c, (cc + 1) * c)
        for gk in packs:
            ls = lanes_of(gk)
            y_ref[0, rs, ls] = (_pdot_nt(rhat_s[rs, ls], state[gk]) + y0_s[rs, ls]).astype(y_ref.dtype)
        state = {gk: jnp.dot(state[gk].astype(BF16), p_s[cc * PACK_LANES:(cc + 1) * PACK_LANES, lanes_of(gk)],
                             preferred_element_type=F32) + q_s[rs, lanes_of(gk)] for gk in packs}
        scan["state"] = state
        scan["chunk"] = cc + 1
        if cc + 1 == nch:
            for gk in packs:
                s_ref[:, lanes_of(gk)] = state[gk]

    u = rw_ref[0].astype(F32)
    r = u[:, :w]
    k = u[:, w:2 * w]
    v = u[:, 2 * w:3 * w]
    lora_in = u[:, 3 * w:]
    lane = lax.broadcasted_iota(jnp.int32, (1, DECAY_LORA + ICLR_LORA), 1)
    lora_in = jnp.where(lane < DECAY_LORA, jnp.tanh(lora_in), lora_in)
    lora = jnp.dot(lora_in.astype(BF16), wlora_ref[...], preferred_element_type=F32)
    c_lw = -0.5 * math.exp(-0.5) * math.log2(math.e)
    lw = c_lw + c_lw * jnp.tanh(w0_ref[...] + lora[:, :w])
    a = 0.5 + 0.5 * jnp.tanh(a0_ref[...] + lora[:, w:])

    ones_tile = ones_ref[...]
    headsum = lambda t: _head_sums(t, ones_tile)
    kk = k * kk_ref[...]
    kk = kk * lax.rsqrt(jnp.maximum(headsum(kk * kk), 1e-24))
    k = k * (1.0 + (a - 1.0) * ka_ref[...])
    bonus_ref[0] = (headsum(r * k * rk_ref[...]) * v).astype(bonus_ref.dtype)
    a_vec = -kk
    b_vec = kk * a

    tri = tri_ref[...]
    cum = sum(jnp.dot(tri, part, preferred_element_type=F32) for part in _split2(lw))

    t_idx = lax.broadcasted_iota(jnp.int32, (c, PACK_LANES), 0)
    s_idx = lax.broadcasted_iota(jnp.int32, (c, PACK_LANES), 1) % RWKV_HEAD
    strict = s_idx < t_idx
    incl = s_idx <= t_idx
    eye = (s_idx == t_idx).astype(F32)
    prow = lax.broadcasted_iota(jnp.int32, (PACK_LANES, PACK_LANES), 0)
    pcol = lax.broadcasted_iota(jnp.int32, (PACK_LANES, PACK_LANES), 1)
    same_head = (prow // RWKV_HEAD) == (pcol // RWKV_HEAD)
    on_diag = prow == pcol
    head = _lane_head(PACK_LANES)

    cum_last = jnp.concatenate(
        [jnp.broadcast_to(cum[(cc + 1) * c - 1:(cc + 1) * c, :], (c, w)) for cc in range(nch)], axis=0)
    e_pos = jnp.exp2(cum)
    e_neg = jnp.exp2(-cum)
    e_last = jnp.exp2(cum_last - cum)
    at_all = a_vec * jnp.exp2(cum - lw)
    bt_all = b_vec * e_neg
    kt_all = k * e_neg
    rt_all = r * e_pos
    bl_all = b_vec * e_last
    kl_all = k * e_last
    w_end_all = jnp.exp2(cum_last)

    probs = [(cc, gk) for cc in range(nch) for gk in packs]
    sl = {pr: (slice(pr[0] * c, (pr[0] + 1) * c), slice(pr[1] * PACK_LANES, (pr[1] + 1) * PACK_LANES))
          for pr in probs}
    at = {pr: at_all[sl[pr]] for pr in probs}
    rt = {pr: rt_all[sl[pr]] for pr in probs}
    vv = {pr: v[sl[pr]] for pr in probs}
    m_bk = {pr: lax.dot_general(
        jnp.concatenate([at[pr], rt[pr]], axis=0).astype(BF16),
        jnp.concatenate([_bd_stack(bt_all[sl[pr]]), _bd_stack(kt_all[sl[pr]])], axis=0),
        (((1,), (1,)), ((), ())), preferred_element_type=F32) for pr in probs}
    advance_scan()
    m_b = {pr: m_bk[pr][:, :PACK_LANES] for pr in probs}
    m_k = {pr: m_bk[pr][:, PACK_LANES:] for pr in probs}
    l_ab = {pr: jnp.where(strict, m_b[pr][:c], 0.0) for pr in probs}
    t_inv = {pr: eye + l_ab[pr] for pr in probs}
    x = {pr: _pdot(l_ab[pr], l_ab[pr]) for pr in probs}
    levels = int(math.log2(c)) - 1
    for lvl in range(levels):
        advance_scan()
        if lvl < levels - 1:
            tx = {pr: _pdot2(x[pr], t_inv[pr], x[pr]) for pr in probs}
            t_inv = {pr: t_inv[pr] + tx[pr][0] for pr in probs}
            x = {pr: tx[pr][1] for pr in probs}
        else:
            t_inv = {pr: t_inv[pr] + _pdot(x[pr], t_inv[pr]) for pr in probs}
    akv = {pr: _pdot(jnp.where(strict, m_k[pr][:c], 0.0), vv[pr]) for pr in probs}
    au = {pr: _pdot2(t_inv[pr], at[pr], akv[pr]) for pr in probs}
    a_hat = {pr: au[pr][0] for pr in probs}
    u0 = {pr: au[pr][1] for pr in probs}
    while scan["chunk"] < nch:
        advance_scan()
    for pr in probs:
        rs, ls = sl[pr]
        m_rb = jnp.where(incl, m_b[pr][c:], 0.0)
        m_rk = jnp.where(incl, m_k[pr][c:], 0.0)
        r_add, y_add = _pdot2(m_rb, a_hat[pr], u0[pr])
        rhat_s[rs, ls] = (rt[pr] + r_add).astype(rhat_s.dtype)
        y0_s[rs, ls] = y_add + _pdot(m_rk, vv[pr])
    for pr in probs:
        rs, ls = sl[pr]
        bl = bl_all[sl[pr]]
        p_full = jnp.dot(a_hat[pr].T.astype(BF16), bl.astype(BF16), preferred_element_type=F32)
        w_end = w_end_all[rs.start:rs.start + 1, ls]
        p_bd = jnp.where(same_head, p_full, 0.0) + jnp.where(on_diag, w_end, 0.0)
        p_s[pr[0] * PACK_LANES:(pr[0] + 1) * PACK_LANES, ls] = p_bd.astype(p_s.dtype)
        uv_t = jnp.concatenate([u0[pr], vv[pr]], axis=0).T
        bk = jnp.concatenate([bl, kl_all[sl[pr]]], axis=0).astype(BF16)
        f = jnp.dot(uv_t.astype(BF16), bk, preferred_element_type=F32)
        q_s[rs, ls] = sum(jnp.where(head == h, f[h * RWKV_HEAD:(h + 1) * RWKV_HEAD], 0.0)
                          for h in range(PACK_HEADS))


def _wkv(rw, w0, w_decay_up, a0, w_iclr_up, k_k, k_a, r_k, tm):
    b, s, width = rw.shape
    w = RWKV_WIDTH
    nt = s // tm
    n_tiles = b * nt
    wlora = (0.5 * jnp.concatenate(
        [jnp.concatenate([w_decay_up, jnp.zeros((DECAY_LORA, w), F32)], 1),
         jnp.concatenate([jnp.zeros((ICLR_LORA, w), F32), w_iclr_up], 1)], 0)).astype(BF16)
    w0 = 0.5 * w0
    a0 = 0.5 * a0
    ones_bd = _head_ones()
    tid = np.arange(tm)
    tri = jnp.asarray(((tid[:, None] >= tid[None, :]) &
                       (tid[:, None] // WKV_CHUNK == tid[None, :] // WKV_CHUNK)).astype(np.float32), BF16)
    row = lambda t: t.reshape(1, -1)
    prep_tile = lambda g: jnp.minimum(g, n_tiles - 1)
    scan_tile = lambda g: jnp.maximum(g - 1, 0)
    rw_spec = pl.BlockSpec((1, tm, width), lambda g: (prep_tile(g) // nt, prep_tile(g) % nt, 0))
    y_spec = pl.BlockSpec((1, tm, w), lambda g: (scan_tile(g) // nt, scan_tile(g) % nt, 0))
    bonus_spec = pl.BlockSpec((1, tm, w), lambda g: (prep_tile(g) // nt, prep_tile(g) % nt, 0))
    n_chunks = tm // WKV_CHUNK
    return pl.pallas_call(
        functools.partial(_wkv_kernel, tm=tm, nt=nt),
        grid=(n_tiles + 1,),
        in_specs=[rw_spec, _const_spec(wlora.shape)]
                 + [_const_spec((1, w))] * 5 + [_const_spec(ones_bd.shape), _const_spec(tri.shape)],
        out_specs=[y_spec, bonus_spec],
        out_shape=[jax.ShapeDtypeStruct((b, s, w), BF16),
                   jax.ShapeDtypeStruct((b, s, w), BF16)],
        scratch_shapes=[pltpu.VMEM((RWKV_HEAD, w), F32),
                        pltpu.VMEM((tm, w), BF16),
                        pltpu.VMEM((tm, w), F32),
                        pltpu.VMEM((n_chunks * PACK_LANES, w), BF16),
                        pltpu.VMEM((tm, w), F32)],
        compiler_params=_params("arbitrary"),
    )(rw, wlora, row(w0), row(a0), row(k_k), row(k_a), row(r_k), ones_bd, tri)


def _epilogue_kernel(x_ref, gate_ref, attn_ref, ga_ref, y_ref, bonus_ref, gb_ref, ma_ref, mb_ref,
                     ones_ref, gng_ref, gnb_ref, wpa_ref, wpb_ref, wout_ref, pg_ref, pb_ref, o_ref):
    silu_half = lambda th: th + th * jnp.tanh(th)
    ones_tile = ones_ref[...]
    headmean = lambda t: _head_sums(t, ones_tile) * (1.0 / RWKV_HEAD)

    y = y_ref[0].astype(F32)
    yc = y - headmean(y)
    yn = yc * lax.rsqrt(headmean(yc * yc) + GN_EPS)
    yb = yn * gng_ref[...] + gnb_ref[...] + bonus_ref[0].astype(F32)
    yb = yb * silu_half(gb_ref[0].astype(F32))
    ya = attn_ref[0].astype(F32) * silu_half(ga_ref[0].astype(F32))
    ya_p = jnp.dot(ya.astype(BF16), wpa_ref[...], preferred_element_type=F32)
    yb_p = jnp.dot(yb.astype(BF16), wpb_ref[...], preferred_element_type=F32)
    merged2 = ((1.0 + jnp.tanh(ma_ref[0].astype(F32))) * ya_p
               + (1.0 + jnp.tanh(mb_ref[0].astype(F32))) * yb_p)
    sub2 = jnp.dot(merged2.astype(BF16), wout_ref[...], preferred_element_type=F32)
    z = ALPHA * x_ref[0] + (0.5 * (1.0 + gate_ref[0])) * sub2
    zc = z - jnp.mean(z, -1, keepdims=True)
    zn = zc * lax.rsqrt(jnp.mean(zc * zc, -1, keepdims=True) + LN_EPS)
    o_ref[0] = (zn * pg_ref[...] + pb_ref[...]).astype(o_ref.dtype)


def _epilogue(x, gate, attn, ga, y, bonus, gb, ma, mb, gn_g, gn_b, w_proj_a, w_proj_b, w_out,
              post_g, post_b, tm):
    b, s, d = x.shape
    w = RWKV_WIDTH
    ones_bd = _head_ones()
    row = lambda t: t.reshape(1, -1)
    tok = lambda wd: pl.BlockSpec((1, tm, wd), lambda bi, i: (bi, i, 0))
    vec = pl.BlockSpec((1, 1, d), lambda bi, i: (bi, 0, 0))
    return pl.pallas_call(
        _epilogue_kernel,
        grid=(b, s // tm),
        in_specs=[tok(d), vec, tok(w), tok(w), tok(w), tok(w), tok(w), tok(d), tok(d),
                  _const_spec(ones_bd.shape), _const_spec((1, w)), _const_spec((1, w)),
                  _const_spec((w, d)), _const_spec((w, d)), _const_spec((d, d)),
                  _const_spec((1, d)), _const_spec((1, d))],
        out_specs=tok(d),
        out_shape=jax.ShapeDtypeStruct((b, s, d), x.dtype),
        compiler_params=_params("parallel", "parallel"),
    )(x, gate, attn, ga, y, bonus, gb, ma, mb, ones_bd, row(gn_g), row(gn_b),
      w_proj_a.astype(BF16), w_proj_b.astype(BF16), w_out.astype(BF16), row(post_g), row(post_b))


def _layer(x, c, pos, w_ada, b_ada, w_in, q_norm_g, w_uq, kv_norm_g, w_ukv, mu_rwkv, w0, w_decay_up,
           a0, w_iclr_up, k_k, k_a, r_k, gn_g, gn_b, w_proj_a, w_proj_b, w_out, post_g, post_b):
    b, s, d = x.shape
    tm = min(512, s)
    ada = _adaln(c, w_ada, b_ada)
    shift, scale, gate = (ada[:, j * d:(j + 1) * d].reshape(b, 1, d) for j in range(3))
    qt, k, vt, ga, rw, gb, ma, mb = _ln_proj(x, scale, shift, w_in.astype(BF16), pos, q_norm_g, kv_norm_g,
                                             w_uq, w_ukv, mu_rwkv, tm)
    attn = _attention(qt, k, vt)
    y, bonus = _wkv(rw, w0, w_decay_up, a0, w_iclr_up, k_k, k_a, r_k.reshape(-1), min(512, s))
    return _epilogue(x, gate, attn, ga, y, bonus, gb, ma, mb, gn_g, gn_b, w_proj_a, w_proj_b, w_out,
                     post_g, post_b, min(1024, s))


def kernel(x, c, positions, w_ada, b_ada, w_in, q_norm_g, w_uq, kv_norm_g, w_ukv, mu_rwkv, w0,
           w_decay_up, a0, w_iclr_up, k_k, k_a, r_k, gn_g, gn_b, w_proj_a, w_proj_b, w_out, post_g,
           post_b):
    pos = positions.astype(F32)[:, None, :]
    for l in range(w_ada.shape[0]):
        x = _layer(x, c, pos, w_ada[l], b_ada[l], w_in[l], q_norm_g[l], w_uq[l], kv_norm_g[l],
                   w_ukv[l], mu_rwkv[l], w0[l], w_decay_up[l], a0[l], w_iclr_up[l], k_k[l], k_a[l],
                   r_k[l], gn_g[l], gn_b[l], w_proj_a[l], w_proj_b[l], w_out[l], post_g[l], post_b[l])
    return x
```

```python
import functools
import math

import jax
import jax.numpy as jnp
import numpy as np
from jax import lax
from jax.experimental import pallas as pl
from jax.experimental.pallas import tpu as pltpu

F32 = jnp.float32
BF16 = jnp.bfloat16

D_MODEL = 1024
LN_EPS = 1e-5
RMS_EPS = 1e-6
GN_EPS = 64e-5

MLA_HEADS = 8
MLA_NOPE = 64
MLA_ROPE = 32
MLA_V = 64
MLA_QK = MLA_NOPE + MLA_ROPE
MLA_Q_RANK = 256
MLA_KV_RANK = 128
MLA_WIDTH = MLA_HEADS * MLA_V
ROPE_THETA = 10000.0
ATTN_CHUNK = 64

RWKV_HEADS = 8
RWKV_HEAD = 64
RWKV_WIDTH = RWKV_HEADS * RWKV_HEAD
DECAY_LORA = 64
ICLR_LORA = 64
RWKV_SHIFT_WIDTH = 3 * RWKV_WIDTH + DECAY_LORA + ICLR_LORA
WKV_CHUNK = 64

DEPTH = 1
ALPHA = (2.0 * DEPTH) ** 0.25

LANES = 128
HEAD_PAD = 128
V_ROWS = 80
KEY_TILE = 256
PACK_HEADS = 2
PACK_LANES = PACK_HEADS * RWKV_HEAD

QKR_WIDTH = 512
VMEM_LIMIT = 56 * 1024 * 1024

NEG_BIG = -1e30


def _const_spec(shape):
    n = len(shape)
    return pl.BlockSpec(shape, lambda *_: (0,) * n)


def _params(*sem):
    return pltpu.CompilerParams(dimension_semantics=sem, vmem_limit_bytes=VMEM_LIMIT)


def _adaln_kernel(c_ref, w_ref, b_ref, o_ref):
    c = c_ref[...]
    sc = c * jax.nn.sigmoid(c)
    o_ref[...] = jnp.dot(sc, w_ref[...], preferred_element_type=F32) + b_ref[...]


def _adaln(c, w_ada, b_ada):
    b = c.shape[0]
    return pl.pallas_call(
        _adaln_kernel,
        out_shape=jax.ShapeDtypeStruct((b, 3 * D_MODEL), F32),
        compiler_params=pltpu.CompilerParams(vmem_limit_bytes=VMEM_LIMIT),
    )(c, w_ada, b_ada.reshape(1, -1))


_PROJ_WIDTHS = (MLA_WIDTH, RWKV_SHIFT_WIDTH, RWKV_WIDTH, D_MODEL, D_MODEL)
_PROJ_SCALES = (0.5, 1.0, 0.5, 0.5, 0.5)
_RWKV_SLICE = 1


def _ln_proj_kernel(x_ref, scale_ref, shift_ref, w_in_ref, pos_ref, inv_ref, qg_ref, kvg_ref, wqp_ref,
                    wqr_ref, wk_ref, wvt_ref, mu_ref, qt_out, k_out, vt_out, *refs):
    out_refs, w_ref, last_row = refs[:-2], refs[-2], refs[-1]

    @pl.when(jnp.logical_and(pl.program_id(0) == 0, pl.program_id(1) == 0))
    def _():
        last_row[...] = jnp.zeros_like(last_row)
        used = MLA_Q_RANK + MLA_KV_RANK + MLA_ROPE
        shift = QKR_WIDTH - used
        lane = lax.broadcasted_iota(jnp.int32, (1, LANES), 1)
        rows = 256
        for r in range(0, D_MODEL, rows):
            rs = slice(r, r + rows)
            aligned = (used // LANES) * LANES
            w_ref[rs, :aligned] = w_in_ref[rs, :aligned]
            edge = w_in_ref[rs, aligned:aligned + LANES]
            w_ref[rs, aligned:QKR_WIDTH] = jnp.where(lane < used - aligned, edge, jnp.zeros_like(edge))
            for c0 in range(QKR_WIDTH, w_ref.shape[1], LANES):
                w_ref[rs, c0:c0 + LANES] = w_in_ref[rs, c0 - shift:c0 - shift + LANES]

    x = x_ref[0]
    xc = x - jnp.mean(x, -1, keepdims=True)
    h = xc * lax.rsqrt(jnp.mean(xc * xc, -1, keepdims=True) + LN_EPS)
    hb = (h * (1.0 + scale_ref[0]) + shift_ref[0]).astype(BF16)
    latent = jnp.dot(hb, w_ref[:, :QKR_WIDTH], preferred_element_type=F32)
    off = QKR_WIDTH
    for n, (o_ref, width, scale) in enumerate(zip(out_refs, _PROJ_WIDTHS, _PROJ_SCALES)):
        acc = jnp.dot(hb, w_ref[:, off:off + width], preferred_element_type=F32)
        if scale != 1.0:
            acc = acc * scale
        if n == _RWKV_SLICE:
            tm = acc.shape[0]
            before = jnp.where(pl.program_id(1) > 0, last_row[0:1, :], 0.0)
            row = lax.broadcasted_iota(jnp.int32, (tm, 1), 0)
            prev = jnp.where(row == 0, before, pltpu.roll(acc, 1, axis=0))
            last_row[0:1, :] = acc[tm - 1:tm, :]
            acc = acc + (prev - acc) * mu_ref[...]
        o_ref[0] = acc.astype(o_ref.dtype)
        off += width
        if n == 1:
            _mla_prep(latent, pos_ref[0], inv_ref[...], qg_ref[...], kvg_ref[...], wqp_ref, wqr_ref,
                      wk_ref, wvt_ref, qt_out, k_out, vt_out)


def _ln_proj(x, scale, shift, w_in, pos, q_norm_g, kv_norm_g, w_uq, w_ukv, mu, tm):
    b, s, d = x.shape
    wqp, wqr, wk, wvt = _mla_weights(w_uq, w_ukv)
    half = MLA_ROPE // 2
    inv = (ROPE_THETA ** (-jnp.arange(0, MLA_ROPE, 2, dtype=F32) / MLA_ROPE)).reshape(half, 1)
    width = MLA_HEADS * HEAD_PAD
    vrows = MLA_HEADS * V_ROWS
    tok = lambda w: pl.BlockSpec((1, tm, w), lambda bi, i: (bi, i, 0))
    vec = pl.BlockSpec((1, 1, d), lambda bi, i: (bi, 0, 0))
    tiles = lambda rows: pl.BlockSpec((1, tm // KEY_TILE, rows, KEY_TILE), lambda bi, i: (bi, i, 0, 0))
    return pl.pallas_call(
        _ln_proj_kernel,
        grid=(b, s // tm),
        in_specs=[tok(d), vec, vec,
                  pl.BlockSpec(w_in.shape, lambda bi, i: (0, 0), pipeline_mode=pl.Buffered(1)),
                  pl.BlockSpec((1, 1, tm), lambda bi, i: (bi, 0, i)),
                  _const_spec((half, 1)), _const_spec((1, MLA_Q_RANK)), _const_spec((1, MLA_KV_RANK)),
                  _const_spec(wqp.shape), _const_spec(wqr.shape), _const_spec(wk.shape),
                  _const_spec(wvt.shape), _const_spec((1, RWKV_SHIFT_WIDTH))],
        out_specs=[tiles(width), tok(width), tiles(vrows)] + [tok(w) for w in _PROJ_WIDTHS],
        out_shape=[jax.ShapeDtypeStruct((b, s // KEY_TILE, width, KEY_TILE), BF16),
                   jax.ShapeDtypeStruct((b, s, width), BF16),
                   jax.ShapeDtypeStruct((b, s // KEY_TILE, vrows, KEY_TILE), BF16)]
                  + [jax.ShapeDtypeStruct((b, s, w), BF16) for w in _PROJ_WIDTHS],
        scratch_shapes=[pltpu.VMEM((d, QKR_WIDTH + sum(_PROJ_WIDTHS)), BF16),
                        pltpu.VMEM((8, RWKV_SHIFT_WIDTH), F32)],
        compiler_params=_params("arbitrary", "arbitrary"),
    )(x, scale, shift, w_in, pos, inv, q_norm_g.reshape(1, -1), kv_norm_g.reshape(1, -1),
      wqp, wqr, wk, wvt, mu.reshape(1, -1))


def _mla_prep(t, pos, inv, qg, kvg, wqp_ref, wqr_ref, wk_ref, wvt_ref, qt_out, k_out, vt_out):
    tm = t.shape[0]
    half = MLA_ROPE // 2
    qc = t[:, :MLA_Q_RANK]
    kvc = t[:, MLA_Q_RANK:MLA_Q_RANK + MLA_KV_RANK]
    kr = t[:, MLA_Q_RANK + MLA_KV_RANK:]
    qn = qc * lax.rsqrt(jnp.mean(qc * qc, -1, keepdims=True) + RMS_EPS) * qg
    kvn = kvc * lax.rsqrt(jnp.mean(kvc * kvc, -1, keepdims=True) + RMS_EPS) * kvg
    qn_t = qn.T.astype(BF16)
    kvn_t = kvn.T.astype(BF16)
    kr_t = kr.T

    ang = inv * pos
    cos_h = jnp.cos(ang)
    sin_h = jnp.sin(ang)
    cos_r = jnp.concatenate([cos_h, cos_h], axis=0)
    sin_r = jnp.concatenate([sin_h, sin_h], axis=0)

    plain_t = jnp.dot(wqp_ref[...], qn_t, preferred_element_type=F32)
    rot_t = jnp.dot(wqr_ref[...], qn_t, preferred_element_type=F32)
    scale = MLA_QK ** -0.5 * math.log2(math.e)
    zpad = jnp.zeros((HEAD_PAD - MLA_QK, tm), BF16)
    for h in range(MLA_HEADS):
        base = h * HEAD_PAD
        nope = (plain_t[base:base + MLA_NOPE] * scale).astype(BF16)
        pe = ((plain_t[base + MLA_NOPE:base + MLA_QK] * cos_r
               + rot_t[h * MLA_ROPE:(h + 1) * MLA_ROPE] * sin_r) * scale).astype(BF16)
        q_t = jnp.concatenate([nope, pe, zpad], axis=0)
        for kt in range(tm // KEY_TILE):
            qt_out[0, kt, base:base + HEAD_PAD, :] = q_t[:, kt * KEY_TILE:(kt + 1) * KEY_TILE]

    k1 = kr_t[:half]
    k2 = kr_t[half:MLA_ROPE]
    kpe_t = jnp.concatenate([jnp.zeros((MLA_NOPE, tm), F32),
                             k1 * cos_h - k2 * sin_h, k1 * sin_h + k2 * cos_h,
                             jnp.zeros((HEAD_PAD - MLA_QK, tm), F32)], axis=0)
    kpe = kpe_t.T
    ka = jnp.dot(kvn.astype(BF16), wk_ref[...], preferred_element_type=F32)
    for h in range(MLA_HEADS):
        sl = slice(h * HEAD_PAD, (h + 1) * HEAD_PAD)
        k_out[0, :, sl] = (ka[:, sl] + kpe).astype(BF16)

    vt = jnp.dot(wvt_ref[...], kvn_t, preferred_element_type=F32)
    vrow = lax.broadcasted_iota(jnp.int32, (MLA_HEADS * V_ROWS, 1), 0) % V_ROWS
    vt = (vt + jnp.where(vrow == MLA_V, 1.0, 0.0)).astype(BF16)
    for kt in range(tm // KEY_TILE):
        vt_out[0, kt] = vt[:, kt * KEY_TILE:(kt + 1) * KEY_TILE]


def _mla_weights(w_uq, w_ukv):
    half = MLA_ROPE // 2
    wq = w_uq.reshape(MLA_Q_RANK, MLA_HEADS, MLA_QK)
    zq = jnp.zeros((MLA_Q_RANK, MLA_HEADS, HEAD_PAD - MLA_QK), F32)
    wq_plain_t = jnp.concatenate([wq, zq], -1).reshape(MLA_Q_RANK, -1).T.astype(BF16)
    t1 = wq[:, :, MLA_NOPE:MLA_NOPE + half]
    t2 = wq[:, :, MLA_NOPE + half:]
    wq_rot_t = jnp.concatenate([-t2, t1], -1).reshape(MLA_Q_RANK, -1).T.astype(BF16)

    wkv = w_ukv.reshape(MLA_KV_RANK, MLA_HEADS, MLA_NOPE + MLA_V)
    zk = jnp.zeros((MLA_KV_RANK, MLA_HEADS, HEAD_PAD - MLA_NOPE), F32)
    wk = jnp.concatenate([wkv[:, :, :MLA_NOPE], zk], -1).reshape(MLA_KV_RANK, -1).astype(BF16)
    zv = jnp.zeros((MLA_KV_RANK, MLA_HEADS, V_ROWS - MLA_V), F32)
    wvt = jnp.concatenate([wkv[:, :, MLA_NOPE:], zv], -1).reshape(MLA_KV_RANK, -1).T.astype(BF16)
    return wq_plain_t, wq_rot_t, wk, wvt


ATTN_SUBTILES = 2
ATTN_PAIRS_PER_ITER = 2


def _attn_kernel(qt_ref, k_ref, vt_ref, o_ref, s_buf):
    ts = KEY_TILE
    nsub = ATTN_SUBTILES
    n_qt = o_ref.shape[1] // (nsub * ts)
    key_chunk = lax.broadcasted_iota(jnp.int32, (ts, ts), 0) // ATTN_CHUNK
    qry_chunk = lax.broadcasted_iota(jnp.int32, (ts, ts), 1) // ATTN_CHUNK
    diag_mask = key_chunk <= qry_chunk
    lanes = [slice(hh * HEAD_PAD, (hh + 1) * HEAD_PAD) for hh in range(2)]
    chains = [(hh, sb) for hh in range(2) for sb in range(nsub)]

    def scores(i, j, slot, first_sub, masked_sub):
        if isinstance(j, int):
            start = j * ts
        else:
            start = pl.multiple_of(j * ts, ts)
        kj = [k_ref[0, pl.ds(start, ts), lanes[hh]] for hh in range(2)]
        tile_max = {}
        for n, ch in enumerate(chains):
            if ch[1] >= first_sub:
                q_t = qt_ref[0, i * nsub + ch[1], lanes[ch[0]], :]
                s = jnp.dot(kj[ch[0]], q_t, preferred_element_type=F32)
                if ch[1] == masked_sub:
                    s = jnp.where(diag_mask, s, NEG_BIG)
                s_buf[slot, n] = s
                tile_max[ch] = jnp.max(s, axis=0, keepdims=True)
        return tile_max

    def consume(j, slot, carry, tile_max, first_sub):
        vj = [vt_ref[0, j, hh * V_ROWS:(hh + 1) * V_ROWS, :] for hh in range(2)]
        active = [(n, ch) for n, ch in enumerate(chains) if ch[1] >= first_sub]
        m_new = {ch: jnp.maximum(carry[ch][0], tile_max[ch]) for _, ch in active}
        alpha = {ch: jnp.exp2(carry[ch][0] - m_new[ch]) for _, ch in active}
        out = dict(carry)
        for n, ch in active:
            p = jnp.exp2(s_buf[slot, n] - m_new[ch]).astype(BF16)
            pv = jnp.dot(vj[ch[0]], p, preferred_element_type=F32)
            out[ch] = (m_new[ch], carry[ch][1] * alpha[ch] + pv)
        return out

    head_max = scores(0, 0, 0, 0, 0)
    for i in range(n_qt):
        sa, sb_ = (0, 1) if i % 2 == 0 else (2, 3)
        next_sa = 2 if i % 2 == 0 else 0

        def pair_step(t, carry, max_a, on_diag, i=i, sa=sa, sb_=sb_):
            max_b = scores(i, 2 * t + 1, sb_, 0, -1)
            carry = consume(2 * t, sa, carry, max_a, 0)
            max_a = scores(i, 2 * t + 2, sa, 0, 0 if on_diag else -1)
            return consume(2 * t + 1, sb_, carry, max_b, 0), max_a

        def loop_body(t, flat):
            carry = {ch: (flat[3 * n], flat[3 * n + 1]) for n, ch in enumerate(chains)}
            max_a = {ch: flat[3 * n + 2] for n, ch in enumerate(chains)}
            for u in range(ATTN_PAIRS_PER_ITER):
                carry, max_a = pair_step(ATTN_PAIRS_PER_ITER * t + u, carry, max_a, False)
            return tuple(v for ch in chains for v in (*carry[ch], max_a[ch]))

        n_iter, n_rest = divmod(max(i - 1, 0), ATTN_PAIRS_PER_ITER)
        init = tuple(v for ch in chains
                     for v in (jnp.full((1, ts), NEG_BIG, F32), jnp.zeros((V_ROWS, ts), F32), head_max[ch]))
        flat = lax.fori_loop(0, n_iter, loop_body, init)
        carry = {ch: (flat[3 * n], flat[3 * n + 1]) for n, ch in enumerate(chains)}
        max_a = {ch: flat[3 * n + 2] for n, ch in enumerate(chains)}
        for u in range(n_rest):
            carry, max_a = pair_step(ATTN_PAIRS_PER_ITER * n_iter + u, carry, max_a, False)
        if i >= 1:
            carry, max_a = pair_step(i - 1, carry, max_a, True)
        max_b = scores(i, 2 * i + 1, sb_, 1, 1)
        if i + 1 < n_qt:
            head_max = scores(i + 1, 0, next_sa, 0, -1)
        carry = consume(2 * i, sa, carry, max_a, 0)
        carry = consume(2 * i + 1, sb_, carry, max_b, 1)

        for sb in range(nsub):
            normed = []
            for hh in range(2):
                acc = carry[(hh, sb)][1]
                normed.append(acc[:MLA_V] / acc[MLA_V:MLA_V + 1])
            out_t = jnp.concatenate(normed, axis=0)
            row0 = (i * nsub + sb) * ts
            o_ref[0, row0:row0 + ts, :] = out_t.T.astype(o_ref.dtype)


def _attention(qt, k, vt):
    b, s, _ = k.shape
    pairs = MLA_HEADS // 2
    nsub = ATTN_SUBTILES
    return pl.pallas_call(
        _attn_kernel,
        grid=(b, pairs),
        in_specs=[pl.BlockSpec((1, s // KEY_TILE, 2 * HEAD_PAD, KEY_TILE), lambda bi, g: (bi, 0, g, 0)),
                  pl.BlockSpec((1, s, 2 * HEAD_PAD), lambda bi, g: (bi, 0, g)),
                  pl.BlockSpec((1, s // KEY_TILE, 2 * V_ROWS, KEY_TILE), lambda bi, g: (bi, 0, g, 0))],
        out_specs=pl.BlockSpec((1, s, 2 * MLA_V), lambda bi, g: (bi, 0, g)),
        out_shape=jax.ShapeDtypeStruct((b, s, MLA_WIDTH), BF16),
        scratch_shapes=[pltpu.VMEM((4, 2 * nsub, KEY_TILE, KEY_TILE), F32)],
        compiler_params=_params("parallel", "parallel"),
    )(qt, k, vt)


def _lane_head(width):
    return lax.broadcasted_iota(jnp.int32, (1, width), 1) // RWKV_HEAD


def _bd_stack(x):
    head = _lane_head(x.shape[1])
    xb = x.astype(BF16)
    return jnp.concatenate([jnp.where(head == h, xb, jnp.zeros_like(xb)) for h in range(PACK_HEADS)], axis=0)


def _pdot(a, b):
    return jnp.dot(a.astype(BF16), _bd_stack(b), preferred_element_type=F32)


def _pdot2(a, b1, b2):
    rhs = jnp.concatenate([_bd_stack(b1), _bd_stack(b2)], axis=1)
    out = jnp.dot(a.astype(BF16), rhs, preferred_element_type=F32)
    return out[:, :PACK_LANES], out[:, PACK_LANES:]


def _pdot_nt(a, b):
    return lax.dot_general(a.astype(BF16), _bd_stack(b), (((1,), (1,)), ((), ())),
                           preferred_element_type=F32)


def _split2(x):
    x1 = x.astype(BF16)
    x2 = (x - x1.astype(F32)).astype(BF16)
    return x1, x2


MXU_TILE = 256


def _head_ones():
    hid = np.arange(MXU_TILE) // RWKV_HEAD
    return jnp.asarray((hid[:, None] == hid[None, :]).astype(np.float32), BF16)


def _head_sums(t, ones_tile):
    tb = t.astype(BF16)
    return jnp.concatenate(
        [jnp.dot(tb[:, o:o + MXU_TILE], ones_tile, preferred_element_type=F32)
         for o in range(0, t.shape[1], MXU_TILE)], axis=1)


def _wkv_kernel(rw_ref, wlora_ref, w0_ref, a0_ref, kk_ref, ka_ref, rk_ref,
                ones_ref, tri_ref, y_ref, bonus_ref, s_ref, rhat_s, y0_s, p_s, q_s, *, tm, nt):
    g = pl.program_id(0)
    c = WKV_CHUNK
    w = RWKV_WIDTH
    nch = tm // c
    packs = range(w // PACK_LANES)
    lanes_of = lambda gk: slice(gk * PACK_LANES, (gk + 1) * PACK_LANES)

    @pl.when(g == 0)
    def _():
        for ref in (s_ref, rhat_s, y0_s, p_s, q_s):
            ref[...] = jnp.zeros_like(ref)

    scan_starts_sequence = (jnp.maximum(g - 1, 0) % nt) == 0
    scan = {"state": {gk: jnp.where(scan_starts_sequence, 0.0, s_ref[:, lanes_of(gk)]) for gk in packs},
            "chunk": 0}

    def advance_scan():
        cc = scan["chunk"]
        if cc >= nch:
            return
        state = scan["state"]
        rs = slice(cc * c, (cc + 1) * c)
        for gk in packs:
            ls = lanes_of(gk)
            y_ref[0, rs, ls] = (_pdot_nt(rhat_s[rs, ls], state[gk]) + y0_s[rs, ls]).astype(y_ref.dtype)
        state = {gk: jnp.dot(state[gk].astype(BF16), p_s[cc * PACK_LANES:(cc + 1) * PACK_LANES, lanes_of(gk)],
                             preferred_element_type=F32) + q_s[rs, lanes_of(gk)] for gk in packs}
        scan["state"] = state
        scan["chunk"] = cc + 1
        if cc + 1 == nch:
            for gk in packs:
                s_ref[:, lanes_of(gk)] = state[gk]

    u = rw_ref[0].astype(F32)
    r = u[:, :w]
    k = u[:, w:2 * w]
    v = u[:, 2 * w:3 * w]
    lora_in = u[:, 3 * w:]
    lane = lax.broadcasted_iota(jnp.int32, (1, DECAY_LORA + ICLR_LORA), 1)
    lora_in = jnp.where(lane < DECAY_LORA, jnp.tanh(lora_in), lora_in)
    lora = jnp.dot(lora_in.astype(BF16), wlora_ref[...], preferred_element_type=F32)
    c_lw = -0.5 * math.exp(-0.5) * math.log2(math.e)
    lw = c_lw + c_lw * jnp.tanh(w0_ref[...] + lora[:, :w])
    a = 0.5 + 0.5 * jnp.tanh(a0_ref[...] + lora[:, w:])

    ones_tile = ones_ref[...]
    headsum = lambda t: _head_sums(t, ones_tile)
    kk = k * kk_ref[...]
    kk = kk * lax.rsqrt(jnp.maximum(headsum(kk * kk), 1e-24))
    k = k * (1.0 + (a - 1.0) * ka_ref[...])
    bonus_ref[0] = (headsum(r * k * rk_ref[...]) * v).astype(bonus_ref.dtype)
    a_vec = -kk
    b_vec = kk * a

    tri = tri_ref[...]
    cum = sum(jnp.dot(tri, part, preferred_element_type=F32) for part in _split2(lw))

    t_idx = lax.broadcasted_iota(jnp.int32, (c, PACK_LANES), 0)
    s_idx = lax.broadcasted_iota(jnp.int32, (c, PACK_LANES), 1) % RWKV_HEAD
    strict = s_idx < t_idx
    incl = s_idx <= t_idx
    eye = (s_idx == t_idx).astype(F32)
    prow = lax.broadcasted_iota(jnp.int32, (PACK_LANES, PACK_LANES), 0)
    pcol = lax.broadcasted_iota(jnp.int32, (PACK_LANES, PACK_LANES), 1)
    same_head = (prow // RWKV_HEAD) == (pcol // RWKV_HEAD)
    on_diag = prow == pcol
    head = _lane_head(PACK_LANES)

    cum_last = jnp.concatenate(
        [jnp.broadcast_to(cum[(cc + 1) * c - 1:(cc + 1) * c, :], (c, w)) for cc in range(nch)], axis=0)
    e_pos = jnp.exp2(cum)
    e_neg = jnp.exp2(-cum)
    e_last = jnp.exp2(cum_last - cum)
    at_all = a_vec * jnp.exp2(cum - lw)
    bt_all = b_vec * e_neg
    kt_all = k * e_neg
    rt_all = r * e_pos
    bl_all = b_vec * e_last
    kl_all = k * e_last
    w_end_all = jnp.exp2(cum_last)

    probs = [(cc, gk) for cc in range(nch) for gk in packs]
    sl = {pr: (slice(pr[0] * c, (pr[0] + 1) * c), slice(pr[1] * PACK_LANES, (pr[1] + 1) * PACK_LANES))
          for pr in probs}
    at = {pr: at_all[sl[pr]] for pr in probs}
    rt = {pr: rt_all[sl[pr]] for pr in probs}
    vv = {pr: v[sl[pr]] for pr in probs}
    m_bk = {pr: lax.dot_general(
        jnp.concatenate([at[pr], rt[pr]], axis=0).astype(BF16),
        jnp.concatenate([_bd_stack(bt_all[sl[pr]]), _bd_stack(kt_all[sl[pr]])], axis=0),
        (((1,), (1,)), ((), ())), preferred_element_type=F32) for pr in probs}
    advance_scan()
    m_b = {pr: m_bk[pr][:, :PACK_LANES] for pr in probs}
    m_k = {pr: m_bk[pr][:, PACK_LANES:] for pr in probs}
    l_ab = {pr: jnp.where(strict, m_b[pr][:c], 0.0) for pr in probs}
    t_inv = {pr: eye + l_ab[pr] for pr in probs}
    x = {pr: _pdot(l_ab[pr], l_ab[pr]) for pr in probs}
    levels = int(math.log2(c)) - 1
    for lvl in range(levels):
        advance_scan()
        if lvl < levels - 1:
            tx = {pr: _pdot2(x[pr], t_inv[pr], x[pr]) for pr in probs}
            t_inv = {pr: t_inv[pr] + tx[pr][0] for pr in probs}
            x = {pr: tx[pr][1] for pr in probs}
        else:
            t_inv = {pr: t_inv[pr] + _pdot(x[pr], t_inv[pr]) for pr in probs}
    akv = {pr: _pdot(jnp.where(strict, m_k[pr][:c], 0.0), vv[pr]) for pr in probs}
    au = {pr: _pdot2(t_inv[pr], at[pr], akv[pr]) for pr in probs}
    a_hat = {pr: au[pr][0] for pr in probs}
    u0 = {pr: au[pr][1] for pr in probs}
    while scan["chunk"] < nch:
        advance_scan()
    for pr in probs:
        rs, ls = sl[pr]
        m_rb = jnp.where(incl, m_b[pr][c:], 0.0)
        m_rk = jnp.where(incl, m_k[pr][c:], 0.0)
        r_add, y_add = _pdot2(m_rb, a_hat[pr], u0[pr])
        rhat_s[rs, ls] = (rt[pr] + r_add).astype(rhat_s.dtype)
        y0_s[rs, ls] = y_add + _pdot(m_rk, vv[pr])
    for pr in probs:
        rs, ls = sl[pr]
        bl = bl_all[sl[pr]]
        p_full = jnp.dot(a_hat[pr].T.astype(BF16), bl.astype(BF16), preferred_element_type=F32)
        w_end = w_end_all[rs.start:rs.start + 1, ls]
        p_bd = jnp.where(same_head, p_full, 0.0) + jnp.where(on_diag, w_end, 0.0)
        p_s[pr[0] * PACK_LANES:(pr[0] + 1) * PACK_LANES, ls] = p_bd.astype(p_s.dtype)
        uv_t = jnp.concatenate([u0[pr], vv[pr]], axis=0).T
        bk = jnp.concatenate([bl, kl_all[sl[pr]]], axis=0).astype(BF16)
        f = jnp.dot(uv_t.astype(BF16), bk, preferred_element_type=F32)
        q_s[rs, ls] = sum(jnp.where(head == h, f[h * RWKV_HEAD:(h + 1) * RWKV_HEAD], 0.0)
                          for h in range(PACK_HEADS))


def _wkv(rw, w0, w_decay_up, a0, w_iclr_up, k_k, k_a, r_k, tm):
    b, s, width = rw.shape
    w = RWKV_WIDTH
    nt = s // tm
    n_tiles = b * nt
    wlora = (0.5 * jnp.concatenate(
        [jnp.concatenate([w_decay_up, jnp.zeros((DECAY_LORA, w), F32)], 1),
         jnp.concatenate([jnp.zeros((ICLR_LORA, w), F32), w_iclr_up], 1)], 0)).astype(BF16)
    w0 = 0.5 * w0
    a0 = 0.5 * a0
    ones_bd = _head_ones()
    tid = np.arange(tm)
    tri = jnp.asarray(((tid[:, None] >= tid[None, :]) &
                       (tid[:, None] // WKV_CHUNK == tid[None, :] // WKV_CHUNK)).astype(np.float32), BF16)
    row = lambda t: t.reshape(1, -1)
    prep_tile = lambda g: jnp.minimum(g, n_tiles - 1)
    scan_tile = lambda g: jnp.maximum(g - 1, 0)
    rw_spec = pl.BlockSpec((1, tm, width), lambda g: (prep_tile(g) // nt, prep_tile(g) % nt, 0))
    y_spec = pl.BlockSpec((1, tm, w), lambda g: (scan_tile(g) // nt, scan_tile(g) % nt, 0))
    bonus_spec = pl.BlockSpec((1, tm, w), lambda g: (prep_tile(g) // nt, prep_tile(g) % nt, 0))
    n_chunks = tm // WKV_CHUNK
    return pl.pallas_call(
        functools.partial(_wkv_kernel, tm=tm, nt=nt),
        grid=(n_tiles + 1,),
        in_specs=[rw_spec, _const_spec(wlora.shape)]
                 + [_const_spec((1, w))] * 5 + [_const_spec(ones_bd.shape), _const_spec(tri.shape)],
        out_specs=[y_spec, bonus_spec],
        out_shape=[jax.ShapeDtypeStruct((b, s, w), BF16),
                   jax.ShapeDtypeStruct((b, s, w), BF16)],
        scratch_shapes=[pltpu.VMEM((RWKV_HEAD, w), F32),
                        pltpu.VMEM((tm, w), BF16),
                        pltpu.VMEM((tm, w), F32),
                        pltpu.VMEM((n_chunks * PACK_LANES, w), BF16),
                        pltpu.VMEM((tm, w), F32)],
        compiler_params=_params("arbitrary"),
    )(rw, wlora, row(w0), row(a0), row(k_k), row(k_a), row(r_k), ones_bd, tri)


MERGE_BLOCK = 512

def _epilogue_kernel(x_ref, gate_ref, attn_ref, ga_ref, y_ref, bonus_ref, gb_ref, ma_ref, mb_ref,
                     ones_ref, gng_ref, gnb_ref, wpa_ref, wpb_ref, wout_ref, pg_ref, pb_ref, o_ref):
    silu_half = lambda th: th + th * jnp.tanh(th)
    ones_tile = ones_ref[...]
    headmean = lambda t: _head_sums(t, ones_tile) * (1.0 / RWKV_HEAD)

    y = y_ref[0].astype(F32)
    yc = y - headmean(y)
    yn = yc * lax.rsqrt(headmean(yc * yc) + GN_EPS)
    yb = yn * gng_ref[...] + gnb_ref[...] + bonus_ref[0].astype(F32)
    yb = yb * silu_half(gb_ref[0].astype(F32))
    ya = attn_ref[0].astype(F32) * silu_half(ga_ref[0].astype(F32))
    ya_b = ya.astype(BF16)
    yb_b = yb.astype(BF16)
    blocks = []
    for c0 in range(0, D_MODEL, MERGE_BLOCK):
        cs = slice(c0, c0 + MERGE_BLOCK)
        ya_p = jnp.dot(ya_b, wpa_ref[:, cs], preferred_element_type=F32)
        yb_p = jnp.dot(yb_b, wpb_ref[:, cs], preferred_element_type=F32)
        blocks.append(((1.0 + jnp.tanh(ma_ref[0, :, cs].astype(F32))) * ya_p
                       + (1.0 + jnp.tanh(mb_ref[0, :, cs].astype(F32))) * yb_p).astype(BF16))
    merged2 = jnp.concatenate(blocks, axis=1)
    sub2 = jnp.dot(merged2, wout_ref[...], preferred_element_type=F32)
    z = ALPHA * x_ref[0] + (0.5 * (1.0 + gate_ref[0])) * sub2
    zc = z - jnp.mean(z, -1, keepdims=True)
    zn = zc * lax.rsqrt(jnp.mean(zc * zc, -1, keepdims=True) + LN_EPS)
    o_ref[0] = (zn * pg_ref[...] + pb_ref[...]).astype(o_ref.dtype)


def _epilogue(x, gate, attn, ga, y, bonus, gb, ma, mb, gn_g, gn_b, w_proj_a, w_proj_b, w_out,
              post_g, post_b, tm):
    b, s, d = x.shape
    w = RWKV_WIDTH
    ones_bd = _head_ones()
    row = lambda t: t.reshape(1, -1)
    tok = lambda wd: pl.BlockSpec((1, tm, wd), lambda bi, i: (bi, i, 0))
    vec = pl.BlockSpec((1, 1, d), lambda bi, i: (bi, 0, 0))
    return pl.pallas_call(
        _epilogue_kernel,
        grid=(b, s // tm),
        in_specs=[tok(d), vec, tok(w), tok(w), tok(w), tok(w), tok(w), tok(d), tok(d),
                  _const_spec(ones_bd.shape), _const_spec((1, w)), _const_spec((1, w)),
                  _const_spec((w, d)), _const_spec((w, d)), _const_spec((d, d)),
                  _const_spec((1, d)), _const_spec((1, d))],
        out_specs=tok(d),
        out_shape=jax.ShapeDtypeStruct((b, s, d), x.dtype),
        compiler_params=_params("parallel", "parallel"),
    )(x, gate, attn, ga, y, bonus, gb, ma, mb, ones_bd, row(gn_g), row(gn_b),
      w_proj_a.astype(BF16), w_proj_b.astype(BF16), w_out.astype(BF16), row(post_g), row(post_b))


def _layer(x, c, pos, w_ada, b_ada, w_in, q_norm_g, w_uq, kv_norm_g, w_ukv, mu_rwkv, w0, w_decay_up,
           a0, w_iclr_up, k_k, k_a, r_k, gn_g, gn_b, w_proj_a, w_proj_b, w_out, post_g, post_b):
    b, s, d = x.shape
    tm = min(512, s)
    ada = _adaln(c, w_ada, b_ada)
    shift, scale, gate = (ada[:, j * d:(j + 1) * d].reshape(b, 1, d) for j in range(3))
    qt, k, vt, ga, rw, gb, ma, mb = _ln_proj(x, scale, shift, w_in.astype(BF16), pos, q_norm_g, kv_norm_g,
                                             w_uq, w_ukv, mu_rwkv, tm)
    attn = _attention(qt, k, vt)
    y, bonus = _wkv(rw, w0, w_decay_up, a0, w_iclr_up, k_k, k_a, r_k.reshape(-1), min(512, s))
    return _epilogue(x, gate, attn, ga, y, bonus, gb, ma, mb, gn_g, gn_b, w_proj_a, w_proj_b, w_out,
                     post_g, post_b, min(1024, s))


def kernel(x, c, positions, w_ada, b_ada, w_in, q_norm_g, w_uq, kv_norm_g, w_ukv, mu_rwkv, w0,
           w_decay_up, a0, w_iclr_up, k_k, k_a, r_k, gn_g, gn_b, w_proj_a, w_proj_b, w_out, post_g,
           post_b):
    pos = positions.astype(F32)[:, None, :]
    for l in range(w_ada.shape[0]):
        x = _layer(x, c, pos, w_ada[l], b_ada[l], w_in[l], q_norm_g[l], w_uq[l], kv_norm_g[l],
                   w_ukv[l], mu_rwkv[l], w0[l], w_decay_up[l], a0[l], w_iclr_up[l], k_k[l], k_a[l],
                   r_k[l], gn_g[l], gn_b[l], w_proj_a[l], w_proj_b[l], w_out[l], post_g[l], post_b[l])
    return x
```
